```python
import math
import jax, jax.numpy as jnp
from jax import lax
import numpy as np

D_MODEL = 1024
BATCH = 2
SEQ = 8192
DEPTH = 1
DEC_BATCH = 128
DEC_SEQ = 4
PAST_LEN = 2048
PAGE_SIZE = 128

HEAD_DIM = 64
MOBA_HEADS = D_MODEL // 128
MOBA_WIDTH = MOBA_HEADS * HEAD_DIM
MOBA_BLOCK = 256
MOBA_TOPK = 3
MOBA_Q_CHUNK = 128
RWKV_HEADS = D_MODEL // 128
RWKV_WIDTH = RWKV_HEADS * HEAD_DIM
DECAY_LORA = 64
AAA_LORA = 64
GATE_LORA = 128
RWKV_PROJ = 3 * RWKV_WIDTH + DECAY_LORA + AAA_LORA + GATE_LORA
IN_PROJ = 3 * MOBA_WIDTH + RWKV_PROJ + 2 * D_MODEL
IN_SPLITS = [MOBA_WIDTH, 2 * MOBA_WIDTH, 3 * MOBA_WIDTH, 3 * MOBA_WIDTH + RWKV_PROJ, 3 * MOBA_WIDTH + RWKV_PROJ + D_MODEL]
RWKV_SPLITS = [RWKV_WIDTH, 2 * RWKV_WIDTH, 3 * RWKV_WIDTH, 3 * RWKV_WIDTH + DECAY_LORA, 3 * RWKV_WIDTH + DECAY_LORA + AAA_LORA]
D_FF = 256 * math.ceil(8 * D_MODEL / 3 / 256)
RMS_EPS = 1e-6
GN_EPS = 64e-5

kernel_name = 'hybrid_moba_rwkv7_macaron_step'


def rms_norm(x, g):
    xf = x.astype(jnp.float32)
    y = xf * lax.rsqrt(jnp.mean(xf * xf, axis=-1, keepdims=True) + RMS_EPS)
    return (y * g.astype(jnp.float32)).astype(x.dtype)


def half_ffn(x, g_pre, g_post, w_gate, w_up, w_down):
    h = rms_norm(x, g_pre)
    return x + 0.5 * rms_norm((jax.nn.silu(h @ w_gate) * (h @ w_up)) @ w_down, g_post)


def alibi_slopes():
    return jnp.asarray(np.power(2.0, -8.0 * np.arange(1, MOBA_HEADS + 1) / MOBA_HEADS), dtype=jnp.float32)


def moba_attend(q, k, v, q_pos, q_chunk):
    B, H, Q, Dh = q.shape
    L = k.shape[2]
    nb = max(-(-L // MOBA_BLOCK), MOBA_TOPK)
    pad = nb * MOBA_BLOCK - L
    k_blk = jnp.pad(k, ((0, 0), (0, 0), (0, pad), (0, 0))).reshape(B, H, nb, MOBA_BLOCK, Dh)
    v_blk = jnp.pad(v, ((0, 0), (0, 0), (0, pad), (0, 0))).reshape(B, H, nb, MOBA_BLOCK, Dh)
    k_mean = jnp.mean(k_blk.astype(jnp.float32), axis=3)
    slopes = alibi_slopes()[None, :, None, None, None]
    blk_ids = jnp.arange(nb)
    slot = jnp.arange(MOBA_TOPK + 1)
    offs = jnp.arange(MOBA_BLOCK)
    bi = jnp.arange(B)[:, None, None, None]
    hi = jnp.arange(H)[None, :, None, None]
    scale = HEAD_DIM ** -0.5

    def attend_chunk(args):
        q_c, pos_c = args
        qc = q_c.shape[2]
        qb = pos_c // MOBA_BLOCK
        gate = jnp.einsum('bhqd,bhnd->bhqn', q_c.astype(jnp.float32), k_mean)
        gate = jnp.where(blk_ids[None, None, None, :] < qb[None, None, :, None], gate, -jnp.inf)
        _, top = lax.top_k(gate, MOBA_TOPK)
        own = jnp.broadcast_to(qb[None, None, :, None], (B, H, qc, 1)).astype(top.dtype)
        sel = jnp.concatenate([top, own], axis=-1)
        k_sel = k_blk[bi, hi, sel]
        v_sel = v_blk[bi, hi, sel]
        k_pos = sel[..., None] * MOBA_BLOCK + offs
        slot_ok = (slot[None, :] < qb[:, None]) | (slot[None, :] == MOBA_TOPK)
        valid = (k_pos <= pos_c[None, None, :, None, None]) & slot_ok[None, None, :, :, None]
        s = jnp.einsum('bhqd,bhqnkd->bhqnk', q_c, k_sel).astype(jnp.float32) * scale
        s = s - slopes * (pos_c[None, None, :, None, None] - k_pos).astype(jnp.float32)
        s = jnp.where(valid, s, -jnp.inf)
        p = jax.nn.softmax(s.reshape(B, H, qc, -1), axis=-1).reshape(s.shape)
        return jnp.einsum('bhqnk,bhqnkd->bhqd', p.astype(v_sel.dtype), v_sel)

    n_chunks = Q // q_chunk
    qs = jnp.moveaxis(q.reshape(B, H, n_chunks, q_chunk, Dh), 2, 0)
    ps = q_pos.reshape(n_chunks, q_chunk)
    out = lax.map(attend_chunk, (qs, ps))
    return jnp.moveaxis(out, 0, 2).reshape(B, H, Q, Dh)


def rwkv_scan(r, w, k, v, kk, a, s0):
    def step(S, inp):
        r_t, w_t, k_t, v_t, kk_t, a_t = inp
        s_kk = jnp.einsum('bhvk,bhk->bhv', S, kk_t)
        S = S * w_t[:, :, None, :] - s_kk[..., None] * (kk_t * a_t)[:, :, None, :] + v_t[..., None] * k_t[:, :, None, :]
        return S, jnp.einsum('bhvk,bhk->bhv', S, r_t)
    xs = tuple(jnp.moveaxis(t, 1, 0) for t in (r, w, k, v, kk, a))
    s_fin, o = lax.scan(step, s0, xs)
    return jnp.moveaxis(o, 0, 1), s_fin


def rwkv_branch(p, prev, wkv0, mu, w0, w2, a0, a2, g2, k_k, k_a, r_k, ln_w, ln_b):
    B, T, _ = p.shape
    f32 = jnp.float32
    shifted = jnp.concatenate([prev[:, None, :], p[:, :-1]], axis=1)
    m = p + (shifted - p) * mu
    r, k, v, wd, ad, gd = jnp.split(m, RWKV_SPLITS, axis=-1)
    w = -jax.nn.softplus(-(w0 + jnp.tanh(wd) @ w2)) - 0.5
    decay = jnp.exp(-jnp.exp(w.astype(f32)))
    a = jax.nn.sigmoid(a0 + ad @ a2)
    g = jax.nn.sigmoid(gd) @ g2
    hs = lambda t: t.reshape(B, T, RWKV_HEADS, HEAD_DIM).astype(f32)
    kk = hs(k * k_k)
    kk = kk / jnp.maximum(jnp.sqrt(jnp.sum(kk * kk, axis=-1, keepdims=True)), 1e-12)
    kh = hs(k * (1.0 + (a - 1.0) * k_a))
    rh, vh, ah, wh = hs(r), hs(v), hs(a), hs(decay)
    o, s_fin = rwkv_scan(rh, wh, kh, vh, kk, ah, wkv0.astype(f32))
    mean = jnp.mean(o, axis=-1, keepdims=True)
    var = jnp.mean(jnp.square(o - mean), axis=-1, keepdims=True)
    on = ((o - mean) * lax.rsqrt(var + GN_EPS)).reshape(B, T, RWKV_WIDTH) * ln_w.astype(f32) + ln_b.astype(f32)
    r_k_h = r_k.astype(f32).reshape(1, 1, RWKV_HEADS, HEAD_DIM)
    bonus = jnp.sum(rh * kh * r_k_h, axis=-1, keepdims=True) * vh
    out = (on + bonus.reshape(B, T, RWKV_WIDTH)).astype(p.dtype) * g
    return out, s_fin.astype(wkv0.dtype), p[:, -1]


def layer(x, k_past, v_past, prev_shift, wkv0, q_chunk, w):
    (g1a, g1b, f1g, f1u, f1d, gma, gmb, w_in, w_ba, w_bb, w_out,
     mu, w0, w2, a0, a2, g2, k_k, k_a, r_k, ln_w, ln_b,
     g2a, g2b, f2g, f2u, f2d) = w
    h = half_ffn(x, g1a, g1b, f1g, f1u, f1d)
    u = rms_norm(h, gma)
    B, T, _ = u.shape
    P = k_past.shape[1]
    q, k, v, p_rw, gate_a, gate_b = jnp.split(u @ w_in, IN_SPLITS, axis=-1)
    heads = lambda t: t.reshape(B, T, MOBA_HEADS, HEAD_DIM)
    q, k, v = heads(q), heads(k), heads(v)
    k_all = jnp.concatenate([k_past.astype(k.dtype), k], axis=1)
    v_all = jnp.concatenate([v_past.astype(v.dtype), v], axis=1)
    q_pos = P + jnp.arange(T, dtype=jnp.int32)
    o_a = moba_attend(q.transpose(0, 2, 1, 3), k_all.transpose(0, 2, 1, 3), v_all.transpose(0, 2, 1, 3), q_pos, q_chunk)
    o_a = o_a.transpose(0, 2, 1, 3).reshape(B, T, MOBA_WIDTH)
    o_b, wkv, last = rwkv_branch(p_rw, prev_shift.astype(p_rw.dtype), wkv0, mu, w0, w2, a0, a2, g2, k_k, k_a, r_k, ln_w, ln_b)
    merged = jax.nn.sigmoid(gate_a) * (o_a @ w_ba) + jax.nn.sigmoid(gate_b) * (o_b @ w_bb)
    h = h + rms_norm(merged @ w_out, gmb)
    y = half_ffn(h, g2a, g2b, f2g, f2u, f2d)
    return y, k, v, wkv, last


def setup_inputs(seed: int = 0) -> dict:
    key = jax.random.key(seed)
    ks = iter(jax.random.split(key, 40))
    f32 = jnp.float32

    def nrm(shape, scale=1.0):
        return jax.random.normal(next(ks), shape, f32) * scale

    def gain(n):
        return 1.0 + nrm((DEPTH, n), 0.05)

    n_pages = PAST_LEN // PAGE_SIZE
    n_used = DEC_BATCH * n_pages
    n_pool = (n_used * 5) // 4
    d = {}
    d['x_prompt'] = nrm((BATCH, SEQ, D_MODEL))
    d['x_sample'] = nrm((DEC_BATCH, DEC_SEQ, D_MODEL))
    d['cache_k'] = nrm((DEPTH, n_pool, PAGE_SIZE, MOBA_HEADS, HEAD_DIM))
    d['cache_v'] = nrm((DEPTH, n_pool, PAGE_SIZE, MOBA_HEADS, HEAD_DIM))
    d['state_wkv'] = nrm((DEPTH, DEC_BATCH, RWKV_HEADS, HEAD_DIM, HEAD_DIM), 0.1)
    d['state_shift'] = nrm((DEPTH, DEC_BATCH, RWKV_PROJ))
    d['page_table'] = jax.random.permutation(next(ks), n_pool)[:n_used].reshape(DEC_BATCH, n_pages).astype(jnp.int32)
    d['g_ffn1_pre'] = gain(D_MODEL)
    d['g_ffn1_post'] = gain(D_MODEL)
    d['w_ffn1_gate'] = nrm((DEPTH, D_MODEL, D_FF), D_MODEL ** -0.5)
    d['w_ffn1_up'] = nrm((DEPTH, D_MODEL, D_FF), D_MODEL ** -0.5)
    d['w_ffn1_down'] = nrm((DEPTH, D_FF, D_MODEL), D_FF ** -0.5)
    d['g_mix_pre'] = gain(D_MODEL)
    d['g_mix_post'] = gain(D_MODEL)
    d['w_in'] = nrm((DEPTH, D_MODEL, IN_PROJ), D_MODEL ** -0.5)
    d['w_branch_a'] = nrm((DEPTH, MOBA_WIDTH, D_MODEL), MOBA_WIDTH ** -0.5)
    d['w_branch_b'] = nrm((DEPTH, RWKV_WIDTH, D_MODEL), RWKV_WIDTH ** -0.5)
    d['w_out'] = nrm((DEPTH, D_MODEL, D_MODEL), D_MODEL ** -0.5)
    d['rwkv_mu'] = jax.random.uniform(next(ks), (DEPTH, RWKV_PROJ), f32, 0.0, 1.0)
    d['rwkv_w0'] = jax.random.uniform(next(ks), (DEPTH, RWKV_WIDTH), f32, -6.0, -1.0)
    d['rwkv_w2'] = nrm((DEPTH, DECAY_LORA, RWKV_WIDTH), DECAY_LORA ** -0.5)
    d['rwkv_a0'] = nrm((DEPTH, RWKV_WIDTH), 0.1)
    d['rwkv_a2'] = nrm((DEPTH, AAA_LORA, RWKV_WIDTH), AAA_LORA ** -0.5)
    d['rwkv_g2'] = nrm((DEPTH, GATE_LORA, RWKV_WIDTH), GATE_LORA ** -0.5)
    d['rwkv_k_k'] = 0.85 + nrm((DEPTH, RWKV_WIDTH), 0.05)
    d['rwkv_k_a'] = 1.0 + nrm((DEPTH, RWKV_WIDTH), 0.05)
    d['rwkv_r_k'] = nrm((DEPTH, RWKV_WIDTH), 0.1)
    d['rwkv_ln_w'] = gain(RWKV_WIDTH)
    d['rwkv_ln_b'] = nrm((DEPTH, RWKV_WIDTH), 0.02)
    d['g_ffn2_pre'] = gain(D_MODEL)
    d['g_ffn2_post'] = gain(D_MODEL)
    d['w_ffn2_gate'] = nrm((DEPTH, D_MODEL, D_FF), D_MODEL ** -0.5)
    d['w_ffn2_up'] = nrm((DEPTH, D_MODEL, D_FF), D_MODEL ** -0.5)
    d['w_ffn2_down'] = nrm((DEPTH, D_FF, D_MODEL), D_FF ** -0.5)
    return d


def reference(x_prompt, x_sample, cache_k, cache_v, state_wkv, state_shift, page_table,
              g_ffn1_pre, g_ffn1_post, w_ffn1_gate, w_ffn1_up, w_ffn1_down,
              g_mix_pre, g_mix_post, w_in, w_branch_a, w_branch_b, w_out,
              rwkv_mu, rwkv_w0, rwkv_w2, rwkv_a0, rwkv_a2, rwkv_g2, rwkv_k_k, rwkv_k_a, rwkv_r_k, rwkv_ln_w, rwkv_ln_b,
              g_ffn2_pre, g_ffn2_post, w_ffn2_gate, w_ffn2_up, w_ffn2_down):
    Bp, Tp, _ = x_prompt.shape
    Bs, Ts, _ = x_sample.shape
    n_pages = page_table.shape[1]
    yp, ys = x_prompt, x_sample
    kp_l, vp_l, ks_l, vs_l, wp_l, sp_l, ws_l, ss_l = [], [], [], [], [], [], [], []
    for i in range(DEPTH):
        w = (g_ffn1_pre[i], g_ffn1_post[i], w_ffn1_gate[i], w_ffn1_up[i], w_ffn1_down[i],
             g_mix_pre[i], g_mix_post[i], w_in[i], w_branch_a[i], w_branch_b[i], w_out[i],
             rwkv_mu[i], rwkv_w0[i], rwkv_w2[i], rwkv_a0[i], rwkv_a2[i], rwkv_g2[i], rwkv_k_k[i], rwkv_k_a[i],
             rwkv_r_k[i], rwkv_ln_w[i], rwkv_ln_b[i],
             g_ffn2_pre[i], g_ffn2_post[i], w_ffn2_gate[i], w_ffn2_up[i], w_ffn2_down[i])
        empty = jnp.zeros((Bp, 0, MOBA_HEADS, HEAD_DIM), x_prompt.dtype)
        yp, kp, vp, wkvp, shp = layer(
            yp, empty, empty, jnp.zeros((Bp, RWKV_PROJ), x_prompt.dtype),
            jnp.zeros((Bp, RWKV_HEADS, HEAD_DIM, HEAD_DIM), state_wkv.dtype), min(MOBA_Q_CHUNK, Tp), w)
        k_past = cache_k[i][page_table].reshape(Bs, n_pages * PAGE_SIZE, MOBA_HEADS, HEAD_DIM)
        v_past = cache_v[i][page_table].reshape(Bs, n_pages * PAGE_SIZE, MOBA_HEADS, HEAD_DIM)
        ys, ksn, vsn, wkvs, shs = layer(ys, k_past, v_past, state_shift[i], state_wkv[i], 1, w)
        kp_l.append(kp); vp_l.append(vp); ks_l.append(ksn); vs_l.append(vsn)
        wp_l.append(wkvp); sp_l.append(shp); ws_l.append(wkvs); ss_l.append(shs)
    return (yp, ys, jnp.stack(kp_l), jnp.stack(vp_l), jnp.stack(ks_l), jnp.stack(vs_l),
            jnp.stack(wp_l), jnp.stack(sp_l), jnp.stack(ws_l), jnp.stack(ss_l))
```

```python
import functools
import math

import jax
import jax.numpy as jnp
import numpy as np
from jax import lax
from jax.experimental import pallas as pl
from jax.experimental.pallas import tpu as pltpu

HEAD_DIM = 64
LANES = 128
HEADS_PER_TILE = LANES // HEAD_DIM
MOBA_BLOCK = 256
MOBA_TOPK = 3
RMS_EPS = 1e-6
GN_EPS = 64e-5
VMEM_LIMIT = 56 * 1024 * 1024
NEG_INF = float("-inf")

f32 = jnp.float32
bf16 = jnp.bfloat16


def _params(*sem):
    return pltpu.CompilerParams(dimension_semantics=sem, vmem_limit_bytes=VMEM_LIMIT)


def _const_spec(shape):
    nd = len(shape)
    return pl.BlockSpec(shape, lambda *_: (0,) * nd, pipeline_mode=pl.Buffered(1))


def _rms(x, g):
    return x * lax.rsqrt(jnp.mean(x * x, axis=-1, keepdims=True) + RMS_EPS) * g


def _split3(x):
    hi = x.astype(bf16)
    r1 = x - hi.astype(f32)
    mid = r1.astype(bf16)
    lo = (r1 - mid.astype(f32)).astype(bf16)
    return hi, mid, lo


_NN = (((1,), (0,)), ((), ()))
_NT = (((1,), (1,)), ((), ()))
_TN = (((0,), (0,)), ((), ()))


def _dg(a, b, dims):
    return lax.dot_general(a, b, dims, preferred_element_type=f32)


def _dot_hi(a, b, dims=_NN):
    ah, am, _ = _split3(a)
    bh, bm, _ = _split3(b)
    return _dg(ah, bh, dims) + (_dg(ah, bm, dims) + _dg(am, bh, dims))


def _dot_exact_lhs(a_bf, b, dims=_NN):
    bh, bm, bl = _split3(b)
    return _dg(a_bf, bh, dims) + (_dg(a_bf, bm, dims) + _dg(a_bf, bl, dims))


def _dot_exact_rhs(a, b_bf, dims=_NN):
    ah, am, al = _split3(a)
    return _dg(ah, b_bf, dims) + (_dg(am, b_bf, dims) + _dg(al, b_bf, dims))


def _ffn_kernel(x_ref, gpre_ref, gpost_ref, wg_ref, wu_ref, wd_ref, o_ref):
    x = x_ref[...]
    h = _rms(x, gpre_ref[...]).astype(bf16)
    gate = jnp.dot(h, wg_ref[...], preferred_element_type=f32)
    up = jnp.dot(h, wu_ref[...], preferred_element_type=f32)
    act = (gate * jax.nn.sigmoid(gate) * up).astype(bf16)
    y = jnp.dot(act, wd_ref[...], preferred_element_type=f32)
    o_ref[...] = x + 0.5 * _rms(y, gpost_ref[...])


def _ffn(x, g_pre, g_post, wg, wu, wd, tm):
    n, d = x.shape
    dff = wg.shape[1]
    return pl.pallas_call(
        _ffn_kernel,
        grid=(n // tm,),
        in_specs=[
            pl.BlockSpec((tm, d), lambda i: (i, 0)),
            _const_spec((1, d)), _const_spec((1, d)),
            _const_spec((d, dff)), _const_spec((d, dff)), _const_spec((dff, d)),
        ],
        out_specs=pl.BlockSpec((tm, d), lambda i: (i, 0)),
        out_shape=jax.ShapeDtypeStruct((n, d), f32),
        compiler_params=_params("parallel"),
        name="ffn",
    )(x, g_pre, g_post, wg, wu, wd)


def _in_proj_kernel(splits, h_ref, g_ref, w_ref, q_ref, k_ref, v_ref, p_ref, ga_ref, gb_ref):
    u = _rms(h_ref[...], g_ref[...]).astype(bf16)
    outs = (q_ref, k_ref, v_ref, p_ref, ga_ref, gb_ref)
    for idx, o_ref in enumerate(outs):
        lo, hi = splits[idx], splits[idx + 1]
        y = jnp.dot(u, w_ref[:, lo:hi], preferred_element_type=f32)
        if idx >= 4:
            y = jax.nn.sigmoid(y)
        o_ref[...] = y


def _in_proj(h, g, w_in, widths, tm):
    n, d = h.shape
    splits = tuple(int(s) for s in np.concatenate([[0], np.cumsum(widths)]))
    assert splits[-1] == w_in.shape[1] and all(s % LANES == 0 for s in splits)
    return pl.pallas_call(
        functools.partial(_in_proj_kernel, splits),
        grid=(n // tm,),
        in_specs=[
            pl.BlockSpec((tm, d), lambda i: (i, 0)),
            _const_spec((1, d)),
            _const_spec(w_in.shape),
        ],
        out_specs=[pl.BlockSpec((tm, w), lambda i: (i, 0)) for w in widths],
        out_shape=[jax.ShapeDtypeStruct((n, w), f32) for w in widths],
        compiler_params=_params("parallel"),
        name="in_proj",
    )(h, g, w_in)


def _alibi_slope(head, n_heads):
    return jnp.exp2(jnp.full((1, 1), head + 1, jnp.int32).astype(f32) * (-8.0 / n_heads))


def _top_blocks(gate, topk):
    col = lax.broadcasted_iota(jnp.int32, gate.shape, 1).astype(f32)
    sel = jnp.zeros(gate.shape, f32)
    big = float(gate.shape[1])
    for _ in range(topk):
        m = jnp.max(gate, axis=-1, keepdims=True)
        cand = (gate == m) & (gate > NEG_INF)
        idx = jnp.min(jnp.where(cand, col, big), axis=-1, keepdims=True)
        hit = col == idx
        sel = jnp.where(hit, 1.0, sel)
        gate = jnp.where(hit, NEG_INF, gate)
    return sel


def _kmean_kernel(k_ref, o_ref):
    nblk = o_ref.shape[0]
    k = k_ref[...]
    o_ref[...] = jnp.mean(k.reshape(nblk, MOBA_BLOCK, k.shape[-1]), axis=1)


def _block_means(k, nblk_per_step):
    n, w = k.shape
    rows = nblk_per_step * MOBA_BLOCK
    return pl.pallas_call(
        _kmean_kernel,
        grid=(n // rows,),
        in_specs=[pl.BlockSpec((rows, w), lambda i: (i, 0))],
        out_specs=pl.BlockSpec((nblk_per_step, w), lambda i: (i, 0)),
        out_shape=jax.ShapeDtypeStruct((n // MOBA_BLOCK, w), f32),
        compiler_params=_params("parallel"),
        name="moba_block_means",
    )(k)


def _moba_prompt_kernel(n_heads, q_ref, k_ref, v_ref, km_ref, o_ref):
    pair = pl.program_id(1)
    qi = pl.program_id(2)
    tq = q_ref.shape[0]
    nb = km_ref.shape[0]
    lane = lax.broadcasted_iota(jnp.int32, (1, LANES), 1)
    row = lax.broadcasted_iota(jnp.int32, (tq, MOBA_BLOCK), 0)
    col = lax.broadcasted_iota(jnp.int32, (tq, MOBA_BLOCK), 1)
    rel = (row - col).astype(f32)
    causal = col <= row
    blk_col = lax.broadcasted_iota(jnp.int32, (tq, nb), 1)

    q = q_ref[...]
    km = km_ref[...]
    k_own = k_ref[pl.ds(pl.multiple_of(qi * MOBA_BLOCK, MOBA_BLOCK), MOBA_BLOCK), :].astype(bf16)
    v_own = v_ref[pl.ds(pl.multiple_of(qi * MOBA_BLOCK, MOBA_BLOCK), MOBA_BLOCK), :].astype(bf16)

    outs = []
    for j in range(HEADS_PER_TILE):
        head_mask = (lane // HEAD_DIM == j).astype(f32)
        qm = q * head_mask
        slope = _alibi_slope(pair * HEADS_PER_TILE + j, n_heads)
        gate = _dot_hi(qm, km, _NT)
        gate = jnp.where(blk_col < qi, gate, NEG_INF)
        sel = _top_blocks(gate, MOBA_TOPK)
        qs = (qm * (HEAD_DIM ** -0.5)).astype(bf16)
        alibi = slope * rel

        s = _dg(qs, k_own, _NT) - alibi
        s = jnp.where(causal, s, NEG_INF)
        m = jnp.max(s, axis=-1, keepdims=True)
        p = jnp.exp(s - m)
        l = jnp.sum(p, axis=-1, keepdims=True)
        acc = jnp.dot(p.astype(bf16), v_own, preferred_element_type=f32)

        def body(n, carry, qs=qs, sel=sel, alibi=alibi, slope=slope):
            m, l, acc = carry
            start = pl.multiple_of(n * MOBA_BLOCK, MOBA_BLOCK)
            kb = k_ref[pl.ds(start, MOBA_BLOCK), :].astype(bf16)
            vb = v_ref[pl.ds(start, MOBA_BLOCK), :].astype(bf16)
            chosen = jnp.sum(jnp.where(blk_col == n, sel, 0.0), axis=-1, keepdims=True) > 0.5
            dist = ((qi - n) * MOBA_BLOCK).astype(f32) * slope
            s = _dg(qs, kb, _NT) - alibi - dist
            s = jnp.where(chosen, s, NEG_INF)
            m_new = jnp.maximum(m, jnp.max(s, axis=-1, keepdims=True))
            scale = jnp.exp(m - m_new)
            p = jnp.exp(s - m_new)
            l = l * scale + jnp.sum(p, axis=-1, keepdims=True)
            acc = acc * scale + jnp.dot(p.astype(bf16), vb, preferred_element_type=f32)
            return m_new, l, acc

        m, l, acc = lax.fori_loop(0, qi, body, (m, l, acc))
        outs.append(acc / l)

    o_ref[...] = jnp.where(lane // HEAD_DIM == 0, outs[0], outs[1])


def _moba_prompt(q, k, v, kmean, batch, seq, n_heads):
    n, w = q.shape
    n_pairs = w // LANES
    nb = seq // MOBA_BLOCK
    tq = MOBA_BLOCK
    return pl.pallas_call(
        functools.partial(_moba_prompt_kernel, n_heads),
        grid=(batch, n_pairs, nb),
        in_specs=[
            pl.BlockSpec((tq, LANES), lambda b, p, i: (b * nb + i, p)),
            pl.BlockSpec((seq, LANES), lambda b, p, i: (b, p)),
            pl.BlockSpec((seq, LANES), lambda b, p, i: (b, p)),
            pl.BlockSpec((nb, LANES), lambda b, p, i: (b, p)),
        ],
        out_specs=pl.BlockSpec((tq, LANES), lambda b, p, i: (b * nb + i, p)),
        out_shape=jax.ShapeDtypeStruct((n, w), f32),
        compiler_params=_params("parallel", "parallel", "arbitrary"),
        name="moba_prompt",
    )(q, k, v, kmean)


def _moba_sample_kernel(n_heads, t_new, pt_ref, qbd_ref, kn_ref, vn_ref,
                        k0_ref, k1_ref, v0_ref, v1_ref, o_ref,
                        m_sc, l_sc, g_sc, acc_sc):
    del pt_ref
    j = pl.program_id(1)
    n_blocks = pl.num_programs(1)
    rows, width = qbd_ref.shape[1], qbd_ref.shape[2]
    page = k0_ref.shape[1]
    qbd = qbd_ref[0]
    r_idx = lax.broadcasted_iota(jnp.int32, (rows, 1), 0)
    head = r_idx % n_heads
    t_q = r_idx // n_heads
    slope = jnp.exp2(-8.0 * (head + 1).astype(f32) / n_heads)
    past_len = n_blocks * MOBA_BLOCK
    pos_q = (past_len + t_q).astype(f32)
    qs = (qbd * (HEAD_DIM ** -0.5)).astype(bf16)

    def scores(k_page, first_pos):
        kcol = lax.broadcasted_iota(jnp.int32, (rows, k_page.shape[0]), 1)
        pos_k = (first_pos + kcol).astype(f32)
        return _dg(qs, k_page.astype(bf16), _NT) - slope * (pos_q - pos_k)

    k0, k1 = k0_ref[0], k1_ref[0]
    base = j * MOBA_BLOCK
    s0 = scores(k0, base)
    s1 = scores(k1, base + page)
    m = jnp.maximum(jnp.max(s0, axis=-1, keepdims=True), jnp.max(s1, axis=-1, keepdims=True))
    p0 = jnp.exp(s0 - m)
    p1 = jnp.exp(s1 - m)
    l = jnp.sum(p0, axis=-1, keepdims=True) + jnp.sum(p1, axis=-1, keepdims=True)
    acc = (jnp.dot(p0.astype(bf16), v0_ref[0].astype(bf16), preferred_element_type=f32)
           + jnp.dot(p1.astype(bf16), v1_ref[0].astype(bf16), preferred_element_type=f32))
    kmean = (jnp.sum(k0, axis=0, keepdims=True) + jnp.sum(k1, axis=0, keepdims=True)) / MOBA_BLOCK
    gate = jnp.sum(qbd * kmean, axis=-1, keepdims=True)
    m_sc[j] = jnp.broadcast_to(m, (rows, LANES))
    l_sc[j] = jnp.broadcast_to(l, (rows, LANES))
    g_sc[j] = jnp.broadcast_to(gate, (rows, LANES))
    acc_sc[j] = acc

    @pl.when(j == n_blocks - 1)
    def _():
        nb = m_sc.shape[0]
        kn = kn_ref[0]
        kcol = lax.broadcasted_iota(jnp.int32, (rows, kn.shape[0]), 1)
        s_own = _dg(qs, kn.astype(bf16), _NT) - slope * (t_q - kcol).astype(f32)
        s_own = jnp.where((kcol <= t_q) & (kcol < t_new), s_own, NEG_INF)
        m_tot = jnp.max(s_own, axis=-1, keepdims=True)
        gates = [g_sc[n][:, 0:1] for n in range(nb)]
        chosen = []
        for n in range(nb):
            rank = jnp.zeros((rows, 1), f32)
            for i in range(nb):
                if i == n:
                    continue
                ahead = (gates[i] > gates[n]) | ((gates[i] == gates[n]) & (i < n))
                rank = rank + ahead.astype(f32)
            chosen.append(rank < MOBA_TOPK)
        for n in range(nb):
            m_tot = jnp.maximum(m_tot, jnp.where(chosen[n], m_sc[n][:, 0:1], NEG_INF))
        p_own = jnp.exp(s_own - m_tot)
        l_tot = jnp.sum(p_own, axis=-1, keepdims=True)
        acc_tot = jnp.dot(p_own.astype(bf16), vn_ref[0].astype(bf16), preferred_element_type=f32)
        for n in range(nb):
            wgt = jnp.where(chosen[n], jnp.exp(m_sc[n][:, 0:1] - m_tot), 0.0)
            l_tot = l_tot + wgt * l_sc[n][:, 0:1]
            acc_tot = acc_tot + wgt * acc_sc[n]
        out = acc_tot / l_tot
        lane_head = lax.broadcasted_iota(jnp.int32, (rows, width), 1) // HEAD_DIM
        out = jnp.where(lane_head == head, out, 0.0)
        t_pad = o_ref.shape[1]
        o_ref[0] = jnp.sum(out.reshape(t_pad, n_heads, width), axis=1)


def _moba_sample(q, k_new, v_new, cache_k, cache_v, page_table, n_heads):
    batch, t_new, width = q.shape
    n_pool, page, _ = cache_k.shape
    n_pages = page_table.shape[1]
    assert MOBA_BLOCK == 2 * page and n_pages % 2 == 0 and t_new <= 8
    n_blocks = n_pages // 2
    assert n_blocks >= MOBA_TOPK
    t_pad = 8
    pad = ((0, 0), (0, t_pad - t_new), (0, 0))
    qp, kp, vp = (jnp.pad(a, pad) for a in (q, k_new, v_new))
    rows = t_pad * n_heads
    head_of_lane = jnp.arange(width) // HEAD_DIM
    head_of_row = jnp.arange(rows) % n_heads
    qbd = jnp.repeat(qp, n_heads, axis=1) * (head_of_row[:, None] == head_of_lane[None, :]).astype(f32)

    def page_spec(which):
        return pl.BlockSpec((1, page, width), lambda b, j, pt: (pt[b, 2 * j + which], 0, 0))

    row_spec = lambda r: pl.BlockSpec((1, r, width), lambda b, j, pt: (b, 0, 0))
    out = pl.pallas_call(
        functools.partial(_moba_sample_kernel, n_heads, t_new),
        grid_spec=pltpu.PrefetchScalarGridSpec(
            num_scalar_prefetch=1,
            grid=(batch, n_blocks),
            in_specs=[row_spec(rows), row_spec(t_pad), row_spec(t_pad),
                      page_spec(0), page_spec(1), page_spec(0), page_spec(1)],
            out_specs=row_spec(t_pad),
            scratch_shapes=[
                pltpu.VMEM((n_blocks, rows, LANES), f32),
                pltpu.VMEM((n_blocks, rows, LANES), f32),
                pltpu.VMEM((n_blocks, rows, LANES), f32),
                pltpu.VMEM((n_blocks, rows, width), f32),
            ],
        ),
        out_shape=jax.ShapeDtypeStruct((batch, t_pad, width), f32),
        compiler_params=_params("parallel", "arbitrary"),
        name="moba_sample",
    )(page_table, qbd, kp, vp, cache_k, cache_k, cache_v, cache_v)
    return out[:, :t_new]


def _rwkv_pre_kernel(t_valid, widths, p_ref, prev_ref, mu_ref, w0_ref, w2_ref, a0_ref, a2_ref,
                     g2_ref, kk_ref, ka_ref, ones_ref,
                     r_out, k_out, v_out, kk_out, b_out, lw_out, g_out, carry):
    ti = pl.program_id(1)
    bt, tt, width = p_ref.shape
    rw, dl, al, gl = widths

    @pl.when(ti == 0)
    def _():
        carry[...] = prev_ref[...]

    p = p_ref[...]
    t_idx = lax.broadcasted_iota(jnp.int32, (bt, tt, 1), 1)
    shifted = jnp.where(t_idx == 0, carry[...], pltpu.roll(p, 1, axis=1))
    carry[...] = p[:, tt - 1:tt, :]
    m = (p + (shifted - p) * mu_ref[...]).reshape(bt * tt, width)

    r = m[:, 0:rw]
    k = m[:, rw:2 * rw]
    v = m[:, 2 * rw:3 * rw]
    lora = m[:, 3 * rw:3 * rw + dl + al]
    gd = m[:, 3 * rw + dl + al:3 * rw + dl + al + gl]

    z = -(w0_ref[...] + _dot_hi(jnp.tanh(lora), w2_ref[...]))
    softplus = jnp.maximum(z, 0.0) + jnp.log(1.0 + jnp.exp(-jnp.abs(z)))
    w = -softplus - 0.5
    log_decay = -jnp.exp(w)
    a = jax.nn.sigmoid(a0_ref[...] + _dot_hi(lora, a2_ref[...]))
    g = _dot_hi(jax.nn.sigmoid(gd), g2_ref[...])
    kk = k * kk_ref[...]
    norm = jnp.sqrt(_dot_exact_rhs(kk * kk, ones_ref[...]))
    kk = kk / jnp.maximum(norm, 1e-12)
    kh = k * (1.0 + (a - 1.0) * ka_ref[...])

    live = (lax.broadcasted_iota(jnp.int32, (bt, tt, 1), 1) + ti * tt < t_valid)
    live = live.astype(f32).reshape(bt * tt, 1)
    shape3 = (bt, tt, rw)
    r_out[...] = (r * live).reshape(shape3)
    k_out[...] = (kh * live).reshape(shape3)
    v_out[...] = (v * live).reshape(shape3)
    kk_out[...] = (kk * live).reshape(shape3)
    b_out[...] = (kk * a * live).reshape(shape3)
    lw_out[...] = (log_decay * live).reshape(shape3)
    g_out[...] = g.reshape(shape3)


def _head_ones(width):
    idx = np.arange(width) // HEAD_DIM
    return jnp.asarray(idx[:, None] == idx[None, :], dtype=bf16)


def _rwkv_pre(p, prev, t_valid, weights, bt, tt):
    batch, t_pad, width = p.shape
    mu, w0, w2, a0, a2, g2, k_k, k_a = weights
    rw = w0.shape[-1]
    widths = (rw, w2.shape[0], a2.shape[0], g2.shape[0])
    assert 3 * rw + sum(widths[1:]) == width
    assert widths[1] + widths[2] == LANES and widths[3] % LANES == 0
    w2p = jnp.pad(w2, ((0, widths[2]), (0, 0)))
    a2p = jnp.pad(a2, ((widths[1], 0), (0, 0)))
    consts = (mu, w0, w2p, a0, a2p, g2, k_k, k_a, _head_ones(rw))
    out_spec = pl.BlockSpec((bt, tt, rw), lambda b, t: (b, t, 0))
    return pl.pallas_call(
        functools.partial(_rwkv_pre_kernel, t_valid, widths),
        grid=(batch // bt, t_pad // tt),
        in_specs=[pl.BlockSpec((bt, tt, width), lambda b, t: (b, t, 0)),
                  pl.BlockSpec((bt, 1, width), lambda b, t: (b, 0, 0))]
                 + [_const_spec(c.shape) for c in consts],
        out_specs=[out_spec] * 7,
        out_shape=[jax.ShapeDtypeStruct((batch, t_pad, rw), f32)] * 7,
        scratch_shapes=[pltpu.VMEM((bt, 1, width), f32)],
        compiler_params=_params("parallel", "arbitrary"),
        name="rwkv_pre",
    )(p, prev, *consts)


def _rwkv_scan_kernel(r_ref, k_ref, v_ref, kk_ref, b_ref, lw_ref, s0_ref, o_ref, s_out, state):
    ci = pl.program_id(1)
    bt, c, width = r_ref.shape
    n_pairs = width // LANES
    n_double = int(math.log2(c))
    assert 1 << n_double == c

    @pl.when(ci == 0)
    def _():
        state[...] = s0_ref[...]

    ri = lax.broadcasted_iota(jnp.int32, (c, c), 0)
    cj = lax.broadcasted_iota(jnp.int32, (c, c), 1)
    tri_incl = (cj <= ri).astype(f32)
    tri_strict = (cj < ri).astype(f32)
    eye = (cj == ri).astype(f32)
    lane = lax.broadcasted_iota(jnp.int32, (1, LANES), 1)
    head_masks = [(lane // HEAD_DIM == j).astype(f32) for j in range(HEADS_PER_TILE)]
    bi_ = lax.broadcasted_iota(jnp.int32, (LANES, LANES), 0) // HEAD_DIM
    bj_ = lax.broadcasted_iota(jnp.int32, (LANES, LANES), 1) // HEAD_DIM
    block_diag = (bi_ == bj_).astype(f32)

    for bi in range(bt):
        for pair in range(n_pairs):
            sl = pl.ds(pair * LANES, LANES)
            r, k, v = r_ref[bi, :, sl], k_ref[bi, :, sl], v_ref[bi, :, sl]
            kk, b, lw = kk_ref[bi, :, sl], b_ref[bi, :, sl], lw_ref[bi, :, sl]
            cum = _dot_exact_lhs(tri_incl.astype(bf16), lw)
            cum_end = cum[c - 1:c, :]
            e_neg = jnp.exp(-cum)
            e_end = jnp.exp(cum_end - cum)
            a_mat = kk * jnp.exp(cum - lw)
            b_til = b * e_neg
            k_til = k * e_neg
            p_mat = r * jnp.exp(cum)
            b_hat = b * e_end
            k_hat = k * e_end

            ta = jnp.zeros((c, LANES), f32)
            tmv = jnp.zeros((c, LANES), f32)
            lr_list, mr_list = [], []
            for j in range(HEADS_PER_TILE):
                hm = head_masks[j]
                x = jnp.concatenate([a_mat * hm, p_mat * hm], axis=0)
                gb = _dot_hi(x, b_til, _NT)
                gk = _dot_hi(x, k_til, _NT)
                l_mat = gb[:c] * tri_strict
                m_mat = gk[:c] * tri_strict
                lr_list.append(gb[c:] * tri_incl)
                mr_list.append(gk[c:] * tri_incl)
                npow = -l_mat
                t_inv = eye + npow
                for _ in range(n_double - 1):
                    npow = _dot_hi(npow, npow)
                    t_inv = t_inv + _dot_hi(npow, t_inv)
                mv = _dot_hi(m_mat, v)
                tz = _dot_hi(t_inv, jnp.concatenate([a_mat * hm, mv], axis=1))
                ta = ta + tz[:, :LANES]
                tmv = tmv + tz[:, LANES:] * hm

            s_prev = state[bi, pair]
            xs = _dot_hi(jnp.concatenate([ta, p_mat], axis=0), s_prev, _NT)
            u = -(xs[:c] + tmv)
            o = xs[c:]
            for j in range(HEADS_PER_TILE):
                o = o + head_masks[j] * (_dot_hi(lr_list[j], u) + _dot_hi(mr_list[j], v))
            upd = _dot_hi(jnp.concatenate([u, v], axis=0),
                          jnp.concatenate([b_hat, k_hat], axis=0), _TN)
            state[bi, pair] = s_prev * jnp.exp(cum_end) + upd * block_diag
            o_ref[bi, :, sl] = o

    @pl.when(ci == pl.num_programs(1) - 1)
    def _():
        s_out[...] = state[...]


def _rwkv_scan(r, k, v, kk, b, lw, s0, bt, chunk):
    batch, t_pad, width = r.shape
    n_pairs = width // LANES
    seq_spec = pl.BlockSpec((bt, chunk, width), lambda bb, c: (bb, c, 0))
    st_spec = pl.BlockSpec((bt, n_pairs, LANES, LANES), lambda bb, c: (bb, 0, 0, 0))
    return pl.pallas_call(
        _rwkv_scan_kernel,
        grid=(batch // bt, t_pad // chunk),
        in_specs=[seq_spec] * 6 + [st_spec],
        out_specs=[seq_spec, st_spec],
        out_shape=[jax.ShapeDtypeStruct((batch, t_pad, width), f32),
                   jax.ShapeDtypeStruct((batch, n_pairs, LANES, LANES), f32)],
        scratch_shapes=[pltpu.VMEM((bt, n_pairs, LANES, LANES), f32)],
        compiler_params=_params("parallel", "arbitrary"),
        name="rwkv_scan",
    )(r, k, v, kk, b, lw, s0)


def _pair_state(s):
    bsz, h, n, _ = s.shape
    s = s.reshape(bsz, h // HEADS_PER_TILE, HEADS_PER_TILE, n, n)
    eye = jnp.eye(HEADS_PER_TILE, dtype=s.dtype)
    out = jnp.einsum("bpjvk,ji->bpjvik", s, eye)
    return out.reshape(bsz, h // HEADS_PER_TILE, LANES, LANES)


def _unpair_state(s, heads):
    bsz = s.shape[0]
    s = s.reshape(bsz, heads // HEADS_PER_TILE, HEADS_PER_TILE, HEAD_DIM, HEADS_PER_TILE, HEAD_DIM)
    diag = jnp.stack([s[:, :, j, :, j, :] for j in range(HEADS_PER_TILE)], axis=2)
    return diag.reshape(bsz, heads, HEAD_DIM, HEAD_DIM)


def _merge_kernel(h_ref, oa_ref, o_ref, r_ref, k_ref, v_ref, g_ref, ga_ref, gb_ref,
                  rk_ref, lnw_ref, lnb_ref, ones_ref, wba_ref, wbb_ref, wout_ref, gpost_ref,
                  out_ref):
    ones = ones_ref[...]
    inv = 1.0 / HEAD_DIM
    o = o_ref[...]
    mean = _dot_exact_rhs(o, ones) * inv
    d = o - mean
    var = _dot_exact_rhs(d * d, ones) * inv
    on = d * lax.rsqrt(var + GN_EPS) * lnw_ref[...] + lnb_ref[...]
    bonus = _dot_exact_rhs(r_ref[...] * k_ref[...] * rk_ref[...], ones) * v_ref[...]
    ob = (on + bonus) * g_ref[...]
    merged = (ga_ref[...] * jnp.dot(oa_ref[...].astype(bf16), wba_ref[...], preferred_element_type=f32)
              + gb_ref[...] * jnp.dot(ob.astype(bf16), wbb_ref[...], preferred_element_type=f32))
    y = jnp.dot(merged.astype(bf16), wout_ref[...], preferred_element_type=f32)
    out_ref[...] = h_ref[...] + _rms(y, gpost_ref[...])


def _merge(h, oa, o, r, k, v, g, ga, gb, consts, tm):
    n, d = h.shape
    w = oa.shape[1]
    row = lambda width: pl.BlockSpec((tm, width), lambda i: (i, 0))
    return pl.pallas_call(
        _merge_kernel,
        grid=(n // tm,),
        in_specs=[row(d)] + [row(w)] * 6 + [row(d)] * 2 + [_const_spec(c.shape) for c in consts],
        out_specs=row(d),
        out_shape=jax.ShapeDtypeStruct((n, d), f32),
        compiler_params=_params("parallel"),
        name="merge",
    )(h, oa, o, r, k, v, g, ga, gb, *consts)


def _row_tile(n, target):
    t = min(n, target)
    assert n % t == 0
    return t


def _layer(x, prev_shift, wkv0, past, w, n_heads):
    (g1a, g1b, f1g, f1u, f1d, gma, gmb, w_in, w_ba, w_bb, w_out,
     mu, w0, w2, a0, a2, g2, k_k, k_a, r_k, ln_w, ln_b,
     g2a, g2b, f2g, f2u, f2d) = w
    batch, t, d = x.shape
    n = batch * t
    mw = w_ba.shape[0]
    rw = w_bb.shape[0]
    rp = mu.shape[-1]
    widths = (mw, mw, mw, rp, d, d)
    row = lambda a: a.reshape(1, -1)

    x2 = x.reshape(n, d)
    h = _ffn(x2, row(g1a), row(g1b), f1g, f1u, f1d, _row_tile(n, 512))
    q, k, v, p_rw, ga, gb = _in_proj(h, row(gma), w_in, widths, _row_tile(n, 256))

    if past is None:
        assert t % MOBA_BLOCK == 0
        kmean = _block_means(k, min(8, n // MOBA_BLOCK))
        o_a = _moba_prompt(q, k, v, kmean, batch, t, n_heads)
    else:
        cache_k, cache_v, page_table = past
        n_pool, page = cache_k.shape[:2]
        o_a = _moba_sample(q.reshape(batch, t, mw), k.reshape(batch, t, mw), v.reshape(batch, t, mw),
                           cache_k.reshape(n_pool, page, mw), cache_v.reshape(n_pool, page, mw),
                           page_table, n_heads).reshape(n, mw)

    if t % 64 == 0:
        chunk, t_pad, bt_pre, tt_pre, bt_scan = 64, t, 1, min(t, 512), min(batch, 2)
    else:
        chunk = 8
        t_pad = -(-t // chunk) * chunk
        bt_pre, tt_pre, bt_scan = min(batch, 16), t_pad, min(batch, 2)
    p3 = p_rw.reshape(batch, t, rp)
    p3p = jnp.pad(p3, ((0, 0), (0, t_pad - t), (0, 0))) if t_pad != t else p3
    pre_w = (row(mu), row(w0), w2, row(a0), a2, g2, row(k_k), row(k_a))
    r_, kh, v_, kk, b_, lw, g_ = _rwkv_pre(p3p, prev_shift.reshape(batch, 1, rp), t, pre_w, bt_pre, tt_pre)
    o_scan, s_fin = _rwkv_scan(r_, kh, v_, kk, b_, lw, _pair_state(wkv0.astype(f32)), bt_scan, chunk)
    wkv = _unpair_state(s_fin, rw // HEAD_DIM).astype(wkv0.dtype)
    unpad = lambda a: a[:, :t].reshape(n, rw)
    consts = (row(r_k), row(ln_w), row(ln_b), _head_ones(rw), w_ba, w_bb, w_out, row(gmb))
    h2 = _merge(h, o_a, unpad(o_scan), unpad(r_), unpad(kh), unpad(v_), unpad(g_), ga, gb, consts,
                _row_tile(n, 256))
    y = _ffn(h2, row(g2a), row(g2b), f2g, f2u, f2d, _row_tile(n, 512))
    kv_shape = (batch, t, n_heads, HEAD_DIM)
    return y.reshape(batch, t, d), k.reshape(kv_shape), v.reshape(kv_shape), wkv, p3[:, -1]


def kernel(x_prompt, x_sample, cache_k, cache_v, state_wkv, state_shift, page_table,
           g_ffn1_pre, g_ffn1_post, w_ffn1_gate, w_ffn1_up, w_ffn1_down,
           g_mix_pre, g_mix_post, w_in, w_branch_a, w_branch_b, w_out,
           rwkv_mu, rwkv_w0, rwkv_w2, rwkv_a0, rwkv_a2, rwkv_g2, rwkv_k_k, rwkv_k_a, rwkv_r_k,
           rwkv_ln_w, rwkv_ln_b,
           g_ffn2_pre, g_ffn2_post, w_ffn2_gate, w_ffn2_up, w_ffn2_down):
    depth = w_in.shape[0]
    bp = x_prompt.shape[0]
    n_heads = cache_k.shape[3]
    rp = rwkv_mu.shape[-1]
    rw = rwkv_w0.shape[-1]
    yp, ys = x_prompt, x_sample
    outs = [[] for _ in range(8)]
    for i in range(depth):
        cast = lambda a: a[i].astype(bf16)
        w = (g_ffn1_pre[i], g_ffn1_post[i], cast(w_ffn1_gate), cast(w_ffn1_up), cast(w_ffn1_down),
             g_mix_pre[i], g_mix_post[i], cast(w_in), cast(w_branch_a), cast(w_branch_b), cast(w_out),
             rwkv_mu[i], rwkv_w0[i], rwkv_w2[i], rwkv_a0[i], rwkv_a2[i], rwkv_g2[i], rwkv_k_k[i],
             rwkv_k_a[i], rwkv_r_k[i], rwkv_ln_w[i], rwkv_ln_b[i],
             g_ffn2_pre[i], g_ffn2_post[i], cast(w_ffn2_gate), cast(w_ffn2_up), cast(w_ffn2_down))
        yp, kp, vp, wkvp, shp = _layer(
            yp, jnp.zeros((bp, rp), x_prompt.dtype),
            jnp.zeros((bp, rw // HEAD_DIM, HEAD_DIM, HEAD_DIM), state_wkv.dtype), None, w, n_heads)
        ys, ksn, vsn, wkvs, shs = _layer(
            ys, state_shift[i], state_wkv[i], (cache_k[i], cache_v[i], page_table), w, n_heads)
        for lst, val in zip(outs, (kp, vp, ksn, vsn, wkvp, shp, wkvs, shs)):
            lst.append(val)
    return (yp, ys) + tuple(jnp.stack(lst) for lst in outs)
```

```python
import functools
import math

import jax
import jax.numpy as jnp
import numpy as np
from jax import lax
from jax.experimental import pallas as pl
from jax.experimental.pallas import tpu as pltpu

HEAD_DIM = 64
LANES = 128
HEADS_PER_TILE = LANES // HEAD_DIM
MOBA_BLOCK = 256
MOBA_TOPK = 3
RMS_EPS = 1e-6
GN_EPS = 64e-5
VMEM_LIMIT = 56 * 1024 * 1024
NEG_INF = float("-inf")
POS_INF = float("inf")
LOG2E = 1.4426950408889634

f32 = jnp.float32
bf16 = jnp.bfloat16


def _params(*sem):
    return pltpu.CompilerParams(dimension_semantics=sem, vmem_limit_bytes=VMEM_LIMIT)


def _const_spec(shape):
    nd = len(shape)
    return pl.BlockSpec(shape, lambda *_: (0,) * nd, pipeline_mode=pl.Buffered(1))


def _rms(x, g):
    return x * lax.rsqrt(jnp.mean(x * x, axis=-1, keepdims=True) + RMS_EPS) * g


def _split3(x):
    hi = x.astype(bf16)
    r1 = x - hi.astype(f32)
    mid = r1.astype(bf16)
    lo = (r1 - mid.astype(f32)).astype(bf16)
    return hi, mid, lo


_NN = (((1,), (0,)), ((), ()))
_NT = (((1,), (1,)), ((), ()))
_TN = (((0,), (0,)), ((), ()))


def _dg(a, b, dims):
    return lax.dot_general(a, b, dims, preferred_element_type=f32)


def _dot_lo(a, b, dims=_NN):
    return _dg(a.astype(bf16), b.astype(bf16), dims)


def _dot_hi(a, b, dims=_NN):
    ah, am, _ = _split3(a)
    bh, bm, _ = _split3(b)
    return _dg(ah, bh, dims) + (_dg(ah, bm, dims) + _dg(am, bh, dims))


def _dot_exact_lhs(a_bf, b, dims=_NN):
    bh, bm, bl = _split3(b)
    return _dg(a_bf, bh, dims) + (_dg(a_bf, bm, dims) + _dg(a_bf, bl, dims))


def _dot_exact_rhs(a, b_bf, dims=_NN):
    ah, am, al = _split3(a)
    return _dg(ah, b_bf, dims) + (_dg(am, b_bf, dims) + _dg(al, b_bf, dims))


def _ffn_kernel(x_ref, gpre_ref, gpost_ref, wg_ref, wu_ref, wd_ref, o_ref):
    x = x_ref[...]
    h = _rms(x, gpre_ref[...]).astype(bf16)
    gate = jnp.dot(h, wg_ref[...], preferred_element_type=f32)
    up = jnp.dot(h, wu_ref[...], preferred_element_type=f32)
    act = (gate * jax.nn.sigmoid(gate) * up).astype(bf16)
    y = jnp.dot(act, wd_ref[...], preferred_element_type=f32)
    o_ref[...] = x + 0.5 * _rms(y, gpost_ref[...])


def _ffn(x, g_pre, g_post, wg, wu, wd, tm):
    n, d = x.shape
    dff = wg.shape[1]
    return pl.pallas_call(
        _ffn_kernel,
        grid=(n // tm,),
        in_specs=[
            pl.BlockSpec((tm, d), lambda i: (i, 0)),
            _const_spec((1, d)), _const_spec((1, d)),
            _const_spec((d, dff)), _const_spec((d, dff)), _const_spec((dff, d)),
        ],
        out_specs=pl.BlockSpec((tm, d), lambda i: (i, 0)),
        out_shape=jax.ShapeDtypeStruct((n, d), f32),
        compiler_params=_params("parallel"),
        name="ffn",
    )(x, g_pre, g_post, wg, wu, wd)


def _in_proj_kernel(splits, h_ref, g_ref, w_ref, q_ref, k_ref, v_ref, p_ref, ga_ref, gb_ref,
                    *mxu_refs):
    u = _rms(h_ref[...], g_ref[...]).astype(bf16)
    outs = (q_ref, k_ref, v_ref, p_ref, ga_ref, gb_ref)
    for idx, o_ref in enumerate(outs):
        lo, hi = splits[idx], splits[idx + 1]
        y = jnp.dot(u, w_ref[:, lo:hi], preferred_element_type=f32)
        if idx >= 4:
            y = jax.nn.sigmoid(y)
        o_ref[...] = y
        if mxu_refs and idx == 1:
            mxu_refs[0][...] = y.astype(bf16)
        if mxu_refs and idx == 2:
            mxu_refs[1][0] = y.T.astype(bf16)


def _in_proj(h, g, w_in, widths, tm, attention_copies):
    n, d = h.shape
    splits = tuple(int(s) for s in np.concatenate([[0], np.cumsum(widths)]))
    assert splits[-1] == w_in.shape[1] and all(s % LANES == 0 for s in splits)
    out_specs = [pl.BlockSpec((tm, w), lambda i: (i, 0)) for w in widths]
    out_shape = [jax.ShapeDtypeStruct((n, w), f32) for w in widths]
    if attention_copies:
        assert tm == MOBA_BLOCK
        out_specs += [pl.BlockSpec((tm, widths[1]), lambda i: (i, 0)),
                      pl.BlockSpec((1, widths[2], tm), lambda i: (i, 0, 0))]
        out_shape += [jax.ShapeDtypeStruct((n, widths[1]), bf16),
                      jax.ShapeDtypeStruct((n // tm, widths[2], tm), bf16)]
    return pl.pallas_call(
        functools.partial(_in_proj_kernel, splits),
        grid=(n // tm,),
        in_specs=[
            pl.BlockSpec((tm, d), lambda i: (i, 0)),
            _const_spec((1, d)),
            _const_spec(w_in.shape),
        ],
        out_specs=out_specs,
        out_shape=out_shape,
        compiler_params=_params("parallel"),
        name="in_proj",
    )(h, g, w_in)


def _top_blocks(gate, topk):
    blk = lax.broadcasted_iota(jnp.int32, gate.shape, 0).astype(f32)
    sel = jnp.zeros(gate.shape, f32)
    big = float(gate.shape[0])
    for _ in range(topk):
        m = jnp.max(gate, axis=0, keepdims=True)
        cand = (gate == m) & (gate > NEG_INF)
        idx = jnp.min(jnp.where(cand, blk, big), axis=0, keepdims=True)
        hit = blk == idx
        sel = jnp.where(hit, 1.0, sel)
        gate = jnp.where(hit, NEG_INF, gate)
    return sel


def _kmean_kernel(k_ref, o_ref):
    nblk = o_ref.shape[0]
    k = k_ref[...]
    o_ref[...] = jnp.mean(k.reshape(nblk, MOBA_BLOCK, k.shape[-1]), axis=1)


def _block_means(k, nblk_per_step):
    n, w = k.shape
    rows = nblk_per_step * MOBA_BLOCK
    return pl.pallas_call(
        _kmean_kernel,
        grid=(n // rows,),
        in_specs=[pl.BlockSpec((rows, w), lambda i: (i, 0))],
        out_specs=pl.BlockSpec((nblk_per_step, w), lambda i: (i, 0)),
        out_shape=jax.ShapeDtypeStruct((n // MOBA_BLOCK, w), f32),
        compiler_params=_params("parallel"),
        name="moba_block_means",
    )(k)


def _moba_prompt_kernel(n_heads, q_ref, kb_ref, vt_ref, km_ref, o_ref, off_sc, s_sc):
    pair = pl.program_id(1)
    qi = pl.program_id(2)
    tq = q_ref.shape[0]
    nb = km_ref.shape[0]
    lane = lax.broadcasted_iota(jnp.int32, (1, LANES), 1)
    key_i = lax.broadcasted_iota(jnp.int32, (MOBA_BLOCK, tq), 0)
    qry_i = lax.broadcasted_iota(jnp.int32, (MOBA_BLOCK, tq), 1)
    rel = (qry_i - key_i).astype(f32)
    causal = key_i <= qry_i
    blk_row = lax.broadcasted_iota(jnp.int32, (nb, tq), 0)

    q = q_ref[...]
    km = km_ref[...]
    own = pl.ds(pl.multiple_of(qi * MOBA_BLOCK, MOBA_BLOCK), MOBA_BLOCK)
    k_own = kb_ref[own, :]
    vt_own = vt_ref[qi]

    consts, carry = [], []
    for j in range(HEADS_PER_TILE):
        head_mask = (lane // HEAD_DIM == j).astype(f32)
        head = jnp.full((1, 1), pair * HEADS_PER_TILE + j + 1, jnp.int32).astype(f32)
        slope2 = jnp.exp2(head * (-8.0 / n_heads)) * LOG2E
        gate = _dot_hi(km * head_mask, q, _NT)
        gate = jnp.where(blk_row < qi, gate, NEG_INF)
        off_sc[j] = jnp.where(_top_blocks(gate, MOBA_TOPK) > 0.5, 0.0, POS_INF)
        qs = (q * head_mask * (HEAD_DIM ** -0.5 * LOG2E)).astype(bf16)
        alibi = slope2 * rel

        s = _dg(k_own, qs, _NT) - alibi
        s = jnp.where(causal, s, NEG_INF)
        m = jnp.max(s, axis=0, keepdims=True)
        p = jnp.exp2(s - m)
        l = jnp.sum(p, axis=0, keepdims=True)
        acc = _dg(vt_own, p.astype(bf16), _NN)
        consts.append((qs, alibi, slope2))
        carry += [m, l, acc]

    def score(n, slot):
        nc = jnp.minimum(n, nb - 1)
        kb = kb_ref[pl.ds(pl.multiple_of(nc * MOBA_BLOCK, MOBA_BLOCK), MOBA_BLOCK), :]
        raw = []
        for j in range(HEADS_PER_TILE):
            s = _dg(kb, consts[j][0], _NT) - consts[j][1]
            s_sc[slot, j] = s
            raw.append(jnp.max(s, axis=0, keepdims=True))
        return raw

    def attend(n, slot, raw, stats):
        nc = jnp.minimum(n, nb - 1)
        vt = vt_ref[nc]
        out = []
        for j in range(HEADS_PER_TILE):
            slope2 = consts[j][2]
            m, l, acc = stats[3 * j:3 * j + 3]
            off = jnp.where(n < qi, off_sc[j, pl.ds(nc, 1), :], POS_INF)
            far = ((qi - n) * MOBA_BLOCK).astype(f32) * slope2
            m_new = jnp.maximum(m, raw[j] - far - off)
            alpha = jnp.exp2(m - m_new)
            p = jnp.exp2(s_sc[slot, j] - (m_new + far + off))
            l = l * alpha + jnp.sum(p, axis=0, keepdims=True)
            acc = acc * alpha + _dg(vt, p.astype(bf16), _NN)
            out += [m_new, l, acc]
        return out

    def body(i, carry):
        stats, raw_even = list(carry[:-HEADS_PER_TILE]), list(carry[-HEADS_PER_TILE:])
        n = 2 * i
        raw_odd = score(n + 1, 1)
        stats = attend(n, 0, raw_even, stats)
        raw_even = score(n + 2, 0)
        stats = attend(n + 1, 1, raw_odd, stats)
        return tuple(stats + raw_even)

    carry = lax.fori_loop(0, (qi + 1) // 2, body, tuple(carry + score(0, 0)))
    outs = [carry[3 * j + 2] / carry[3 * j + 1] for j in range(HEADS_PER_TILE)]
    out_t = jnp.concatenate([outs[j][j * HEAD_DIM:(j + 1) * HEAD_DIM] for j in range(HEADS_PER_TILE)],
                            axis=0)
    o_ref[...] = out_t.T


def _moba_prompt(q, k_bf, v_t, kmean, batch, seq, n_heads):
    n, w = q.shape
    n_pairs = w // LANES
    nb = seq // MOBA_BLOCK
    tq = MOBA_BLOCK
    return pl.pallas_call(
        functools.partial(_moba_prompt_kernel, n_heads),
        grid=(batch, n_pairs, nb),
        in_specs=[
            pl.BlockSpec((tq, LANES), lambda b, p, i: (b * nb + i, p)),
            pl.BlockSpec((seq, LANES), lambda b, p, i: (b, p)),
            pl.BlockSpec((nb, LANES, MOBA_BLOCK), lambda b, p, i: (b, p, 0)),
            pl.BlockSpec((nb, LANES), lambda b, p, i: (b, p)),
        ],
        out_specs=pl.BlockSpec((tq, LANES), lambda b, p, i: (b * nb + i, p)),
        out_shape=jax.ShapeDtypeStruct((n, w), f32),
        scratch_shapes=[pltpu.VMEM((HEADS_PER_TILE, nb, tq), f32),
                        pltpu.VMEM((2, HEADS_PER_TILE, MOBA_BLOCK, tq), f32)],
        compiler_params=_params("parallel", "parallel", "arbitrary"),
        name="moba_prompt",
    )(q, k_bf, v_t, kmean)


def _moba_sample_kernel(n_heads, pt_ref, q_ref, kn_ref, vn_ref,
                        k0_ref, k1_ref, v0_ref, v1_ref, o_ref,
                        m_sc, l_sc, g_sc, acc_sc):
    del pt_ref
    j = pl.program_id(1)
    n_blocks = pl.num_programs(1)
    rows = q_ref.shape[1]
    page = k0_ref.shape[1]
    cols = page * n_heads
    head_bits = int(math.log2(n_heads))
    q = q_ref[0]
    r_idx = lax.broadcasted_iota(jnp.int32, (rows, 1), 0)
    head = r_idx & (n_heads - 1)
    t_q = r_idx >> head_bits
    slope = jnp.exp2((head + 1).astype(f32) * (-8.0 / n_heads))
    past_len = n_blocks * MOBA_BLOCK
    qs = (q * (HEAD_DIM ** -0.5)).astype(bf16)
    col = lax.broadcasted_iota(jnp.int32, (rows, cols), 1)
    same_head = (col & (n_heads - 1)) == head
    tok = (col >> head_bits).astype(f32)

    def scores(k2d, first_pos):
        dist = (past_len - first_pos + t_q).astype(f32) - tok
        s = _dg(qs, k2d.astype(bf16), _NT) - slope * dist
        return jnp.where(same_head, s, NEG_INF)

    k0, k1 = k0_ref[0], k1_ref[0]
    base = j * MOBA_BLOCK
    s0 = scores(k0.reshape(cols, HEAD_DIM), base)
    s1 = scores(k1.reshape(cols, HEAD_DIM), base + page)
    m = jnp.maximum(jnp.max(s0, axis=-1, keepdims=True), jnp.max(s1, axis=-1, keepdims=True))
    p0 = jnp.exp(s0 - m)
    p1 = jnp.exp(s1 - m)
    l = jnp.sum(p0, axis=-1, keepdims=True) + jnp.sum(p1, axis=-1, keepdims=True)
    acc = (_dg(p0.astype(bf16), v0_ref[0].reshape(cols, HEAD_DIM).astype(bf16), _NN)
           + _dg(p1.astype(bf16), v1_ref[0].reshape(cols, HEAD_DIM).astype(bf16), _NN))
    kmean = (jnp.sum(k0, axis=0) + jnp.sum(k1, axis=0)) * (1.0 / MOBA_BLOCK)
    q3 = q.reshape(rows // n_heads, n_heads, HEAD_DIM)
    gate = jnp.sum(q3 * kmean[None], axis=-1, keepdims=True).reshape(rows, 1)
    m_sc[j] = jnp.broadcast_to(m, (rows, LANES))
    l_sc[j] = jnp.broadcast_to(l, (rows, LANES))
    g_sc[j] = jnp.broadcast_to(gate, (rows, LANES))
    acc_sc[j] = acc

    @pl.when(j == n_blocks - 1)
    def _():
        nb = m_sc.shape[0]
        kcol = lax.broadcasted_iota(jnp.int32, (rows, rows), 1)
        t_k = kcol >> head_bits
        s_own = _dg(qs, kn_ref[0].astype(bf16), _NT) - slope * (t_q - t_k).astype(f32)
        s_own = jnp.where(((kcol & (n_heads - 1)) == head) & (t_k <= t_q), s_own, NEG_INF)
        m_tot = jnp.max(s_own, axis=-1, keepdims=True)
        gates = [g_sc[n][:, 0:1] for n in range(nb)]
        chosen = []
        for n in range(nb):
            rank = jnp.zeros((rows, 1), f32)
            for i in range(nb):
                if i == n:
                    continue
                ahead = (gates[i] > gates[n]) | ((gates[i] == gates[n]) & (i < n))
                rank = rank + ahead.astype(f32)
            chosen.append(rank < MOBA_TOPK)
        for n in range(nb):
            m_tot = jnp.maximum(m_tot, jnp.where(chosen[n], m_sc[n][:, 0:1], NEG_INF))
        p_own = jnp.exp(s_own - m_tot)
        l_tot = jnp.sum(p_own, axis=-1, keepdims=True)
        acc_tot = _dg(p_own.astype(bf16), vn_ref[0].astype(bf16), _NN)
        for n in range(nb):
            wgt = jnp.where(chosen[n], jnp.exp(m_sc[n][:, 0:1] - m_tot), 0.0)
            l_tot = l_tot + wgt * l_sc[n][:, 0:1]
            acc_tot = acc_tot + wgt * acc_sc[n]
        o_ref[0] = acc_tot / l_tot


def _moba_sample(q, k_new, v_new, cache_k, cache_v, page_table):
    batch, rows, dim = q.shape
    n_pool, page, n_heads, _ = cache_k.shape
    n_pages = page_table.shape[1]
    assert dim == HEAD_DIM and MOBA_BLOCK == 2 * page and n_pages % 2 == 0
    assert n_heads & (n_heads - 1) == 0 and rows <= MOBA_BLOCK * n_heads
    n_blocks = n_pages // 2
    assert n_blocks >= MOBA_TOPK

    def page_spec(which):
        return pl.BlockSpec((1, page, n_heads, dim), lambda b, j, pt: (pt[b, 2 * j + which], 0, 0, 0))

    row_spec = pl.BlockSpec((1, rows, dim), lambda b, j, pt: (b, 0, 0))
    return pl.pallas_call(
        functools.partial(_moba_sample_kernel, n_heads),
        grid_spec=pltpu.PrefetchScalarGridSpec(
            num_scalar_prefetch=1,
            grid=(batch, n_blocks),
            in_specs=[row_spec, row_spec, row_spec,
                      page_spec(0), page_spec(1), page_spec(0), page_spec(1)],
            out_specs=row_spec,
            scratch_shapes=[
                pltpu.VMEM((n_blocks, rows, LANES), f32),
                pltpu.VMEM((n_blocks, rows, LANES), f32),
                pltpu.VMEM((n_blocks, rows, LANES), f32),
                pltpu.VMEM((n_blocks, rows, dim), f32),
            ],
        ),
        out_shape=jax.ShapeDtypeStruct((batch, rows, dim), f32),
        compiler_params=_params("parallel", "arbitrary"),
        name="moba_sample",
    )(page_table, q, k_new, v_new, cache_k, cache_k, cache_v, cache_v)


def _rwkv_pre_kernel(t_valid, widths, p_ref, prev_ref, mu_ref, w0_ref, w2_ref, a0_ref, a2_ref,
                     g2_ref, kk_ref, ka_ref, ones_ref,
                     r_out, k_out, v_out, kk_out, b_out, lw_out, g_out, carry):
    ti = pl.program_id(1)
    bt, tt, width = p_ref.shape
    rw, dl, al, gl = widths

    @pl.when(ti == 0)
    def _():
        carry[...] = prev_ref[...]

    p = p_ref[...]
    t_idx = lax.broadcasted_iota(jnp.int32, (bt, tt, 1), 1)
    shifted = jnp.where(t_idx == 0, carry[...], pltpu.roll(p, 1, axis=1))
    carry[...] = p[:, tt - 1:tt, :]
    m = (p + (shifted - p) * mu_ref[...]).reshape(bt * tt, width)

    r = m[:, 0:rw]
    k = m[:, rw:2 * rw]
    v = m[:, 2 * rw:3 * rw]
    lora = m[:, 3 * rw:3 * rw + dl + al]
    gd = m[:, 3 * rw + dl + al:3 * rw + dl + al + gl]

    z = -(w0_ref[...] + _dot_hi(jnp.tanh(lora), w2_ref[...]))
    softplus = jnp.maximum(z, 0.0) + jnp.log(1.0 + jnp.exp(-jnp.abs(z)))
    w = -softplus - 0.5
    log_decay = -jnp.exp(w)
    a = jax.nn.sigmoid(a0_ref[...] + _dot_hi(lora, a2_ref[...]))
    g = _dot_hi(jax.nn.sigmoid(gd), g2_ref[...])
    kk = k * kk_ref[...]
    norm = jnp.sqrt(_dot_exact_rhs(kk * kk, ones_ref[...]))
    kk = kk / jnp.maximum(norm, 1e-12)
    kh = k * (1.0 + (a - 1.0) * ka_ref[...])

    live = (lax.broadcasted_iota(jnp.int32, (bt, tt, 1), 1) + ti * tt < t_valid)
    live = live.astype(f32).reshape(bt * tt, 1)
    shape3 = (bt, tt, rw)
    r_out[...] = (r * live).reshape(shape3)
    k_out[...] = (kh * live).reshape(shape3)
    v_out[...] = (v * live).reshape(shape3)
    kk_out[...] = (kk * live).reshape(shape3)
    b_out[...] = (kk * a * live).reshape(shape3)
    lw_out[...] = (log_decay * live).reshape(shape3)
    g_out[...] = g.reshape(shape3)


def _head_ones(width):
    idx = np.arange(width) // HEAD_DIM
    return jnp.asarray(idx[:, None] == idx[None, :], dtype=bf16)


def _rwkv_pre(p, prev, t_valid, weights, bt, tt):
    batch, t_pad, width = p.shape
    mu, w0, w2, a0, a2, g2, k_k, k_a = weights
    rw = w0.shape[-1]
    widths = (rw, w2.shape[0], a2.shape[0], g2.shape[0])
    assert 3 * rw + sum(widths[1:]) == width
    assert widths[1] + widths[2] == LANES and widths[3] % LANES == 0
    w2p = jnp.pad(w2, ((0, widths[2]), (0, 0)))
    a2p = jnp.pad(a2, ((widths[1], 0), (0, 0)))
    consts = (mu, w0, w2p, a0, a2p, g2, k_k, k_a, _head_ones(rw))
    out_spec = pl.BlockSpec((bt, tt, rw), lambda b, t: (b, t, 0))
    return pl.pallas_call(
        functools.partial(_rwkv_pre_kernel, t_valid, widths),
        grid=(batch // bt, t_pad // tt),
        in_specs=[pl.BlockSpec((bt, tt, width), lambda b, t: (b, t, 0)),
                  pl.BlockSpec((bt, 1, width), lambda b, t: (b, 0, 0))]
                 + [_const_spec(c.shape) for c in consts],
        out_specs=[out_spec] * 7,
        out_shape=[jax.ShapeDtypeStruct((batch, t_pad, rw), f32)] * 7,
        scratch_shapes=[pltpu.VMEM((bt, 1, width), f32)],
        compiler_params=_params("parallel", "arbitrary"),
        name="rwkv_pre",
    )(p, prev, *consts)


def _rwkv_scan_kernel(r_ref, k_ref, v_ref, kk_ref, b_ref, lw_ref, s0_ref, o_ref, s_out, state):
    ci = pl.program_id(1)
    bt, c, width = r_ref.shape
    n_pairs = width // LANES
    c2 = HEADS_PER_TILE * c
    n_double = int(math.log2(c))
    assert 1 << n_double == c and HEADS_PER_TILE == 2

    @pl.when(ci == 0)
    def _():
        state[...] = s0_ref[...]

    ri = lax.broadcasted_iota(jnp.int32, (c, c), 0)
    cj = lax.broadcasted_iota(jnp.int32, (c, c), 1)
    tri_bf = (cj <= ri).astype(bf16)
    r2 = lax.broadcasted_iota(jnp.int32, (c2, c2), 0)
    q2 = lax.broadcasted_iota(jnp.int32, (c2, c2), 1)
    same_blk = (r2 >= c) == (q2 >= c)
    strict2 = (same_blk & (q2 < r2)).astype(f32)
    incl2 = (same_blk & (q2 <= r2)).astype(f32)
    eye2 = (q2 == r2).astype(f32)
    lane = lax.broadcasted_iota(jnp.int32, (1, LANES), 1)
    hm = [(lane // HEAD_DIM == j).astype(f32) for j in range(HEADS_PER_TILE)]
    bi_ = lax.broadcasted_iota(jnp.int32, (LANES, LANES), 0) // HEAD_DIM
    bj_ = lax.broadcasted_iota(jnp.int32, (LANES, LANES), 1) // HEAD_DIM
    block_diag = (bi_ == bj_).astype(f32)

    def stack(x):
        return jnp.concatenate([x * hm[0], x * hm[1]], axis=0)

    def twice(x):
        return jnp.concatenate([x, x], axis=0)

    def unstack(x2):
        return x2[:c] * hm[0] + x2[c:] * hm[1]

    probs = [(bi, pl.ds(pair * LANES, LANES), pair) for bi in range(bt) for pair in range(n_pairs)]
    ld = lambda ref: [ref[bi, :, sl] for bi, sl, _ in probs]
    r, k, v, kk, b, lw = ld(r_ref), ld(k_ref), ld(v_ref), ld(kk_ref), ld(b_ref), ld(lw_ref)
    nprob = len(probs)
    each = range(nprob)

    cum = [_dot_exact_lhs(tri_bf, lw[i]) for i in each]
    cum_end = [cum[i][c - 1:c, :] for i in each]
    e_neg = [jnp.exp(-cum[i]) for i in each]
    e_end = [jnp.exp(cum_end[i] - cum[i]) for i in each]
    a_mat = [kk[i] * jnp.exp(cum[i] - lw[i]) for i in each]
    p_mat = [r[i] * jnp.exp(cum[i]) for i in each]
    xa = [stack(a_mat[i]) for i in each]
    xap = [jnp.concatenate([xa[i], stack(p_mat[i])], axis=0) for i in each]
    yb = [stack(b[i] * e_neg[i]) for i in each]
    yk = [stack(k[i] * e_neg[i]) for i in each]
    v2 = [twice(v[i]) for i in each]

    gb = [_dot_lo(xap[i], yb[i], _NT) for i in each]
    gk = [_dot_lo(xap[i], yk[i], _NT) for i in each]
    l_mat = [gb[i][:c2] * strict2 for i in each]
    m_mat = [gk[i][:c2] * strict2 for i in each]
    lr_mat = [gb[i][c2:] * incl2 for i in each]
    mr_mat = [gk[i][c2:] * incl2 for i in each]

    npow = [-l_mat[i] for i in each]
    t_inv = [eye2 + npow[i] for i in each]
    for _ in range(n_double - 1):
        npow = [_dot_lo(npow[i], npow[i]) for i in each]
        t_inv = [t_inv[i] + _dot_lo(npow[i], t_inv[i]) for i in each]
    mv = [_dot_lo(m_mat[i], v2[i]) for i in each]
    tz = [_dot_lo(t_inv[i], jnp.concatenate([xa[i], mv[i]], axis=1)) for i in each]
    ta = [tz[i][:c, :LANES] + tz[i][c:, :LANES] for i in each]
    tmv = [unstack(tz[i][:, LANES:]) for i in each]

    s_prev = [state[bi, pair] for bi, _, pair in probs]
    xs = [_dot_hi(jnp.concatenate([ta[i], p_mat[i]], axis=0), s_prev[i], _NT) for i in each]
    u = [-(xs[i][:c] + tmv[i]) for i in each]
    corr = [_dot_hi(lr_mat[i], twice(u[i])) + _dot_hi(mr_mat[i], v2[i]) for i in each]
    upd = [_dot_hi(jnp.concatenate([u[i], v[i]], axis=0),
                   jnp.concatenate([b[i] * e_end[i], k[i] * e_end[i]], axis=0), _TN) for i in each]
    for i, (bi, sl, pair) in enumerate(probs):
        state[bi, pair] = s_prev[i] * jnp.exp(cum_end[i]) + upd[i] * block_diag
        o_ref[bi, :, sl] = xs[i][c:] + unstack(corr[i])

    @pl.when(ci == pl.num_programs(1) - 1)
    def _():
        s_out[...] = state[...]


def _rwkv_scan(r, k, v, kk, b, lw, s0, bt, chunk):
    batch, t_pad, width = r.shape
    n_pairs = width // LANES
    seq_spec = pl.BlockSpec((bt, chunk, width), lambda bb, c: (bb, c, 0))
    st_spec = pl.BlockSpec((bt, n_pairs, LANES, LANES), lambda bb, c: (bb, 0, 0, 0))
    return pl.pallas_call(
        _rwkv_scan_kernel,
        grid=(batch // bt, t_pad // chunk),
        in_specs=[seq_spec] * 6 + [st_spec],
        out_specs=[seq_spec, st_spec],
        out_shape=[jax.ShapeDtypeStruct((batch, t_pad, width), f32),
                   jax.ShapeDtypeStruct((batch, n_pairs, LANES, LANES), f32)],
        scratch_shapes=[pltpu.VMEM((bt, n_pairs, LANES, LANES), f32)],
        compiler_params=_params("parallel", "arbitrary"),
        name="rwkv_scan",
    )(r, k, v, kk, b, lw, s0)


def _pair_state(s):
    bsz, h, n, _ = s.shape
    s = s.reshape(bsz, h // HEADS_PER_TILE, HEADS_PER_TILE, n, n)
    eye = jnp.eye(HEADS_PER_TILE, dtype=s.dtype)
    out = jnp.einsum("bpjvk,ji->bpjvik", s, eye)
    return out.reshape(bsz, h // HEADS_PER_TILE, LANES, LANES)


def _unpair_state(s, heads):
    bsz = s.shape[0]
    s = s.reshape(bsz, heads // HEADS_PER_TILE, HEADS_PER_TILE, HEAD_DIM, HEADS_PER_TILE, HEAD_DIM)
    diag = jnp.stack([s[:, :, j, :, j, :] for j in range(HEADS_PER_TILE)], axis=2)
    return diag.reshape(bsz, heads, HEAD_DIM, HEAD_DIM)


def _merge_kernel(h_ref, oa_ref, o_ref, r_ref, k_ref, v_ref, g_ref, ga_ref, gb_ref,
                  rk_ref, lnw_ref, lnb_ref, ones_ref, wba_ref, wbb_ref, wout_ref, gpost_ref,
                  out_ref):
    ones = ones_ref[...]
    inv = 1.0 / HEAD_DIM
    o = o_ref[...]
    mean = _dot_exact_rhs(o, ones) * inv
    d = o - mean
    var = _dot_exact_rhs(d * d, ones) * inv
    on = d * lax.rsqrt(var + GN_EPS) * lnw_ref[...] + lnb_ref[...]
    bonus = _dot_exact_rhs(r_ref[...] * k_ref[...] * rk_ref[...], ones) * v_ref[...]
    ob = (on + bonus) * g_ref[...]
    merged = (ga_ref[...] * jnp.dot(oa_ref[...].astype(bf16), wba_ref[...], preferred_element_type=f32)
              + gb_ref[...] * jnp.dot(ob.astype(bf16), wbb_ref[...], preferred_element_type=f32))
    y = jnp.dot(merged.astype(bf16), wout_ref[...], preferred_element_type=f32)
    out_ref[...] = h_ref[...] + _rms(y, gpost_ref[...])


def _merge(h, oa, o, r, k, v, g, ga, gb, consts, tm):
    n, d = h.shape
    w = oa.shape[1]
    row = lambda width: pl.BlockSpec((tm, width), lambda i: (i, 0))
    return pl.pallas_call(
        _merge_kernel,
        grid=(n // tm,),
        in_specs=[row(d)] + [row(w)] * 6 + [row(d)] * 2 + [_const_spec(c.shape) for c in consts],
        out_specs=row(d),
        out_shape=jax.ShapeDtypeStruct((n, d), f32),
        compiler_params=_params("parallel"),
        name="merge",
    )(h, oa, o, r, k, v, g, ga, gb, *consts)


def _row_tile(n, target):
    t = min(n, target)
    assert n % t == 0
    return t


def _layer(x, prev_shift, wkv0, past, w, n_heads):
    (g1a, g1b, f1g, f1u, f1d, gma, gmb, w_in, w_ba, w_bb, w_out,
     mu, w0, w2, a0, a2, g2, k_k, k_a, r_k, ln_w, ln_b,
     g2a, g2b, f2g, f2u, f2d) = w
    batch, t, d = x.shape
    n = batch * t
    mw = w_ba.shape[0]
    rw = w_bb.shape[0]
    rp = mu.shape[-1]
    widths = (mw, mw, mw, rp, d, d)
    row = lambda a: a.reshape(1, -1)

    x2 = x.reshape(n, d)
    h = _ffn(x2, row(g1a), row(g1b), f1g, f1u, f1d, _row_tile(n, 512))
    proj = _in_proj(h, row(gma), w_in, widths, _row_tile(n, MOBA_BLOCK), past is None)
    q, k, v, p_rw, ga, gb = proj[:6]

    if past is None:
        assert t % MOBA_BLOCK == 0
        kmean = _block_means(k, min(8, n // MOBA_BLOCK))
        o_a = _moba_prompt(q, proj[6], proj[7], kmean, batch, t, n_heads)
    else:
        cache_k, cache_v, page_table = past
        per_head = lambda a: a.reshape(batch, t * n_heads, HEAD_DIM)
        o_a = _moba_sample(per_head(q), per_head(k), per_head(v), cache_k, cache_v,
                           page_table).reshape(n, mw)

    if t % 64 == 0:
        chunk, t_pad, bt_pre, tt_pre, bt_scan = 64, t, 1, min(t, 512), min(batch, 2)
    else:
        chunk = 8
        t_pad = -(-t // chunk) * chunk
        bt_pre, tt_pre, bt_scan = min(batch, 16), t_pad, min(batch, 4)
    p3 = p_rw.reshape(batch, t, rp)
    p3p = jnp.pad(p3, ((0, 0), (0, t_pad - t), (0, 0))) if t_pad != t else p3
    pre_w = (row(mu), row(w0), w2, row(a0), a2, g2, row(k_k), row(k_a))
    r_, kh, v_, kk, b_, lw, g_ = _rwkv_pre(p3p, prev_shift.reshape(batch, 1, rp), t, pre_w, bt_pre, tt_pre)
    o_scan, s_fin = _rwkv_scan(r_, kh, v_, kk, b_, lw, _pair_state(wkv0.astype(f32)), bt_scan, chunk)
    wkv = _unpair_state(s_fin, rw // HEAD_DIM).astype(wkv0.dtype)
    unpad = lambda a: a[:, :t].reshape(n, rw)
    consts = (row(r_k), row(ln_w), row(ln_b), _head_ones(rw), w_ba, w_bb, w_out, row(gmb))
    h2 = _merge(h, o_a, unpad(o_scan), unpad(r_), unpad(kh), unpad(v_), unpad(g_), ga, gb, consts,
                _row_tile(n, 256))
    y = _ffn(h2, row(g2a), row(g2b), f2g, f2u, f2d, _row_tile(n, 512))
    kv_shape = (batch, t, n_heads, HEAD_DIM)
    return y.reshape(batch, t, d), k.reshape(kv_shape), v.reshape(kv_shape), wkv, p3[:, -1]


def kernel(x_prompt, x_sample, cache_k, cache_v, state_wkv, state_shift, page_table,
           g_ffn1_pre, g_ffn1_post, w_ffn1_gate, w_ffn1_up, w_ffn1_down,
           g_mix_pre, g_mix_post, w_in, w_branch_a, w_branch_b, w_out,
           rwkv_mu, rwkv_w0, rwkv_w2, rwkv_a0, rwkv_a2, rwkv_g2, rwkv_k_k, rwkv_k_a, rwkv_r_k,
           rwkv_ln_w, rwkv_ln_b,
           g_ffn2_pre, g_ffn2_post, w_ffn2_gate, w_ffn2_up, w_ffn2_down):
    depth = w_in.shape[0]
    bp = x_prompt.shape[0]
    n_heads = cache_k.shape[3]
    rp = rwkv_mu.shape[-1]
    rw = rwkv_w0.shape[-1]
    yp, ys = x_prompt, x_sample
    outs = [[] for _ in range(8)]
    for i in range(depth):
        cast = lambda a: a[i].astype(bf16)
        w = (g_ffn1_pre[i], g_ffn1_post[i], cast(w_ffn1_gate), cast(w_ffn1_up), cast(w_ffn1_down),
             g_mix_pre[i], g_mix_post[i], cast(w_in), cast(w_branch_a), cast(w_branch_b), cast(w_out),
             rwkv_mu[i], rwkv_w0[i], rwkv_w2[i], rwkv_a0[i], rwkv_a2[i], rwkv_g2[i], rwkv_k_k[i],
             rwkv_k_a[i], rwkv_r_k[i], rwkv_ln_w[i], rwkv_ln_b[i],
             g_ffn2_pre[i], g_ffn2_post[i], cast(w_ffn2_gate), cast(w_ffn2_up), cast(w_ffn2_down))
        yp, kp, vp, wkvp, shp = _layer(
            yp, jnp.zeros((bp, rp), x_prompt.dtype),
            jnp.zeros((bp, rw // HEAD_DIM, HEAD_DIM, HEAD_DIM), state_wkv.dtype), None, w, n_heads)
        ys, ksn, vsn, wkvs, shs = _layer(
            ys, state_shift[i], state_wkv[i], (cache_k[i], cache_v[i], page_table), w, n_heads)
        for lst, val in zip(outs, (kp, vp, ksn, vsn, wkvp, shp, wkvs, shs)):
            lst.append(val)
    return (yp, ys) + tuple(jnp.stack(lst) for lst in outs)
```

```python
import functools
import math

import jax
import jax.numpy as jnp
import numpy as np
from jax import lax
from jax.experimental import pallas as pl
from jax.experimental.pallas import tpu as pltpu

HEAD_DIM = 64
LANES = 128
HEADS_PER_TILE = LANES // HEAD_DIM
MOBA_BLOCK = 256
MOBA_TOPK = 3
RMS_EPS = 1e-6
GN_EPS = 64e-5
VMEM_LIMIT = 56 * 1024 * 1024
NEG_INF = float("-inf")
POS_INF = float("inf")
LOG2E = 1.4426950408889634

f32 = jnp.float32
bf16 = jnp.bfloat16


def _params(*sem):
    return pltpu.CompilerParams(dimension_semantics=sem, vmem_limit_bytes=VMEM_LIMIT)


def _const_spec(shape):
    nd = len(shape)
    return pl.BlockSpec(shape, lambda *_: (0,) * nd, pipeline_mode=pl.Buffered(1))


def _rms(x, g):
    return x * lax.rsqrt(jnp.mean(x * x, axis=-1, keepdims=True) + RMS_EPS) * g


def _split3(x):
    hi = x.astype(bf16)
    r1 = x - hi.astype(f32)
    mid = r1.astype(bf16)
    lo = (r1 - mid.astype(f32)).astype(bf16)
    return hi, mid, lo


_NN = (((1,), (0,)), ((), ()))
_NT = (((1,), (1,)), ((), ()))
_TN = (((0,), (0,)), ((), ()))


def _dg(a, b, dims):
    return lax.dot_general(a, b, dims, preferred_element_type=f32)


def _dot_lo(a, b, dims=_NN):
    return _dg(a.astype(bf16), b.astype(bf16), dims)


def _dot_hi(a, b, dims=_NN):
    ah, am, _ = _split3(a)
    bh, bm, _ = _split3(b)
    return _dg(ah, bh, dims) + (_dg(ah, bm, dims) + _dg(am, bh, dims))


def _dot_exact_lhs(a_bf, b, dims=_NN):
    bh, bm, bl = _split3(b)
    return _dg(a_bf, bh, dims) + (_dg(a_bf, bm, dims) + _dg(a_bf, bl, dims))


def _dot_exact_rhs(a, b_bf, dims=_NN):
    ah = a.astype(bf16)
    am = (a - ah.astype(f32)).astype(bf16)
    return _dg(ah, b_bf, dims) + _dg(am, b_bf, dims)


def _ffn_kernel(x_ref, gpre_ref, gpost_ref, wg_ref, wu_ref, wd_ref, o_ref):
    x = x_ref[...]
    h = _rms(x, gpre_ref[...]).astype(bf16)
    gate = jnp.dot(h, wg_ref[...], preferred_element_type=f32)
    up = jnp.dot(h, wu_ref[...], preferred_element_type=f32)
    act = (gate * jax.nn.sigmoid(gate) * up).astype(bf16)
    y = jnp.dot(act, wd_ref[...], preferred_element_type=f32)
    o_ref[...] = x + 0.5 * _rms(y, gpost_ref[...])


def _ffn(x, g_pre, g_post, wg, wu, wd, tm):
    n, d = x.shape
    dff = wg.shape[1]
    return pl.pallas_call(
        _ffn_kernel,
        grid=(n // tm,),
        in_specs=[
            pl.BlockSpec((tm, d), lambda i: (i, 0)),
            _const_spec((1, d)), _const_spec((1, d)),
            _const_spec((d, dff)), _const_spec((d, dff)), _const_spec((dff, d)),
        ],
        out_specs=pl.BlockSpec((tm, d), lambda i: (i, 0)),
        out_shape=jax.ShapeDtypeStruct((n, d), f32),
        compiler_params=_params("parallel"),
        name="ffn",
    )(x, g_pre, g_post, wg, wu, wd)


def _in_proj_kernel(splits, blocked, h_ref, g_ref, w_ref, q_ref, p_ref, ga_ref, gb_ref, *kv_refs):
    u = _rms(h_ref[...], g_ref[...]).astype(bf16)
    proj = lambda idx: _dg(u, w_ref[:, splits[idx]:splits[idx + 1]], _NN)
    q_ref[...] = proj(0)
    p_ref[...] = proj(3)
    ga_ref[...] = jax.nn.sigmoid(proj(4))
    gb_ref[...] = jax.nn.sigmoid(proj(5))
    k, v = proj(1), proj(2)
    if blocked:
        kt_ref, vt_ref, kb_ref, vtb_ref, km_ref = kv_refs
        v_t = v.T
        kt_ref[0] = k.T
        vt_ref[0] = v_t
        kb_ref[...] = k.astype(bf16)
        vtb_ref[0] = v_t.astype(bf16)
        km_ref[0] = jnp.mean(k, axis=0, keepdims=True)
    else:
        kv_refs[0][...] = k
        kv_refs[1][...] = v


def _in_proj(h, g, w_in, widths, tm, seq=None):
    n, d = h.shape
    splits = tuple(int(s) for s in np.concatenate([[0], np.cumsum(widths)]))
    assert splits[-1] == w_in.shape[1] and all(s % LANES == 0 for s in splits)
    mw = widths[1]
    row = lambda w, dt=f32: (pl.BlockSpec((tm, w), lambda i: (i, 0)), jax.ShapeDtypeStruct((n, w), dt))
    outs = [row(widths[0]), row(widths[3]), row(widths[4]), row(widths[5])]
    if seq is None:
        outs += [row(mw), row(mw)]
    else:
        assert tm == MOBA_BLOCK and seq % tm == 0
        nbs = seq // tm
        t_spec = pl.BlockSpec((1, mw, tm), lambda i: (i // nbs, 0, i % nbs))
        t_shape = jax.ShapeDtypeStruct((n // seq, mw, seq), f32)
        outs += [(t_spec, t_shape), (t_spec, t_shape), row(mw, bf16),
                 (pl.BlockSpec((1, mw, tm), lambda i: (i, 0, 0)), jax.ShapeDtypeStruct((n // tm, mw, tm), bf16)),
                 (pl.BlockSpec((1, 1, mw), lambda i: (i, 0, 0)), jax.ShapeDtypeStruct((n // tm, 1, mw), f32))]
    return pl.pallas_call(
        functools.partial(_in_proj_kernel, splits, seq is not None),
        grid=(n // tm,),
        in_specs=[
            pl.BlockSpec((tm, d), lambda i: (i, 0)),
            _const_spec((1, d)),
            _const_spec(w_in.shape),
        ],
        out_specs=[o[0] for o in outs],
        out_shape=[o[1] for o in outs],
        compiler_params=_params("parallel"),
        name="in_proj",
    )(h, g, w_in)


def _top_blocks(gate, topk):
    blk = lax.broadcasted_iota(jnp.int32, gate.shape, 0).astype(f32)
    sel = jnp.zeros(gate.shape, f32)
    big = float(gate.shape[0])
    for _ in range(topk):
        m = jnp.max(gate, axis=0, keepdims=True)
        cand = (gate == m) & (gate > NEG_INF)
        idx = jnp.min(jnp.where(cand, blk, big), axis=0, keepdims=True)
        hit = blk == idx
        sel = jnp.where(hit, 1.0, sel)
        gate = jnp.where(hit, NEG_INF, gate)
    return sel


def _moba_prompt_kernel(n_heads, q_ref, kb_ref, vt_ref, km_ref, o_ref, off_sc, s_sc):
    pair = pl.program_id(1)
    qi = pl.program_id(2)
    tq = q_ref.shape[0]
    nb = km_ref.shape[0]
    lane = lax.broadcasted_iota(jnp.int32, (1, LANES), 1)
    key_i = lax.broadcasted_iota(jnp.int32, (MOBA_BLOCK, tq), 0)
    qry_i = lax.broadcasted_iota(jnp.int32, (MOBA_BLOCK, tq), 1)
    rel = (qry_i - key_i).astype(f32)
    causal = key_i <= qry_i
    blk_row = lax.broadcasted_iota(jnp.int32, (nb, tq), 0)

    q = q_ref[...]
    km = km_ref[...]
    own = pl.ds(pl.multiple_of(qi * MOBA_BLOCK, MOBA_BLOCK), MOBA_BLOCK)
    k_own = kb_ref[own, :]
    vt_own = vt_ref[qi]

    consts, carry = [], []
    for j in range(HEADS_PER_TILE):
        head_mask = (lane // HEAD_DIM == j).astype(f32)
        head = jnp.full((1, 1), pair * HEADS_PER_TILE + j + 1, jnp.int32).astype(f32)
        slope2 = jnp.exp2(head * (-8.0 / n_heads)) * LOG2E
        gate = _dot_hi(km * head_mask, q, _NT)
        gate = jnp.where(blk_row < qi, gate, NEG_INF)
        off_sc[j] = jnp.where(_top_blocks(gate, MOBA_TOPK) > 0.5, 0.0, POS_INF)
        qs = (q * head_mask * (HEAD_DIM ** -0.5 * LOG2E)).astype(bf16)
        alibi = slope2 * rel

        s = _dg(k_own, qs, _NT) - alibi
        s = jnp.where(causal, s, NEG_INF)
        m = jnp.max(s, axis=0, keepdims=True)
        p = jnp.exp2(s - m)
        l = jnp.sum(p, axis=0, keepdims=True)
        acc = _dg(vt_own, p.astype(bf16), _NN)
        consts.append((qs, alibi, slope2))
        carry += [m, l, acc]

    def score(n, slot):
        nc = jnp.minimum(n, nb - 1)
        kb = kb_ref[pl.ds(pl.multiple_of(nc * MOBA_BLOCK, MOBA_BLOCK), MOBA_BLOCK), :]
        raw = []
        for j in range(HEADS_PER_TILE):
            s = _dg(kb, consts[j][0], _NT) - consts[j][1]
            s_sc[slot, j] = s
            raw.append(jnp.max(s, axis=0, keepdims=True))
        return raw

    def attend(n, slot, raw, stats):
        nc = jnp.minimum(n, nb - 1)
        vt = vt_ref[nc]
        out = []
        for j in range(HEADS_PER_TILE):
            slope2 = consts[j][2]
            m, l, acc = stats[3 * j:3 * j + 3]
            off = jnp.where(n < qi, off_sc[j, pl.ds(nc, 1), :], POS_INF)
            far = ((qi - n) * MOBA_BLOCK).astype(f32) * slope2
            m_new = jnp.maximum(m, raw[j] - far - off)
            alpha = jnp.exp2(m - m_new)
            p = jnp.exp2(s_sc[slot, j] - (m_new + far + off))
            l = l * alpha + jnp.sum(p, axis=0, keepdims=True)
            acc = acc * alpha + _dg(vt, p.astype(bf16), _NN)
            out += [m_new, l, acc]
        return out

    def body(i, carry):
        stats, raw_even = list(carry[:-HEADS_PER_TILE]), list(carry[-HEADS_PER_TILE:])
        n = 2 * i
        raw_odd = score(n + 1, 1)
        stats = attend(n, 0, raw_even, stats)
        raw_even = score(n + 2, 0)
        stats = attend(n + 1, 1, raw_odd, stats)
        return tuple(stats + raw_even)

    carry = lax.fori_loop(0, (qi + 1) // 2, body, tuple(carry + score(0, 0)))
    outs = [carry[3 * j + 2] / carry[3 * j + 1] for j in range(HEADS_PER_TILE)]
    out_t = jnp.concatenate([outs[j][j * HEAD_DIM:(j + 1) * HEAD_DIM] for j in range(HEADS_PER_TILE)],
                            axis=0)
    o_ref[...] = out_t.T


def _moba_prompt(q, k_bf, v_t, kmean, batch, seq, n_heads):
    n, w = q.shape
    n_pairs = w // LANES
    nb = seq // MOBA_BLOCK
    tq = MOBA_BLOCK
    return pl.pallas_call(
        functools.partial(_moba_prompt_kernel, n_heads),
        grid=(batch, n_pairs, nb),
        in_specs=[
            pl.BlockSpec((tq, LANES), lambda b, p, i: (b * nb + i, p)),
            pl.BlockSpec((seq, LANES), lambda b, p, i: (b, p)),
            pl.BlockSpec((nb, LANES, MOBA_BLOCK), lambda b, p, i: (b, p, 0)),
            pl.BlockSpec((nb, LANES), lambda b, p, i: (b, p)),
        ],
        out_specs=pl.BlockSpec((tq, LANES), lambda b, p, i: (b * nb + i, p)),
        out_shape=jax.ShapeDtypeStruct((n, w), f32),
        scratch_shapes=[pltpu.VMEM((HEADS_PER_TILE, nb, tq), f32),
                        pltpu.VMEM((2, HEADS_PER_TILE, MOBA_BLOCK, tq), f32)],
        compiler_params=_params("parallel", "parallel", "arbitrary"),
        name="moba_prompt",
    )(q, k_bf, v_t, kmean)


def _per_head(a, b, dims):
    a, b = a.astype(bf16), b.astype(bf16)
    return jnp.stack([_dg(a[h], b[h], dims) for h in range(a.shape[0])], axis=0)


def _moba_sample_kernel(t_new, pt_ref, q_ref, qt_ref, kn_ref, vn_ref,
                        k0_ref, k1_ref, v0_ref, v1_ref, o_ref,
                        m_sc, l_sc, g_sc, acc_sc):
    del pt_ref
    j = pl.program_id(1)
    n_blocks = pl.num_programs(1)
    n_heads, t_pad = q_ref.shape[1], q_ref.shape[2]
    page = k0_ref.shape[3]
    q = q_ref[0]
    head = lax.broadcasted_iota(jnp.int32, (n_heads, 1, 1), 0)
    slope = jnp.exp2((head + 1).astype(f32) * (-8.0 / n_heads))
    t_q = lax.broadcasted_iota(jnp.int32, (1, t_pad, 1), 1)
    qs = (q * (HEAD_DIM ** -0.5)).astype(bf16)
    tok = lax.broadcasted_iota(jnp.int32, (1, 1, page), 2)

    def scores(kt, first_pos):
        dist = (n_blocks * MOBA_BLOCK - first_pos + t_q - tok).astype(f32)
        return _per_head(qs, kt, _NN) - slope * dist

    k0, k1 = k0_ref[0], k1_ref[0]
    base = j * MOBA_BLOCK
    s0 = scores(k0, base)
    s1 = scores(k1, base + page)
    m = jnp.maximum(jnp.max(s0, axis=-1, keepdims=True), jnp.max(s1, axis=-1, keepdims=True))
    p0 = jnp.exp(s0 - m)
    p1 = jnp.exp(s1 - m)
    l = jnp.sum(p0, axis=-1, keepdims=True) + jnp.sum(p1, axis=-1, keepdims=True)
    acc = _per_head(p0, v0_ref[0], _NT) + _per_head(p1, v1_ref[0], _NT)
    kmean = jnp.sum(k0 + k1, axis=-1, keepdims=True) * (1.0 / MOBA_BLOCK)
    gate = jnp.sum(qt_ref[0] * kmean, axis=1, keepdims=True)
    m_sc[j] = jnp.broadcast_to(m, (n_heads, t_pad, LANES))
    l_sc[j] = jnp.broadcast_to(l, (n_heads, t_pad, LANES))
    g_sc[j] = gate
    acc_sc[j] = acc

    @pl.when(j == n_blocks - 1)
    def _():
        nb = m_sc.shape[0]
        t_k = lax.broadcasted_iota(jnp.int32, (1, t_pad, t_pad), 2)
        s_own = _per_head(qs, kn_ref[0], _NT) - slope * (t_q - t_k).astype(f32)
        s_own = jnp.where((t_k <= t_q) & (t_k < t_new), s_own, NEG_INF)
        m_tot = jnp.max(s_own, axis=-1, keepdims=True)
        gates = [g_sc[n] for n in range(nb)]
        eye = (lax.broadcasted_iota(jnp.int32, (1, t_pad, t_pad), 1) == t_k).astype(f32)
        chosen = []
        for n in range(nb):
            rank = jnp.zeros_like(gates[n])
            for i in range(nb):
                if i == n:
                    continue
                ahead = (gates[i] > gates[n]) | ((gates[i] == gates[n]) & (i < n))
                rank = rank + ahead.astype(f32)
            picked = (rank < MOBA_TOPK).astype(f32)
            chosen.append(jnp.sum(eye * picked, axis=-1, keepdims=True) > 0.5)
        for n in range(nb):
            m_tot = jnp.maximum(m_tot, jnp.where(chosen[n], m_sc[n][:, :, 0:1], NEG_INF))
        p_own = jnp.exp(s_own - m_tot)
        l_tot = jnp.sum(p_own, axis=-1, keepdims=True)
        acc_tot = _per_head(p_own, vn_ref[0], _NN)
        for n in range(nb):
            wgt = jnp.where(chosen[n], jnp.exp(m_sc[n][:, :, 0:1] - m_tot), 0.0)
            l_tot = l_tot + wgt * l_sc[n][:, :, 0:1]
            acc_tot = acc_tot + wgt * acc_sc[n]
        o_ref[0] = acc_tot / l_tot


def _moba_sample(q, k_new, v_new, cache_k, cache_v, page_table):
    batch, t_new, n_heads, dim = q.shape
    n_pool, page = cache_k.shape[:2]
    n_pages = page_table.shape[1]
    t_pad = 8
    assert dim == HEAD_DIM and MOBA_BLOCK == 2 * page and n_pages % 2 == 0 and t_new <= t_pad
    n_blocks = n_pages // 2
    assert n_blocks >= MOBA_TOPK
    cache_kt = jnp.transpose(cache_k, (0, 2, 3, 1))
    cache_vt = jnp.transpose(cache_v, (0, 2, 3, 1))
    by_head = lambda a: jnp.pad(jnp.transpose(a, (0, 2, 1, 3)), ((0, 0), (0, 0), (0, t_pad - t_new), (0, 0)))
    qh, kh, vh = by_head(q), by_head(k_new), by_head(v_new)
    qt = jnp.transpose(qh, (0, 1, 3, 2))

    def page_spec(which):
        return pl.BlockSpec((1, n_heads, dim, page), lambda b, j, pt: (pt[b, 2 * j + which], 0, 0, 0))

    row_spec = pl.BlockSpec((1, n_heads, t_pad, dim), lambda b, j, pt: (b, 0, 0, 0))
    out = pl.pallas_call(
        functools.partial(_moba_sample_kernel, t_new),
        grid_spec=pltpu.PrefetchScalarGridSpec(
            num_scalar_prefetch=1,
            grid=(batch, n_blocks),
            in_specs=[row_spec, pl.BlockSpec((1, n_heads, dim, t_pad), lambda b, j, pt: (b, 0, 0, 0)),
                      row_spec, row_spec,
                      page_spec(0), page_spec(1), page_spec(0), page_spec(1)],
            out_specs=row_spec,
            scratch_shapes=[
                pltpu.VMEM((n_blocks, n_heads, t_pad, LANES), f32),
                pltpu.VMEM((n_blocks, n_heads, t_pad, LANES), f32),
                pltpu.VMEM((n_blocks, n_heads, 1, t_pad), f32),
                pltpu.VMEM((n_blocks, n_heads, t_pad, dim), f32),
            ],
        ),
        out_shape=jax.ShapeDtypeStruct((batch, n_heads, t_pad, dim), f32),
        compiler_params=_params("parallel", "arbitrary"),
        name="moba_sample",
    )(page_table, qh, qt, kh, vh, cache_kt, cache_kt, cache_vt, cache_vt)
    return jnp.transpose(out[:, :, :t_new], (0, 2, 1, 3))


def _rwkv_pre_kernel(t_valid, widths, p_ref, prev_ref, mu_ref, w0_ref, w2_ref, a0_ref, a2_ref,
                     g2_ref, kk_ref, ka_ref, ones_ref,
                     r_out, k_out, v_out, kk_out, b_out, lw_out, g_out, carry):
    ti = pl.program_id(1)
    bt, tt, width = p_ref.shape
    rw, dl, al, gl = widths

    @pl.when(ti == 0)
    def _():
        carry[...] = prev_ref[...]

    p = p_ref[...]
    t_idx = lax.broadcasted_iota(jnp.int32, (bt, tt, 1), 1)
    shifted = jnp.where(t_idx == 0, carry[...], pltpu.roll(p, 1, axis=1))
    carry[...] = p[:, tt - 1:tt, :]
    m = (p + (shifted - p) * mu_ref[...]).reshape(bt * tt, width)

    r = m[:, 0:rw]
    k = m[:, rw:2 * rw]
    v = m[:, 2 * rw:3 * rw]
    lora = m[:, 3 * rw:3 * rw + dl + al]
    gd = m[:, 3 * rw + dl + al:3 * rw + dl + al + gl]

    z = -(w0_ref[...] + _dot_hi(jnp.tanh(lora), w2_ref[...]))
    softplus = jnp.maximum(z, 0.0) + jnp.log(1.0 + jnp.exp(-jnp.abs(z)))
    w = -softplus - 0.5
    log_decay = -jnp.exp(w)
    a = jax.nn.sigmoid(a0_ref[...] + _dot_hi(lora, a2_ref[...]))
    g = _dot_hi(jax.nn.sigmoid(gd), g2_ref[...])
    kk = k * kk_ref[...]
    norm = jnp.sqrt(_dot_exact_rhs(kk * kk, ones_ref[...]))
    kk = kk / jnp.maximum(norm, 1e-12)
    kh = k * (1.0 + (a - 1.0) * ka_ref[...])

    live = (lax.broadcasted_iota(jnp.int32, (bt, tt, 1), 1) + ti * tt < t_valid)
    live = live.astype(f32).reshape(bt * tt, 1)
    shape3 = (bt, tt, rw)
    r_out[...] = (r * live).reshape(shape3)
    k_out[...] = (kh * live).reshape(shape3)
    v_out[...] = (v * live).reshape(shape3)
    kk_out[...] = (kk * live).reshape(shape3)
    b_out[...] = (kk * a * live).reshape(shape3)
    lw_out[...] = (log_decay * live).reshape(shape3)
    g_out[...] = g.reshape(shape3)


def _head_ones(width):
    idx = np.arange(width) // HEAD_DIM
    return jnp.asarray(idx[:, None] == idx[None, :], dtype=bf16)


def _rwkv_pre(p, prev, t_valid, weights, bt, tt):
    batch, t_pad, width = p.shape
    mu, w0, w2, a0, a2, g2, k_k, k_a = weights
    rw = w0.shape[-1]
    widths = (rw, w2.shape[0], a2.shape[0], g2.shape[0])
    assert 3 * rw + sum(widths[1:]) == width
    assert widths[1] + widths[2] == LANES and widths[3] % LANES == 0
    w2p = jnp.pad(w2, ((0, widths[2]), (0, 0)))
    a2p = jnp.pad(a2, ((widths[1], 0), (0, 0)))
    consts = (mu, w0, w2p, a0, a2p, g2, k_k, k_a, _head_ones(rw))
    out_spec = pl.BlockSpec((bt, tt, rw), lambda b, t: (b, t, 0))
    return pl.pallas_call(
        functools.partial(_rwkv_pre_kernel, t_valid, widths),
        grid=(batch // bt, t_pad // tt),
        in_specs=[pl.BlockSpec((bt, tt, width), lambda b, t: (b, t, 0)),
                  pl.BlockSpec((bt, 1, width), lambda b, t: (b, 0, 0))]
                 + [_const_spec(c.shape) for c in consts],
        out_specs=[out_spec] * 7,
        out_shape=[jax.ShapeDtypeStruct((batch, t_pad, rw), f32)] * 7,
        scratch_shapes=[pltpu.VMEM((bt, 1, width), f32)],
        compiler_params=_params("parallel", "arbitrary"),
        name="rwkv_pre",
    )(p, prev, *consts)


def _rwkv_scan_kernel(r_ref, k_ref, v_ref, kk_ref, b_ref, lw_ref, s0_ref, o_ref, s_out, state):
    ci = pl.program_id(1)
    bt, c, width = r_ref.shape
    n_pairs = width // LANES
    c2 = HEADS_PER_TILE * c
    n_double = int(math.log2(c))
    assert 1 << n_double == c and HEADS_PER_TILE == 2

    @pl.when(ci == 0)
    def _():
        state[...] = s0_ref[...]

    ri = lax.broadcasted_iota(jnp.int32, (c, c), 0)
    cj = lax.broadcasted_iota(jnp.int32, (c, c), 1)
    tri_bf = (cj <= ri).astype(bf16)
    r2 = lax.broadcasted_iota(jnp.int32, (c2, c2), 0)
    q2 = lax.broadcasted_iota(jnp.int32, (c2, c2), 1)
    same_blk = (r2 >= c) == (q2 >= c)
    strict2 = (same_blk & (q2 < r2)).astype(f32)
    incl2 = (same_blk & (q2 <= r2)).astype(f32)
    eye2 = (q2 == r2).astype(f32)
    lane = lax.broadcasted_iota(jnp.int32, (1, LANES), 1)
    hm = [(lane // HEAD_DIM == j).astype(f32) for j in range(HEADS_PER_TILE)]
    bi_ = lax.broadcasted_iota(jnp.int32, (LANES, LANES), 0) // HEAD_DIM
    bj_ = lax.broadcasted_iota(jnp.int32, (LANES, LANES), 1) // HEAD_DIM
    block_diag = (bi_ == bj_).astype(f32)

    def stack(x):
        return jnp.concatenate([x * hm[0], x * hm[1]], axis=0)

    def twice(x):
        return jnp.concatenate([x, x], axis=0)

    def unstack(x2):
        return x2[:c] * hm[0] + x2[c:] * hm[1]

    probs = [(bi, pl.ds(pair * LANES, LANES), pair) for bi in range(bt) for pair in range(n_pairs)]
    ld = lambda ref: [ref[bi, :, sl] for bi, sl, _ in probs]
    r, k, v, kk, b, lw = ld(r_ref), ld(k_ref), ld(v_ref), ld(kk_ref), ld(b_ref), ld(lw_ref)
    nprob = len(probs)
    each = range(nprob)

    cum = [_dot_exact_lhs(tri_bf, lw[i]) for i in each]
    cum_end = [cum[i][c - 1:c, :] for i in each]
    e_neg = [jnp.exp(-cum[i]) for i in each]
    e_end = [jnp.exp(cum_end[i] - cum[i]) for i in each]
    a_mat = [kk[i] * jnp.exp(cum[i] - lw[i]) for i in each]
    p_mat = [r[i] * jnp.exp(cum[i]) for i in each]
    xa = [stack(a_mat[i]) for i in each]
    xap = [jnp.concatenate([xa[i], stack(p_mat[i])], axis=0) for i in each]
    yb = [stack(b[i] * e_neg[i]) for i in each]
    yk = [stack(k[i] * e_neg[i]) for i in each]
    v2 = [twice(v[i]) for i in each]

    gb = [_dot_lo(xap[i], yb[i], _NT) for i in each]
    gk = [_dot_lo(xap[i], yk[i], _NT) for i in each]
    l_mat = [gb[i][:c2] * strict2 for i in each]
    m_mat = [gk[i][:c2] * strict2 for i in each]
    lr_mat = [gb[i][c2:] * incl2 for i in each]
    mr_mat = [gk[i][c2:] * incl2 for i in each]

    npow = [-l_mat[i] for i in each]
    t_inv = [eye2 + npow[i] for i in each]
    for _ in range(n_double - 1):
        npow = [_dot_lo(npow[i], npow[i]) for i in each]
        t_inv = [t_inv[i] + _dot_lo(npow[i], t_inv[i]) for i in each]
    mv = [_dot_lo(m_mat[i], v2[i]) for i in each]
    tz = [_dot_lo(t_inv[i], jnp.concatenate([xa[i], mv[i]], axis=1)) for i in each]
    ta = [tz[i][:c, :LANES] + tz[i][c:, :LANES] for i in each]
    tmv = [unstack(tz[i][:, LANES:]) for i in each]

    s_prev = [state[bi, pair] for bi, _, pair in probs]
    xs = [_dot_hi(jnp.concatenate([ta[i], p_mat[i]], axis=0), s_prev[i], _NT) for i in each]
    u = [-(xs[i][:c] + tmv[i]) for i in each]
    corr = [_dot_hi(lr_mat[i], twice(u[i])) + _dot_hi(mr_mat[i], v2[i]) for i in each]
    upd = [_dot_hi(jnp.concatenate([u[i], v[i]], axis=0),
                   jnp.concatenate([b[i] * e_end[i], k[i] * e_end[i]], axis=0), _TN) for i in each]
    for i, (bi, sl, pair) in enumerate(probs):
        state[bi, pair] = s_prev[i] * jnp.exp(cum_end[i]) + upd[i] * block_diag
        o_ref[bi, :, sl] = xs[i][c:] + unstack(corr[i])

    @pl.when(ci == pl.num_programs(1) - 1)
    def _():
        s_out[...] = state[...]


def _rwkv_scan(r, k, v, kk, b, lw, s0, bt, chunk):
    batch, t_pad, width = r.shape
    n_pairs = width // LANES
    seq_spec = pl.BlockSpec((bt, chunk, width), lambda bb, c: (bb, c, 0))
    st_spec = pl.BlockSpec((bt, n_pairs, LANES, LANES), lambda bb, c: (bb, 0, 0, 0))
    return pl.pallas_call(
        _rwkv_scan_kernel,
        grid=(batch // bt, t_pad // chunk),
        in_specs=[seq_spec] * 6 + [st_spec],
        out_specs=[seq_spec, st_spec],
        out_shape=[jax.ShapeDtypeStruct((batch, t_pad, width), f32),
                   jax.ShapeDtypeStruct((batch, n_pairs, LANES, LANES), f32)],
        scratch_shapes=[pltpu.VMEM((bt, n_pairs, LANES, LANES), f32)],
        compiler_params=_params("parallel", "arbitrary"),
        name="rwkv_scan",
    )(r, k, v, kk, b, lw, s0)


def _pair_state(s):
    bsz, h, n, _ = s.shape
    s = s.reshape(bsz, h // HEADS_PER_TILE, HEADS_PER_TILE, n, n)
    eye = jnp.eye(HEADS_PER_TILE, dtype=s.dtype)
    out = jnp.einsum("bpjvk,ji->bpjvik", s, eye)
    return out.reshape(bsz, h // HEADS_PER_TILE, LANES, LANES)


def _unpair_state(s, heads):
    bsz = s.shape[0]
    s = s.reshape(bsz, heads // HEADS_PER_TILE, HEADS_PER_TILE, HEAD_DIM, HEADS_PER_TILE, HEAD_DIM)
    diag = jnp.stack([s[:, :, j, :, j, :] for j in range(HEADS_PER_TILE)], axis=2)
    return diag.reshape(bsz, heads, HEAD_DIM, HEAD_DIM)


def _merge_kernel(h_ref, oa_ref, o_ref, r_ref, k_ref, v_ref, g_ref, ga_ref, gb_ref,
                  rk_ref, lnw_ref, lnb_ref, ones_ref, wba_ref, wbb_ref, wout_ref, gpost_ref,
                  out_ref):
    ones = ones_ref[...]
    inv = 1.0 / HEAD_DIM
    o = o_ref[...]
    mean = _dot_exact_rhs(o, ones) * inv
    d = o - mean
    var = _dot_exact_rhs(d * d, ones) * inv
    on = d * lax.rsqrt(var + GN_EPS) * lnw_ref[...] + lnb_ref[...]
    bonus = _dot_exact_rhs(r_ref[...] * k_ref[...] * rk_ref[...], ones) * v_ref[...]
    ob = (on + bonus) * g_ref[...]
    merged = (ga_ref[...] * jnp.dot(oa_ref[...].astype(bf16), wba_ref[...], preferred_element_type=f32)
              + gb_ref[...] * jnp.dot(ob.astype(bf16), wbb_ref[...], preferred_element_type=f32))
    y = jnp.dot(merged.astype(bf16), wout_ref[...], preferred_element_type=f32)
    out_ref[...] = h_ref[...] + _rms(y, gpost_ref[...])


def _merge(h, oa, o, r, k, v, g, ga, gb, consts, tm):
    n, d = h.shape
    w = oa.shape[1]
    row = lambda width: pl.BlockSpec((tm, width), lambda i: (i, 0))
    return pl.pallas_call(
        _merge_kernel,
        grid=(n // tm,),
        in_specs=[row(d)] + [row(w)] * 6 + [row(d)] * 2 + [_const_spec(c.shape) for c in consts],
        out_specs=row(d),
        out_shape=jax.ShapeDtypeStruct((n, d), f32),
        compiler_params=_params("parallel"),
        name="merge",
    )(h, oa, o, r, k, v, g, ga, gb, *consts)


def _row_tile(n, target):
    t = min(n, target)
    assert n % t == 0
    return t


def _layer(x, prev_shift, wkv0, past, w, n_heads):
    (g1a, g1b, f1g, f1u, f1d, gma, gmb, w_in, w_ba, w_bb, w_out,
     mu, w0, w2, a0, a2, g2, k_k, k_a, r_k, ln_w, ln_b,
     g2a, g2b, f2g, f2u, f2d) = w
    batch, t, d = x.shape
    n = batch * t
    mw = w_ba.shape[0]
    rw = w_bb.shape[0]
    rp = mu.shape[-1]
    widths = (mw, mw, mw, rp, d, d)
    row = lambda a: a.reshape(1, -1)

    x2 = x.reshape(n, d)
    h = _ffn(x2, row(g1a), row(g1b), f1g, f1u, f1d, _row_tile(n, 512))
    kv_shape = (batch, t, n_heads, HEAD_DIM)
    if past is None:
        assert t % MOBA_BLOCK == 0
        q, p_rw, ga, gb, k_t, v_t, k_bf, vt_bf, kmean = _in_proj(
            h, row(gma), w_in, widths, MOBA_BLOCK, seq=t)
        o_a = _moba_prompt(q, k_bf, vt_bf, kmean.reshape(n // MOBA_BLOCK, mw), batch, t, n_heads)
        k, v = (jnp.transpose(a.reshape(batch, n_heads, HEAD_DIM, t), (0, 3, 1, 2)) for a in (k_t, v_t))
    else:
        cache_k, cache_v, page_table = past
        q, p_rw, ga, gb, k, v = _in_proj(h, row(gma), w_in, widths, _row_tile(n, MOBA_BLOCK))
        k, v = k.reshape(kv_shape), v.reshape(kv_shape)
        o_a = _moba_sample(q.reshape(kv_shape), k, v, cache_k, cache_v, page_table).reshape(n, mw)

    if t % 64 == 0:
        chunk, t_pad, bt_pre, tt_pre, bt_scan = 64, t, 1, min(t, 512), min(batch, 2)
    else:
        chunk = 8
        t_pad = -(-t // chunk) * chunk
        bt_pre, tt_pre, bt_scan = min(batch, 16), t_pad, min(batch, 4)
    p3 = p_rw.reshape(batch, t, rp)
    p3p = jnp.pad(p3, ((0, 0), (0, t_pad - t), (0, 0))) if t_pad != t else p3
    pre_w = (row(mu), row(w0), w2, row(a0), a2, g2, row(k_k), row(k_a))
    r_, kh, v_, kk, b_, lw, g_ = _rwkv_pre(p3p, prev_shift.reshape(batch, 1, rp), t, pre_w, bt_pre, tt_pre)
    o_scan, s_fin = _rwkv_scan(r_, kh, v_, kk, b_, lw, _pair_state(wkv0.astype(f32)), bt_scan, chunk)
    wkv = _unpair_state(s_fin, rw // HEAD_DIM).astype(wkv0.dtype)
    unpad = lambda a: a[:, :t].reshape(n, rw)
    consts = (row(r_k), row(ln_w), row(ln_b), _head_ones(rw), w_ba, w_bb, w_out, row(gmb))
    h2 = _merge(h, o_a, unpad(o_scan), unpad(r_), unpad(kh), unpad(v_), unpad(g_), ga, gb, consts,
                _row_tile(n, 256))
    y = _ffn(h2, row(g2a), row(g2b), f2g, f2u, f2d, _row_tile(n, 512))
    return y.reshape(batch, t, d), k, v, wkv, p3[:, -1]


def kernel(x_prompt, x_sample, cache_k, cache_v, state_wkv, state_shift, page_table,
           g_ffn1_pre, g_ffn1_post, w_ffn1_gate, w_ffn1_up, w_ffn1_down,
           g_mix_pre, g_mix_post, w_in, w_branch_a, w_branch_b, w_out,
           rwkv_mu, rwkv_w0, rwkv_w2, rwkv_a0, rwkv_a2, rwkv_g2, rwkv_k_k, rwkv_k_a, rwkv_r_k,
           rwkv_ln_w, rwkv_ln_b,
           g_ffn2_pre, g_ffn2_post, w_ffn2_gate, w_ffn2_up, w_ffn2_down):
    depth = w_in.shape[0]
    bp = x_prompt.shape[0]
    n_heads = cache_k.shape[3]
    rp = rwkv_mu.shape[-1]
    rw = rwkv_w0.shape[-1]
    yp, ys = x_prompt, x_sample
    outs = [[] for _ in range(8)]
    for i in range(depth):
        cast = lambda a: a[i].astype(bf16)
        w = (g_ffn1_pre[i], g_ffn1_post[i], cast(w_ffn1_gate), cast(w_ffn1_up), cast(w_ffn1_down),
             g_mix_pre[i], g_mix_post[i], cast(w_in), cast(w_branch_a), cast(w_branch_b), cast(w_out),
             rwkv_mu[i], rwkv_w0[i], rwkv_w2[i], rwkv_a0[i], rwkv_a2[i], rwkv_g2[i], rwkv_k_k[i],
             rwkv_k_a[i], rwkv_r_k[i], rwkv_ln_w[i], rwkv_ln_b[i],
             g_ffn2_pre[i], g_ffn2_post[i], cast(w_ffn2_gate), cast(w_ffn2_up), cast(w_ffn2_down))
        yp, kp, vp, wkvp, shp = _layer(
            yp, jnp.zeros((bp, rp), x_prompt.dtype),
            jnp.zeros((bp, rw // HEAD_DIM, HEAD_DIM, HEAD_DIM), state_wkv.dtype), None, w, n_heads)
        ys, ksn, vsn, wkvs, shs = _layer(
            ys, state_shift[i], state_wkv[i], (cache_k[i], cache_v[i], page_table), w, n_heads)
        for lst, val in zip(outs, (kp, vp, ksn, vsn, wkvp, shp, wkvs, shs)):
            lst.append(val)
    return (yp, ys) + tuple(jnp.stack(lst) for lst in outs)
```

```python
import functools
import math

import jax
import jax.numpy as jnp
import numpy as np
from jax import lax
from jax.experimental import pallas as pl
from jax.experimental.pallas import tpu as pltpu

HEAD_DIM = 64
LANES = 128
HEADS_PER_TILE = LANES // HEAD_DIM
MXU_DIM = 256
MOBA_BLOCK = 256
MOBA_TOPK = 3
SUM_ROWS = 16
PIPE_UNROLL = 4
RMS_EPS = 1e-6
GN_EPS = 64e-5
VMEM_LIMIT = 56 * 1024 * 1024
NEG_INF = float("-inf")
POS_INF = float("inf")
LOG2E = 1.4426950408889634

f32 = jnp.float32
bf16 = jnp.bfloat16


def _params(*sem):
    return pltpu.CompilerParams(dimension_semantics=sem, vmem_limit_bytes=VMEM_LIMIT)


def _const_spec(shape):
    nd = len(shape)
    return pl.BlockSpec(shape, lambda *_: (0,) * nd, pipeline_mode=pl.Buffered(1))


def _rms(x, g):
    return x * lax.rsqrt(jnp.mean(x * x, axis=-1, keepdims=True) + RMS_EPS) * g


def _split3(x):
    hi = x.astype(bf16)
    r1 = x - hi.astype(f32)
    mid = r1.astype(bf16)
    lo = (r1 - mid.astype(f32)).astype(bf16)
    return hi, mid, lo


_NN = (((1,), (0,)), ((), ()))
_NT = (((1,), (1,)), ((), ()))
_TN = (((0,), (0,)), ((), ()))


def _dg(a, b, dims):
    return lax.dot_general(a, b, dims, preferred_element_type=f32)


def _dot_lo(a, b, dims=_NN):
    return _dg(a.astype(bf16), b.astype(bf16), dims)


def _dot_hi(a, b, dims=_NN):
    ah, am, _ = _split3(a)
    bh, bm, _ = _split3(b)
    return _dg(ah, bh, dims) + (_dg(ah, bm, dims) + _dg(am, bh, dims))


def _dot_exact_lhs(a_bf, b, dims=_NN):
    bh, bm, bl = _split3(b)
    return _dg(a_bf, bh, dims) + (_dg(a_bf, bm, dims) + _dg(a_bf, bl, dims))


def _dot_exact_rhs(a, b_bf, dims=_NN):
    ah = a.astype(bf16)
    am = (a - ah.astype(f32)).astype(bf16)
    return _dg(ah, b_bf, dims) + _dg(am, b_bf, dims)


def _head_sums(x, ones_bf):
    g = ones_bf.shape[0]
    return jnp.concatenate([_dot_exact_rhs(x[:, c:c + g], ones_bf) for c in range(0, x.shape[1], g)],
                           axis=1)


def _ffn_kernel(x_ref, gpre_ref, gpost_ref, wg_ref, wu_ref, wd_ref, o_ref):
    x = x_ref[...]
    h = _rms(x, gpre_ref[...]).astype(bf16)
    gate = jnp.dot(h, wg_ref[...], preferred_element_type=f32)
    up = jnp.dot(h, wu_ref[...], preferred_element_type=f32)
    act = (gate * jax.nn.sigmoid(gate) * up).astype(bf16)
    y = jnp.dot(act, wd_ref[...], preferred_element_type=f32)
    o_ref[...] = x + 0.5 * _rms(y, gpost_ref[...])


def _ffn(x, g_pre, g_post, wg, wu, wd, tm):
    n, d = x.shape
    dff = wg.shape[1]
    return pl.pallas_call(
        _ffn_kernel,
        grid=(n // tm,),
        in_specs=[
            pl.BlockSpec((tm, d), lambda i: (i, 0)),
            _const_spec((1, d)), _const_spec((1, d)),
            _const_spec((d, dff)), _const_spec((d, dff)), _const_spec((dff, d)),
        ],
        out_specs=pl.BlockSpec((tm, d), lambda i: (i, 0)),
        out_shape=jax.ShapeDtypeStruct((n, d), f32),
        compiler_params=_params("parallel"),
        name="ffn",
    )(x, g_pre, g_post, wg, wu, wd)


def _in_proj_kernel(splits, blocked, h_ref, g_ref, w_ref, q_ref, p_ref, ga_ref, gb_ref, *kv_refs):
    u = _rms(h_ref[...], g_ref[...]).astype(bf16)
    proj = lambda idx: _dg(u, w_ref[:, splits[idx]:splits[idx + 1]], _NN)
    q_ref[...] = proj(0)
    p_ref[...] = proj(3)
    ga_ref[...] = jax.nn.sigmoid(proj(4))
    gb_ref[...] = jax.nn.sigmoid(proj(5))
    k, v = proj(1), proj(2)
    if blocked:
        kt_ref, vt_ref, kb_ref, vtb_ref, km_ref = kv_refs
        v_t = v.T
        kt_ref[0] = k.T
        vt_ref[0] = v_t
        kb_ref[...] = k.astype(bf16)
        vtb_ref[0] = v_t.astype(bf16)
        km_ref[0] = jnp.mean(k, axis=0, keepdims=True)
    else:
        kv_refs[0][...] = k
        kv_refs[1][...] = v


def _in_proj(h, g, w_in, widths, tm, seq=None):
    n, d = h.shape
    splits = tuple(int(s) for s in np.concatenate([[0], np.cumsum(widths)]))
    assert splits[-1] == w_in.shape[1] and all(s % LANES == 0 for s in splits)
    mw = widths[1]
    row = lambda w, dt=f32: (pl.BlockSpec((tm, w), lambda i: (i, 0)), jax.ShapeDtypeStruct((n, w), dt))
    outs = [row(widths[0]), row(widths[3]), row(widths[4]), row(widths[5])]
    if seq is None:
        outs += [row(mw), row(mw)]
    else:
        assert tm == MOBA_BLOCK and seq % tm == 0
        nbs = seq // tm
        t_spec = pl.BlockSpec((1, mw, tm), lambda i: (i // nbs, 0, i % nbs))
        t_shape = jax.ShapeDtypeStruct((n // seq, mw, seq), f32)
        outs += [(t_spec, t_shape), (t_spec, t_shape), row(mw, bf16),
                 (pl.BlockSpec((1, mw, tm), lambda i: (i, 0, 0)), jax.ShapeDtypeStruct((n // tm, mw, tm), bf16)),
                 (pl.BlockSpec((1, 1, mw), lambda i: (i, 0, 0)), jax.ShapeDtypeStruct((n // tm, 1, mw), f32))]
    return pl.pallas_call(
        functools.partial(_in_proj_kernel, splits, seq is not None),
        grid=(n // tm,),
        in_specs=[
            pl.BlockSpec((tm, d), lambda i: (i, 0)),
            _const_spec((1, d)),
            _const_spec(w_in.shape),
        ],
        out_specs=[o[0] for o in outs],
        out_shape=[o[1] for o in outs],
        compiler_params=_params("parallel"),
        name="in_proj",
    )(h, g, w_in)


def _top_blocks(gate, topk):
    blk = lax.broadcasted_iota(jnp.int32, gate.shape, 0).astype(f32)
    sel = jnp.zeros(gate.shape, f32)
    big = float(gate.shape[0])
    for _ in range(topk):
        m = jnp.max(gate, axis=0, keepdims=True)
        cand = (gate == m) & (gate > NEG_INF)
        idx = jnp.min(jnp.where(cand, blk, big), axis=0, keepdims=True)
        hit = blk == idx
        sel = jnp.where(hit, 1.0, sel)
        gate = jnp.where(hit, NEG_INF, gate)
    return sel


def _moba_prompt_kernel(n_heads, q_ref, kb_ref, vt_ref, km_ref, o_ref, off_sc, s_sc):
    pair = pl.program_id(1)
    qi = pl.program_id(2)
    tq = q_ref.shape[0]
    nb = km_ref.shape[0]
    lane = lax.broadcasted_iota(jnp.int32, (1, LANES), 1)
    key_i = lax.broadcasted_iota(jnp.int32, (MOBA_BLOCK, tq), 0)
    qry_i = lax.broadcasted_iota(jnp.int32, (MOBA_BLOCK, tq), 1)
    rel = (qry_i - key_i).astype(f32)
    causal = key_i <= qry_i
    blk_row = lax.broadcasted_iota(jnp.int32, (nb, tq), 0)

    q = q_ref[...]
    km = km_ref[...]
    own = pl.ds(pl.multiple_of(qi * MOBA_BLOCK, MOBA_BLOCK), MOBA_BLOCK)
    k_own = kb_ref[own, :]
    ones_rows = jnp.ones((SUM_ROWS, MOBA_BLOCK), bf16)
    with_sum = lambda vt: jnp.concatenate([vt, ones_rows], axis=0)

    consts, carry = [], []
    for j in range(HEADS_PER_TILE):
        head_mask = (lane // HEAD_DIM == j).astype(f32)
        head = jnp.full((1, 1), pair * HEADS_PER_TILE + j + 1, jnp.int32).astype(f32)
        slope2 = jnp.exp2(head * (-8.0 / n_heads)) * LOG2E
        gate = _dot_hi(km * head_mask, q, _NT)
        gate = jnp.where(blk_row < qi, gate, NEG_INF)
        off_sc[j] = jnp.where(_top_blocks(gate, MOBA_TOPK) > 0.5, 0.0, POS_INF)
        qs = (q * head_mask * (HEAD_DIM ** -0.5 * LOG2E)).astype(bf16)
        alibi = slope2 * rel

        s = _dg(k_own, qs, _NT) - alibi
        s = jnp.where(causal, s, NEG_INF)
        m = jnp.max(s, axis=0, keepdims=True)
        p = jnp.exp2(s - m).astype(bf16)
        acc = _dg(with_sum(vt_ref[qi]), p, _NN)
        consts.append((qs, alibi, slope2))
        carry += [m, acc]

    def score(n, slot):
        nc = jnp.minimum(n, nb - 1)
        kb = kb_ref[pl.ds(pl.multiple_of(nc * MOBA_BLOCK, MOBA_BLOCK), MOBA_BLOCK), :]
        raw = []
        for j in range(HEADS_PER_TILE):
            s = _dg(kb, consts[j][0], _NT) - consts[j][1]
            s_sc[slot, j] = s
            raw.append(jnp.max(s, axis=0, keepdims=True))
        return raw

    def attend(n, slot, raw, stats):
        nc = jnp.minimum(n, nb - 1)
        vt = with_sum(vt_ref[nc])
        out = []
        for j in range(HEADS_PER_TILE):
            slope2 = consts[j][2]
            m, acc = stats[2 * j:2 * j + 2]
            off = jnp.where(n < qi, off_sc[j, pl.ds(nc, 1), :], POS_INF)
            far = ((qi - n) * MOBA_BLOCK).astype(f32) * slope2
            m_new = jnp.maximum(m, raw[j] - far - off)
            alpha = jnp.exp2(m - m_new)
            p = jnp.exp2(s_sc[slot, j] - (m_new + far + off)).astype(bf16)
            out += [m_new, acc * alpha + _dg(vt, p, _NN)]
        return out

    def body(i, carry):
        stats, raw = list(carry[:-HEADS_PER_TILE]), list(carry[-HEADS_PER_TILE:])
        for u in range(PIPE_UNROLL):
            n = PIPE_UNROLL * i + u
            raw_next = score(n + 1, (u + 1) % 2)
            stats = attend(n, u % 2, raw, stats)
            raw = raw_next
        return tuple(stats + raw)

    trips = (qi + PIPE_UNROLL - 1) // PIPE_UNROLL
    carry = lax.fori_loop(0, trips, body, tuple(carry + score(0, 0)))
    accs = [carry[2 * j + 1] for j in range(HEADS_PER_TILE)]
    out_t = jnp.concatenate(
        [accs[j][j * HEAD_DIM:(j + 1) * HEAD_DIM] / accs[j][LANES:LANES + 1] for j in range(HEADS_PER_TILE)],
        axis=0)
    o_ref[...] = out_t.T


def _moba_prompt(q, k_bf, v_t, kmean, batch, seq, n_heads):
    n, w = q.shape
    n_pairs = w // LANES
    nb = seq // MOBA_BLOCK
    tq = MOBA_BLOCK
    return pl.pallas_call(
        functools.partial(_moba_prompt_kernel, n_heads),
        grid=(batch, n_pairs, nb),
        in_specs=[
            pl.BlockSpec((tq, LANES), lambda b, p, i: (b * nb + i, p)),
            pl.BlockSpec((seq, LANES), lambda b, p, i: (b, p)),
            pl.BlockSpec((nb, LANES, MOBA_BLOCK), lambda b, p, i: (b, p, 0)),
            pl.BlockSpec((nb, LANES), lambda b, p, i: (b, p)),
        ],
        out_specs=pl.BlockSpec((tq, LANES), lambda b, p, i: (b * nb + i, p)),
        out_shape=jax.ShapeDtypeStruct((n, w), f32),
        scratch_shapes=[pltpu.VMEM((HEADS_PER_TILE, nb, tq), f32),
                        pltpu.VMEM((2, HEADS_PER_TILE, MOBA_BLOCK, tq), f32)],
        compiler_params=_params("parallel", "parallel", "arbitrary"),
        name="moba_prompt",
    )(q, k_bf, v_t, kmean)


def _per_head(a, b, dims):
    a, b = a.astype(bf16), b.astype(bf16)
    return jnp.stack([_dg(a[h], b[h], dims) for h in range(a.shape[0])], axis=0)


def _moba_sample_kernel(t_new, n_blocks, pt_ref, q_ref, qt_ref, kn_ref, vn_ref, *refs):
    del pt_ref
    k_refs, v_refs, o_ref = refs[:2 * n_blocks], refs[2 * n_blocks:4 * n_blocks], refs[4 * n_blocks]
    n_heads, t_pad = q_ref.shape[1], q_ref.shape[2]
    page = k_refs[0].shape[3]
    q = q_ref[0]
    qt = qt_ref[0]
    head = lax.broadcasted_iota(jnp.int32, (n_heads, 1, 1), 0)
    slope = jnp.exp2((head + 1).astype(f32) * (-8.0 / n_heads))
    t_q = lax.broadcasted_iota(jnp.int32, (1, t_pad, 1), 1)
    qs = (q * (HEAD_DIM ** -0.5)).astype(bf16)
    tok = lax.broadcasted_iota(jnp.int32, (1, 1, page), 2)
    past_len = n_blocks * MOBA_BLOCK

    def scores(kt, first_pos):
        dist = (past_len - first_pos + t_q - tok).astype(f32)
        return _per_head(qs, kt, _NN) - slope * dist

    blocks, pages = range(n_blocks), range(2 * n_blocks)
    kt = [k_refs[i][0] for i in pages]
    s = [scores(kt[i], i * page) for i in pages]
    row_max = [jnp.max(s[i], axis=-1, keepdims=True) for i in pages]
    m_blk = [jnp.maximum(row_max[2 * n], row_max[2 * n + 1]) for n in blocks]
    p = [jnp.exp(s[i] - m_blk[i // 2]) for i in pages]
    row_sum = [jnp.sum(p[i], axis=-1, keepdims=True) for i in pages]
    l_blk = [row_sum[2 * n] + row_sum[2 * n + 1] for n in blocks]
    pv = [_per_head(p[i], v_refs[i][0], _NT) for i in pages]
    acc_blk = [pv[2 * n] + pv[2 * n + 1] for n in blocks]
    kmean = [jnp.sum(kt[2 * n] + kt[2 * n + 1], axis=-1, keepdims=True) * (1.0 / MOBA_BLOCK)
             for n in blocks]
    gates = [jnp.sum(qt * kmean[n], axis=1, keepdims=True) for n in blocks]

    t_k = lax.broadcasted_iota(jnp.int32, (1, t_pad, t_pad), 2)
    s_own = _per_head(qs, kn_ref[0], _NT) - slope * (t_q - t_k).astype(f32)
    s_own = jnp.where((t_k <= t_q) & (t_k < t_new), s_own, NEG_INF)
    m_tot = jnp.max(s_own, axis=-1, keepdims=True)
    eye = (lax.broadcasted_iota(jnp.int32, (1, t_pad, t_pad), 1) == t_k).astype(f32)
    chosen = []
    for n in range(n_blocks):
        rank = jnp.zeros_like(gates[n])
        for i in range(n_blocks):
            if i == n:
                continue
            ahead = (gates[i] > gates[n]) | ((gates[i] == gates[n]) & (i < n))
            rank = rank + ahead.astype(f32)
        picked = (rank < MOBA_TOPK).astype(f32)
        chosen.append(jnp.sum(eye * picked, axis=-1, keepdims=True) > 0.5)
    for n in range(n_blocks):
        m_tot = jnp.maximum(m_tot, jnp.where(chosen[n], m_blk[n], NEG_INF))
    p_own = jnp.exp(s_own - m_tot)
    l_tot = jnp.sum(p_own, axis=-1, keepdims=True)
    acc_tot = _per_head(p_own, vn_ref[0], _NN)
    for n in range(n_blocks):
        wgt = jnp.where(chosen[n], jnp.exp(m_blk[n] - m_tot), 0.0)
        l_tot = l_tot + wgt * l_blk[n]
        acc_tot = acc_tot + wgt * acc_blk[n]
    o_ref[0] = acc_tot / l_tot


def _moba_sample(q, k_new, v_new, cache_k, cache_v, page_table):
    batch, t_new, n_heads, dim = q.shape
    n_pool, page = cache_k.shape[:2]
    n_pages = page_table.shape[1]
    t_pad = 8
    assert dim == HEAD_DIM and MOBA_BLOCK == 2 * page and n_pages % 2 == 0 and t_new <= t_pad
    n_blocks = n_pages // 2
    assert n_blocks >= MOBA_TOPK
    cache_kt = jnp.transpose(cache_k, (0, 2, 3, 1))
    cache_vt = jnp.transpose(cache_v, (0, 2, 3, 1))
    by_head = lambda a: jnp.pad(jnp.transpose(a, (0, 2, 1, 3)), ((0, 0), (0, 0), (0, t_pad - t_new), (0, 0)))
    qh, kh, vh = by_head(q), by_head(k_new), by_head(v_new)
    qt = jnp.transpose(qh, (0, 1, 3, 2))

    def page_spec(i):
        return pl.BlockSpec((1, n_heads, dim, page), lambda b, pt: (pt[b, i], 0, 0, 0))

    pages = [page_spec(i) for i in range(n_pages)]
    row_spec = pl.BlockSpec((1, n_heads, t_pad, dim), lambda b, pt: (b, 0, 0, 0))
    out = pl.pallas_call(
        functools.partial(_moba_sample_kernel, t_new, n_blocks),
        grid_spec=pltpu.PrefetchScalarGridSpec(
            num_scalar_prefetch=1,
            grid=(batch,),
            in_specs=[row_spec, pl.BlockSpec((1, n_heads, dim, t_pad), lambda b, pt: (b, 0, 0, 0)),
                      row_spec, row_spec] + pages + pages,
            out_specs=row_spec,
        ),
        out_shape=jax.ShapeDtypeStruct((batch, n_heads, t_pad, dim), f32),
        compiler_params=_params("parallel"),
        name="moba_sample",
    )(page_table, qh, qt, kh, vh, *([cache_kt] * n_pages), *([cache_vt] * n_pages))
    return jnp.transpose(out[:, :, :t_new], (0, 2, 1, 3))


def _rwkv_pre_kernel(t_valid, t_pad, widths, p_ref, prev_ref, mu_ref, w0_ref, w2_ref, a0_ref, a2_ref,
                     g2_ref, kk_ref, ka_ref, ones_ref,
                     r_out, k_out, v_out, kk_out, b_out, lw_out, g_out, carry):
    ti = pl.program_id(1)
    bt, tt, width = p_ref.shape
    rw, dl, al, gl = widths

    @pl.when(ti == 0)
    def _():
        carry[...] = prev_ref[...]

    p = p_ref[...]
    t_idx = lax.broadcasted_iota(jnp.int32, (bt, tt, 1), 1)
    shifted = jnp.where(t_idx == 0, carry[...], pltpu.roll(p, 1, axis=1))
    carry[...] = p[:, tt - 1:tt, :]
    m = (p + (shifted - p) * mu_ref[...]).reshape(bt * tt, width)

    r = m[:, 0:rw]
    k = m[:, rw:2 * rw]
    v = m[:, 2 * rw:3 * rw]
    lora = m[:, 3 * rw:3 * rw + dl + al]
    gd = m[:, 3 * rw + dl + al:3 * rw + dl + al + gl]

    z = -(w0_ref[...] + _dot_hi(jnp.tanh(lora), w2_ref[...]))
    softplus = jnp.maximum(z, 0.0) + jnp.log(1.0 + jnp.exp(-jnp.abs(z)))
    w = -softplus - 0.5
    log_decay = -jnp.exp(w)
    a = jax.nn.sigmoid(a0_ref[...] + _dot_hi(lora, a2_ref[...]))
    g = _dot_hi(jax.nn.sigmoid(gd), g2_ref[...])
    kk = k * kk_ref[...]
    norm = jnp.sqrt(_head_sums(kk * kk, ones_ref[...]))
    kk = kk / jnp.maximum(norm, 1e-12)
    kh = k * (1.0 + (a - 1.0) * ka_ref[...])

    outs = [r, kh, v, kk, kk * a, log_decay]
    if t_valid < t_pad:
        live = (lax.broadcasted_iota(jnp.int32, (bt, tt, 1), 1) + ti * tt < t_valid)
        live = live.astype(f32).reshape(bt * tt, 1)
        outs = [x * live for x in outs]
    shape3 = (bt, tt, rw)
    for o_ref, x in zip((r_out, k_out, v_out, kk_out, b_out, lw_out, g_out), outs + [g]):
        o_ref[...] = x.reshape(shape3)


def _head_ones():
    idx = np.arange(MXU_DIM) // HEAD_DIM
    return jnp.asarray(idx[:, None] == idx[None, :], dtype=bf16)


def _rwkv_pre(p, prev, t_valid, weights, bt, tt):
    batch, t_pad, width = p.shape
    mu, w0, w2, a0, a2, g2, k_k, k_a = weights
    rw = w0.shape[-1]
    widths = (rw, w2.shape[0], a2.shape[0], g2.shape[0])
    assert 3 * rw + sum(widths[1:]) == width
    assert widths[1] + widths[2] == LANES and widths[3] % LANES == 0
    w2p = jnp.pad(w2, ((0, widths[2]), (0, 0)))
    a2p = jnp.pad(a2, ((widths[1], 0), (0, 0)))
    assert rw % MXU_DIM == 0
    consts = (mu, w0, w2p, a0, a2p, g2, k_k, k_a, _head_ones())
    out_spec = pl.BlockSpec((bt, tt, rw), lambda b, t: (b, t, 0))
    return pl.pallas_call(
        functools.partial(_rwkv_pre_kernel, t_valid, t_pad, widths),
        grid=(batch // bt, t_pad // tt),
        in_specs=[pl.BlockSpec((bt, tt, width), lambda b, t: (b, t, 0)),
                  pl.BlockSpec((bt, 1, width), lambda b, t: (b, 0, 0))]
                 + [_const_spec(c.shape) for c in consts],
        out_specs=[out_spec] * 7,
        out_shape=[jax.ShapeDtypeStruct((batch, t_pad, rw), f32)] * 7,
        scratch_shapes=[pltpu.VMEM((bt, 1, width), f32)],
        compiler_params=_params("parallel", "arbitrary"),
        name="rwkv_pre",
    )(p, prev, *consts)


def _rwkv_scan_kernel(r_ref, k_ref, v_ref, kk_ref, b_ref, lw_ref, s0_ref, o_ref, s_out, state):
    ci = pl.program_id(1)
    bt, c, width = r_ref.shape
    n_pairs = width // LANES
    c2 = HEADS_PER_TILE * c
    n_double = int(math.log2(c))
    assert 1 << n_double == c and HEADS_PER_TILE == 2

    @pl.when(ci == 0)
    def _():
        state[...] = s0_ref[...]

    ri = lax.broadcasted_iota(jnp.int32, (c, c), 0)
    cj = lax.broadcasted_iota(jnp.int32, (c, c), 1)
    tri_bf = (cj <= ri).astype(bf16)
    r2 = lax.broadcasted_iota(jnp.int32, (c2, c2), 0)
    q2 = lax.broadcasted_iota(jnp.int32, (c2, c2), 1)
    same_blk = (r2 >= c) == (q2 >= c)
    strict2 = (same_blk & (q2 < r2)).astype(f32)
    incl2 = (same_blk & (q2 <= r2)).astype(f32)
    eye2 = (q2 == r2).astype(f32)
    lane = lax.broadcasted_iota(jnp.int32, (1, LANES), 1)
    hm = [(lane // HEAD_DIM == j).astype(f32) for j in range(HEADS_PER_TILE)]
    bi_ = lax.broadcasted_iota(jnp.int32, (LANES, LANES), 0) // HEAD_DIM
    bj_ = lax.broadcasted_iota(jnp.int32, (LANES, LANES), 1) // HEAD_DIM
    block_diag = (bi_ == bj_).astype(f32)

    def stack(x):
        return jnp.concatenate([x * hm[0], x * hm[1]], axis=0)

    def twice(x):
        return jnp.concatenate([x, x], axis=0)

    def unstack(x2):
        return x2[:c] * hm[0] + x2[c:] * hm[1]

    probs = [(bi, pl.ds(pair * LANES, LANES), pair) for bi in range(bt) for pair in range(n_pairs)]
    ld = lambda ref: [ref[bi, :, sl] for bi, sl, _ in probs]
    r, k, v, kk, b, lw = ld(r_ref), ld(k_ref), ld(v_ref), ld(kk_ref), ld(b_ref), ld(lw_ref)
    nprob = len(probs)
    each = range(nprob)

    cum = [_dot_exact_lhs(tri_bf, lw[i]) for i in each]
    cum_end = [cum[i][c - 1:c, :] for i in each]
    e_neg = [jnp.exp(-cum[i]) for i in each]
    e_end = [jnp.exp(cum_end[i] - cum[i]) for i in each]
    a_mat = [kk[i] * jnp.exp(cum[i] - lw[i]) for i in each]
    p_mat = [r[i] * jnp.exp(cum[i]) for i in each]
    xa = [stack(a_mat[i]) for i in each]
    xap = [jnp.concatenate([xa[i], stack(p_mat[i])], axis=0) for i in each]
    yb = [stack(b[i] * e_neg[i]) for i in each]
    yk = [stack(k[i] * e_neg[i]) for i in each]
    v2 = [twice(v[i]) for i in each]

    gb = [_dot_lo(xap[i], yb[i], _NT) for i in each]
    gk = [_dot_lo(xap[i], yk[i], _NT) for i in each]
    l_mat = [gb[i][:c2] * strict2 for i in each]
    m_mat = [gk[i][:c2] * strict2 for i in each]
    lr_mat = [gb[i][c2:] * incl2 for i in each]
    mr_mat = [gk[i][c2:] * incl2 for i in each]

    npow = [-l_mat[i] for i in each]
    t_inv = [eye2 + npow[i] for i in each]
    for _ in range(n_double - 1):
        npow = [_dot_lo(npow[i], npow[i]) for i in each]
        t_inv = [t_inv[i] + _dot_lo(npow[i], t_inv[i]) for i in each]
    mv = [_dot_lo(m_mat[i], v2[i]) for i in each]
    tz = [_dot_lo(t_inv[i], jnp.concatenate([xa[i], mv[i]], axis=1)) for i in each]
    ta = [tz[i][:c, :LANES] + tz[i][c:, :LANES] for i in each]
    tmv = [unstack(tz[i][:, LANES:]) for i in each]

    s_prev = [state[bi, pair] for bi, _, pair in probs]
    xs = [_dot_hi(jnp.concatenate([ta[i], p_mat[i]], axis=0), s_prev[i], _NT) for i in each]
    u = [-(xs[i][:c] + tmv[i]) for i in each]
    corr = [_dot_lo(lr_mat[i], twice(u[i])) + _dot_lo(mr_mat[i], v2[i]) for i in each]
    upd = [_dot_hi(jnp.concatenate([u[i], v[i]], axis=0),
                   jnp.concatenate([b[i] * e_end[i], k[i] * e_end[i]], axis=0), _TN) for i in each]
    for i, (bi, sl, pair) in enumerate(probs):
        state[bi, pair] = s_prev[i] * jnp.exp(cum_end[i]) + upd[i] * block_diag
        o_ref[bi, :, sl] = xs[i][c:] + unstack(corr[i])

    @pl.when(ci == pl.num_programs(1) - 1)
    def _():
        s_out[...] = state[...]


def _rwkv_scan(r, k, v, kk, b, lw, s0, bt, chunk):
    batch, t_pad, width = r.shape
    n_pairs = width // LANES
    seq_spec = pl.BlockSpec((bt, chunk, width), lambda bb, c: (bb, c, 0))
    st_spec = pl.BlockSpec((bt, n_pairs, LANES, LANES), lambda bb, c: (bb, 0, 0, 0))
    return pl.pallas_call(
        _rwkv_scan_kernel,
        grid=(batch // bt, t_pad // chunk),
        in_specs=[seq_spec] * 6 + [st_spec],
        out_specs=[seq_spec, st_spec],
        out_shape=[jax.ShapeDtypeStruct((batch, t_pad, width), f32),
                   jax.ShapeDtypeStruct((batch, n_pairs, LANES, LANES), f32)],
        scratch_shapes=[pltpu.VMEM((bt, n_pairs, LANES, LANES), f32)],
        compiler_params=_params("parallel", "arbitrary"),
        name="rwkv_scan",
    )(r, k, v, kk, b, lw, s0)


def _pair_state(s):
    bsz, h, n, _ = s.shape
    s = s.reshape(bsz, h // HEADS_PER_TILE, HEADS_PER_TILE, n, n)
    eye = jnp.eye(HEADS_PER_TILE, dtype=s.dtype)
    out = jnp.einsum("bpjvk,ji->bpjvik", s, eye)
    return out.reshape(bsz, h // HEADS_PER_TILE, LANES, LANES)


def _unpair_state(s, heads):
    bsz = s.shape[0]
    s = s.reshape(bsz, heads // HEADS_PER_TILE, HEADS_PER_TILE, HEAD_DIM, HEADS_PER_TILE, HEAD_DIM)
    diag = jnp.stack([s[:, :, j, :, j, :] for j in range(HEADS_PER_TILE)], axis=2)
    return diag.reshape(bsz, heads, HEAD_DIM, HEAD_DIM)


def _merge_kernel(h_ref, oa_ref, o_ref, r_ref, k_ref, v_ref, g_ref, ga_ref, gb_ref,
                  rk_ref, lnw_ref, lnb_ref, ones_ref, wba_ref, wbb_ref, wout_ref, gpost_ref,
                  out_ref):
    ones = ones_ref[...]
    inv = 1.0 / HEAD_DIM
    o = o_ref[...]
    mean = _head_sums(o, ones) * inv
    d = o - mean
    var = _head_sums(d * d, ones) * inv
    on = d * lax.rsqrt(var + GN_EPS) * lnw_ref[...] + lnb_ref[...]
    bonus = _head_sums(r_ref[...] * k_ref[...] * rk_ref[...], ones) * v_ref[...]
    ob = (on + bonus) * g_ref[...]
    merged = (ga_ref[...] * jnp.dot(oa_ref[...].astype(bf16), wba_ref[...], preferred_element_type=f32)
              + gb_ref[...] * jnp.dot(ob.astype(bf16), wbb_ref[...], preferred_element_type=f32))
    y = jnp.dot(merged.astype(bf16), wout_ref[...], preferred_element_type=f32)
    out_ref[...] = h_ref[...] + _rms(y, gpost_ref[...])


def _merge(h, oa, o, r, k, v, g, ga, gb, consts, tm):
    n, d = h.shape
    w = oa.shape[1]
    row = lambda width: pl.BlockSpec((tm, width), lambda i: (i, 0))
    return pl.pallas_call(
        _merge_kernel,
        grid=(n // tm,),
        in_specs=[row(d)] + [row(w)] * 6 + [row(d)] * 2 + [_const_spec(c.shape) for c in consts],
        out_specs=row(d),
        out_shape=jax.ShapeDtypeStruct((n, d), f32),
        compiler_params=_params("parallel"),
        name="merge",
    )(h, oa, o, r, k, v, g, ga, gb, *consts)


def _row_tile(n, target):
    t = min(n, target)
    assert n % t == 0
    return t


def _layer(x, prev_shift, wkv0, past, w, n_heads):
    (g1a, g1b, f1g, f1u, f1d, gma, gmb, w_in, w_ba, w_bb, w_out,
     mu, w0, w2, a0, a2, g2, k_k, k_a, r_k, ln_w, ln_b,
     g2a, g2b, f2g, f2u, f2d) = w
    batch, t, d = x.shape
    n = batch * t
    mw = w_ba.shape[0]
    rw = w_bb.shape[0]
    rp = mu.shape[-1]
    widths = (mw, mw, mw, rp, d, d)
    row = lambda a: a.reshape(1, -1)

    x2 = x.reshape(n, d)
    h = _ffn(x2, row(g1a), row(g1b), f1g, f1u, f1d, _row_tile(n, 512))
    kv_shape = (batch, t, n_heads, HEAD_DIM)
    if past is None:
        assert t % MOBA_BLOCK == 0
        q, p_rw, ga, gb, k_t, v_t, k_bf, vt_bf, kmean = _in_proj(
            h, row(gma), w_in, widths, MOBA_BLOCK, seq=t)
        o_a = _moba_prompt(q, k_bf, vt_bf, kmean.reshape(n // MOBA_BLOCK, mw), batch, t, n_heads)
        k, v = (jnp.transpose(a.reshape(batch, n_heads, HEAD_DIM, t), (0, 3, 1, 2)) for a in (k_t, v_t))
    else:
        cache_k, cache_v, page_table = past
        q, p_rw, ga, gb, k, v = _in_proj(h, row(gma), w_in, widths, _row_tile(n, MOBA_BLOCK))
        k, v = k.reshape(kv_shape), v.reshape(kv_shape)
        o_a = _moba_sample(q.reshape(kv_shape), k, v, cache_k, cache_v, page_table).reshape(n, mw)

    if t % 64 == 0:
        chunk, t_pad, bt_pre, tt_pre, bt_scan = 64, t, 1, min(t, 512), min(batch, 2)
    else:
        chunk = 8
        t_pad = -(-t // chunk) * chunk
        bt_pre, tt_pre, bt_scan = min(batch, 16), t_pad, min(batch, 4)
    p3 = p_rw.reshape(batch, t, rp)
    p3p = jnp.pad(p3, ((0, 0), (0, t_pad - t), (0, 0))) if t_pad != t else p3
    pre_w = (row(mu), row(w0), w2, row(a0), a2, g2, row(k_k), row(k_a))
    r_, kh, v_, kk, b_, lw, g_ = _rwkv_pre(p3p, prev_shift.reshape(batch, 1, rp), t, pre_w, bt_pre, tt_pre)
    o_scan, s_fin = _rwkv_scan(r_, kh, v_, kk, b_, lw, _pair_state(wkv0.astype(f32)), bt_scan, chunk)
    wkv = _unpair_state(s_fin, rw // HEAD_DIM).astype(wkv0.dtype)
    unpad = lambda a: a[:, :t].reshape(n, rw)
    consts = (row(r_k), row(ln_w), row(ln_b), _head_ones(), w_ba, w_bb, w_out, row(gmb))
    h2 = _merge(h, o_a, unpad(o_scan), unpad(r_), unpad(kh), unpad(v_), unpad(g_), ga, gb, consts,
                _row_tile(n, 256))
    y = _ffn(h2, row(g2a), row(g2b), f2g, f2u, f2d, _row_tile(n, 512))
    return y.reshape(batch, t, d), k, v, wkv, p3[:, -1]


def kernel(x_prompt, x_sample, cache_k, cache_v, state_wkv, state_shift, page_table,
           g_ffn1_pre, g_ffn1_post, w_ffn1_gate, w_ffn1_up, w_ffn1_down,
           g_mix_pre, g_mix_post, w_in, w_branch_a, w_branch_b, w_out,
           rwkv_mu, rwkv_w0, rwkv_w2, rwkv_a0, rwkv_a2, rwkv_g2, rwkv_k_k, rwkv_k_a, rwkv_r_k,
           rwkv_ln_w, rwkv_ln_b,
           g_ffn2_pre, g_ffn2_post, w_ffn2_gate, w_ffn2_up, w_ffn2_down):
    depth = w_in.shape[0]
    bp = x_prompt.shape[0]
    n_heads = cache_k.shape[3]
    rp = rwkv_mu.shape[-1]
    rw = rwkv_w0.shape[-1]
    yp, ys = x_prompt, x_sample
    outs = [[] for _ in range(8)]
    for i in range(depth):
        cast = lambda a: a[i].astype(bf16)
        w = (g_ffn1_pre[i], g_ffn1_post[i], cast(w_ffn1_gate), cast(w_ffn1_up), cast(w_ffn1_down),
             g_mix_pre[i], g_mix_post[i], cast(w_in), cast(w_branch_a), cast(w_branch_b), cast(w_out),
             rwkv_mu[i], rwkv_w0[i], rwkv_w2[i], rwkv_a0[i], rwkv_a2[i], rwkv_g2[i], rwkv_k_k[i],
             rwkv_k_a[i], rwkv_r_k[i], rwkv_ln_w[i], rwkv_ln_b[i],
             g_ffn2_pre[i], g_ffn2_post[i], cast(w_ffn2_gate), cast(w_ffn2_up), cast(w_ffn2_down))
        yp, kp, vp, wkvp, shp = _layer(
            yp, jnp.zeros((bp, rp), x_prompt.dtype),
            jnp.zeros((bp, rw // HEAD_DIM, HEAD_DIM, HEAD_DIM), state_wkv.dtype), None, w, n_heads)
        ys, ksn, vsn, wkvs, shs = _layer(
            ys, state_shift[i], state_wkv[i], (cache_k[i], cache_v[i], page_table), w, n_heads)
        for lst, val in zip(outs, (kp, vp, ksn, vsn, wkvp, shp, wkvs, shs)):
            lst.append(val)
    return (yp, ys) + tuple(jnp.stack(lst) for lst in outs)
```

```python
import functools
import math

import jax
import jax.numpy as jnp
import numpy as np
from jax import lax
from jax.experimental import pallas as pl
from jax.experimental.pallas import tpu as pltpu

HEAD_DIM = 64
LANES = 128
HEADS_PER_TILE = LANES // HEAD_DIM
MXU_DIM = 256
MOBA_BLOCK = 256
MOBA_TOPK = 3
SUM_ROWS = 16
PIPE_UNROLL = 4
RWKV_CHUNK = 64
MOBA_GROUP_HEADS = 4
RMS_EPS = 1e-6
GN_EPS = 64e-5
VMEM_LIMIT = 56 * 1024 * 1024
NEG_INF = float("-inf")
POS_INF = float("inf")
LOG2E = 1.4426950408889634

f32 = jnp.float32
bf16 = jnp.bfloat16


def _params(*sem):
    return pltpu.CompilerParams(dimension_semantics=sem, vmem_limit_bytes=VMEM_LIMIT)


def _const_spec(shape):
    nd = len(shape)
    return pl.BlockSpec(shape, lambda *_: (0,) * nd, pipeline_mode=pl.Buffered(1))


def _rms(x, g):
    return x * lax.rsqrt(jnp.mean(x * x, axis=-1, keepdims=True) + RMS_EPS) * g


def _split3(x):
    hi = x.astype(bf16)
    r1 = x - hi.astype(f32)
    mid = r1.astype(bf16)
    lo = (r1 - mid.astype(f32)).astype(bf16)
    return hi, mid, lo


_NN = (((1,), (0,)), ((), ()))
_NT = (((1,), (1,)), ((), ()))
_TN = (((0,), (0,)), ((), ()))


def _dg(a, b, dims):
    return lax.dot_general(a, b, dims, preferred_element_type=f32)


def _dot_lo(a, b, dims=_NN):
    return _dg(a.astype(bf16), b.astype(bf16), dims)


def _dot_hi(a, b, dims=_NN):
    ah, am, _ = _split3(a)
    bh, bm, _ = _split3(b)
    return _dg(ah, bh, dims) + (_dg(ah, bm, dims) + _dg(am, bh, dims))


def _dot_exact_lhs(a_bf, b, dims=_NN):
    bh, bm, bl = _split3(b)
    return _dg(a_bf, bh, dims) + (_dg(a_bf, bm, dims) + _dg(a_bf, bl, dims))


def _dot_exact_rhs(a, b_bf, dims=_NN):
    ah = a.astype(bf16)
    am = (a - ah.astype(f32)).astype(bf16)
    return _dg(ah, b_bf, dims) + _dg(am, b_bf, dims)


def _head_sums(x, ones_bf):
    g = ones_bf.shape[0]
    return jnp.concatenate([_dot_exact_rhs(x[:, c:c + g], ones_bf) for c in range(0, x.shape[1], g)],
                           axis=1)


def _ffn_kernel(x_ref, gpre_ref, gpost_ref, wg_ref, wu_ref, wd_ref, o_ref):
    x = x_ref[...]
    h = _rms(x, gpre_ref[...]).astype(bf16)
    gate = jnp.dot(h, wg_ref[...], preferred_element_type=f32)
    up = jnp.dot(h, wu_ref[...], preferred_element_type=f32)
    act = (gate * jax.nn.sigmoid(gate) * up).astype(bf16)
    y = jnp.dot(act, wd_ref[...], preferred_element_type=f32)
    o_ref[...] = x + 0.5 * _rms(y, gpost_ref[...])


def _ffn(x, g_pre, g_post, wg, wu, wd, tm):
    n, d = x.shape
    dff = wg.shape[1]
    return pl.pallas_call(
        _ffn_kernel,
        grid=(n // tm,),
        in_specs=[
            pl.BlockSpec((tm, d), lambda i: (i, 0)),
            _const_spec((1, d)), _const_spec((1, d)),
            _const_spec((d, dff)), _const_spec((d, dff)), _const_spec((dff, d)),
        ],
        out_specs=pl.BlockSpec((tm, d), lambda i: (i, 0)),
        out_shape=jax.ShapeDtypeStruct((n, d), f32),
        compiler_params=_params("parallel"),
        name="ffn",
    )(x, g_pre, g_post, wg, wu, wd)


def _in_proj_kernel(splits, blocked, h_ref, g_ref, w_ref, q_ref, p_ref, ga_ref, gb_ref, *kv_refs):
    u = _rms(h_ref[...], g_ref[...]).astype(bf16)
    proj = lambda idx: _dg(u, w_ref[:, splits[idx]:splits[idx + 1]], _NN)
    q_ref[...] = proj(0)
    p_ref[...] = proj(3)
    ga_ref[...] = jax.nn.sigmoid(proj(4))
    gb_ref[...] = jax.nn.sigmoid(proj(5))
    k, v = proj(1), proj(2)
    if blocked:
        kt_ref, vt_ref, kb_ref, vtb_ref, km_ref = kv_refs
        v_t = v.T
        kt_ref[0] = k.T
        vt_ref[0] = v_t
        kb_ref[...] = k.astype(bf16)
        vtb_ref[0] = v_t.astype(bf16)
        km_ref[0] = jnp.mean(k, axis=0, keepdims=True)
    else:
        kv_refs[0][...] = k
        kv_refs[1][...] = v


def _in_proj(h, g, w_in, widths, tm, seq=None):
    n, d = h.shape
    splits = tuple(int(s) for s in np.concatenate([[0], np.cumsum(widths)]))
    assert splits[-1] == w_in.shape[1] and all(s % LANES == 0 for s in splits)
    mw = widths[1]
    row = lambda w, dt=f32: (pl.BlockSpec((tm, w), lambda i: (i, 0)), jax.ShapeDtypeStruct((n, w), dt))
    outs = [row(widths[0]), row(widths[3]), row(widths[4]), row(widths[5])]
    if seq is None:
        outs += [row(mw), row(mw)]
    else:
        assert tm == MOBA_BLOCK and seq % tm == 0
        nbs = seq // tm
        t_spec = pl.BlockSpec((1, mw, tm), lambda i: (i // nbs, 0, i % nbs))
        t_shape = jax.ShapeDtypeStruct((n // seq, mw, seq), f32)
        outs += [(t_spec, t_shape), (t_spec, t_shape), row(mw, bf16),
                 (pl.BlockSpec((1, mw, tm), lambda i: (i, 0, 0)), jax.ShapeDtypeStruct((n // tm, mw, tm), bf16)),
                 (pl.BlockSpec((1, 1, mw), lambda i: (i, 0, 0)), jax.ShapeDtypeStruct((n // tm, 1, mw), f32))]
    return pl.pallas_call(
        functools.partial(_in_proj_kernel, splits, seq is not None),
        grid=(n // tm,),
        in_specs=[
            pl.BlockSpec((tm, d), lambda i: (i, 0)),
            _const_spec((1, d)),
            _const_spec(w_in.shape),
        ],
        out_specs=[o[0] for o in outs],
        out_shape=[o[1] for o in outs],
        compiler_params=_params("parallel"),
        name="in_proj",
    )(h, g, w_in)


def _top_blocks(gate, topk):
    blk = lax.broadcasted_iota(jnp.int32, gate.shape, 0).astype(f32)
    sel = jnp.zeros(gate.shape, f32)
    big = float(gate.shape[0])
    for _ in range(topk):
        m = jnp.max(gate, axis=0, keepdims=True)
        cand = (gate == m) & (gate > NEG_INF)
        idx = jnp.min(jnp.where(cand, blk, big), axis=0, keepdims=True)
        hit = blk == idx
        sel = jnp.where(hit, 1.0, sel)
        gate = jnp.where(hit, NEG_INF, gate)
    return sel


def _moba_prompt_kernel(n_heads, q_ref, kb_ref, vt_ref, km_ref, o_ref, off_sc, s_sc):
    group = pl.program_id(1)
    qi = pl.program_id(2)
    tq = q_ref.shape[0]
    nb = km_ref.shape[0]
    heads = q_ref.shape[1] // HEAD_DIM
    tile = lambda j: pl.ds((j // HEADS_PER_TILE) * LANES, LANES)
    lane = lax.broadcasted_iota(jnp.int32, (1, LANES), 1)
    key_i = lax.broadcasted_iota(jnp.int32, (MOBA_BLOCK, tq), 0)
    qry_i = lax.broadcasted_iota(jnp.int32, (MOBA_BLOCK, tq), 1)
    rel = (qry_i - key_i).astype(f32)
    causal = key_i <= qry_i
    blk_row = lax.broadcasted_iota(jnp.int32, (nb, tq), 0)

    own = pl.ds(pl.multiple_of(qi * MOBA_BLOCK, MOBA_BLOCK), MOBA_BLOCK)
    ones_rows = jnp.ones((SUM_ROWS, MOBA_BLOCK), bf16)
    with_sum = lambda vt: jnp.concatenate([vt, ones_rows], axis=0)

    consts, carry = [], []
    for j in range(heads):
        q = q_ref[:, tile(j)]
        head_mask = (lane // HEAD_DIM == j % HEADS_PER_TILE).astype(f32)
        head = jnp.full((1, 1), group * heads + j + 1, jnp.int32).astype(f32)
        slope2 = jnp.exp2(head * (-8.0 / n_heads)) * LOG2E
        gate = _dot_hi(km_ref[:, tile(j)] * head_mask, q, _NT)
        gate = jnp.where(blk_row < qi, gate, NEG_INF)
        off_sc[j] = jnp.where(_top_blocks(gate, MOBA_TOPK) > 0.5, 0.0, POS_INF)
        qs = (q * head_mask * (HEAD_DIM ** -0.5 * LOG2E)).astype(bf16)
        alibi = slope2 * rel

        s = _dg(kb_ref[own, tile(j)], qs, _NT) - alibi
        s = jnp.where(causal, s, NEG_INF)
        m = jnp.max(s, axis=0, keepdims=True)
        p = jnp.exp2(s - m).astype(bf16)
        acc = _dg(with_sum(vt_ref[qi, tile(j), :]), p, _NN)
        consts.append((qs, alibi, slope2))
        carry += [m, acc]

    def score(n, slot):
        nc = jnp.minimum(n, nb - 1)
        rows = pl.ds(pl.multiple_of(nc * MOBA_BLOCK, MOBA_BLOCK), MOBA_BLOCK)
        raw = []
        for j in range(heads):
            s = _dg(kb_ref[rows, tile(j)], consts[j][0], _NT) - consts[j][1]
            s_sc[slot, j] = s
            raw.append(jnp.max(s, axis=0, keepdims=True))
        return raw

    def attend(n, slot, raw, stats):
        nc = jnp.minimum(n, nb - 1)
        out = []
        for j in range(heads):
            slope2 = consts[j][2]
            m, acc = stats[2 * j:2 * j + 2]
            off = jnp.where(n < qi, off_sc[j, pl.ds(nc, 1), :], POS_INF)
            far = ((qi - n) * MOBA_BLOCK).astype(f32) * slope2
            m_new = jnp.maximum(m, raw[j] - far - off)
            alpha = jnp.exp2(m - m_new)
            p = jnp.exp2(s_sc[slot, j] - (m_new + far + off)).astype(bf16)
            out += [m_new, acc * alpha + _dg(with_sum(vt_ref[nc, tile(j), :]), p, _NN)]
        return out

    def body(i, carry):
        stats, raw = list(carry[:-heads]), list(carry[-heads:])
        for u in range(PIPE_UNROLL):
            n = PIPE_UNROLL * i + u
            raw_next = score(n + 1, (u + 1) % 2)
            stats = attend(n, u % 2, raw, stats)
            raw = raw_next
        return tuple(stats + raw)

    trips = (qi + PIPE_UNROLL - 1) // PIPE_UNROLL
    carry = lax.fori_loop(0, trips, body, tuple(carry + score(0, 0)))
    accs = [carry[2 * j + 1] for j in range(heads)]
    in_tile = lambda j: slice((j % HEADS_PER_TILE) * HEAD_DIM, (j % HEADS_PER_TILE + 1) * HEAD_DIM)
    out_t = jnp.concatenate([accs[j][in_tile(j)] / accs[j][LANES:LANES + 1] for j in range(heads)], axis=0)
    o_ref[...] = out_t.T


def _moba_prompt(q, k_bf, v_t, kmean, batch, seq, n_heads):
    n, w = q.shape
    wg = MOBA_GROUP_HEADS * HEAD_DIM
    assert w % wg == 0 and wg % LANES == 0
    nb = seq // MOBA_BLOCK
    tq = MOBA_BLOCK
    return pl.pallas_call(
        functools.partial(_moba_prompt_kernel, n_heads),
        grid=(batch, w // wg, nb),
        in_specs=[
            pl.BlockSpec((tq, wg), lambda b, p, i: (b * nb + i, p)),
            pl.BlockSpec((seq, wg), lambda b, p, i: (b, p)),
            pl.BlockSpec((nb, wg, MOBA_BLOCK), lambda b, p, i: (b, p, 0)),
            pl.BlockSpec((nb, wg), lambda b, p, i: (b, p)),
        ],
        out_specs=pl.BlockSpec((tq, wg), lambda b, p, i: (b * nb + i, p)),
        out_shape=jax.ShapeDtypeStruct((n, w), f32),
        scratch_shapes=[pltpu.VMEM((MOBA_GROUP_HEADS, nb, tq), f32),
                        pltpu.VMEM((2, MOBA_GROUP_HEADS, MOBA_BLOCK, tq), f32)],
        compiler_params=_params("parallel", "parallel", "arbitrary"),
        name="moba_prompt",
    )(q, k_bf, v_t, kmean)


def _per_head(a, b, dims):
    a, b = a.astype(bf16), b.astype(bf16)
    return jnp.stack([_dg(a[h], b[h], dims) for h in range(a.shape[0])], axis=0)


def _moba_sample_kernel(t_new, n_blocks, pt_ref, q_ref, qt_ref, kn_ref, vn_ref, *refs):
    del pt_ref
    k_refs, v_refs, o_ref = refs[:2 * n_blocks], refs[2 * n_blocks:4 * n_blocks], refs[4 * n_blocks]
    n_heads, t_pad = q_ref.shape[1], q_ref.shape[2]
    page = k_refs[0].shape[3]
    q = q_ref[0]
    qt = qt_ref[0]
    head = lax.broadcasted_iota(jnp.int32, (n_heads, 1, 1), 0)
    slope = jnp.exp2((head + 1).astype(f32) * (-8.0 / n_heads))
    t_q = lax.broadcasted_iota(jnp.int32, (1, t_pad, 1), 1)
    qs = (q * (HEAD_DIM ** -0.5)).astype(bf16)
    tok = lax.broadcasted_iota(jnp.int32, (1, 1, page), 2)
    past_len = n_blocks * MOBA_BLOCK

    def scores(kt, first_pos):
        dist = (past_len - first_pos + t_q - tok).astype(f32)
        return _per_head(qs, kt, _NN) - slope * dist

    blocks, pages = range(n_blocks), range(2 * n_blocks)
    kt = [k_refs[i][0] for i in pages]
    s = [scores(kt[i], i * page) for i in pages]
    row_max = [jnp.max(s[i], axis=-1, keepdims=True) for i in pages]
    m_blk = [jnp.maximum(row_max[2 * n], row_max[2 * n + 1]) for n in blocks]
    p = [jnp.exp(s[i] - m_blk[i // 2]) for i in pages]
    row_sum = [jnp.sum(p[i], axis=-1, keepdims=True) for i in pages]
    l_blk = [row_sum[2 * n] + row_sum[2 * n + 1] for n in blocks]
    pv = [_per_head(p[i], v_refs[i][0], _NT) for i in pages]
    acc_blk = [pv[2 * n] + pv[2 * n + 1] for n in blocks]
    kmean = [jnp.sum(kt[2 * n] + kt[2 * n + 1], axis=-1, keepdims=True) * (1.0 / MOBA_BLOCK)
             for n in blocks]
    gates = [jnp.sum(qt * kmean[n], axis=1, keepdims=True) for n in blocks]

    t_k = lax.broadcasted_iota(jnp.int32, (1, t_pad, t_pad), 2)
    s_own = _per_head(qs, kn_ref[0], _NT) - slope * (t_q - t_k).astype(f32)
    s_own = jnp.where((t_k <= t_q) & (t_k < t_new), s_own, NEG_INF)
    m_tot = jnp.max(s_own, axis=-1, keepdims=True)
    eye = (lax.broadcasted_iota(jnp.int32, (1, t_pad, t_pad), 1) == t_k).astype(f32)
    chosen = []
    for n in range(n_blocks):
        rank = jnp.zeros_like(gates[n])
        for i in range(n_blocks):
            if i == n:
                continue
            ahead = (gates[i] > gates[n]) | ((gates[i] == gates[n]) & (i < n))
            rank = rank + ahead.astype(f32)
        picked = (rank < MOBA_TOPK).astype(f32)
        chosen.append(jnp.sum(eye * picked, axis=-1, keepdims=True) > 0.5)
    for n in range(n_blocks):
        m_tot = jnp.maximum(m_tot, jnp.where(chosen[n], m_blk[n], NEG_INF))
    p_own = jnp.exp(s_own - m_tot)
    l_tot = jnp.sum(p_own, axis=-1, keepdims=True)
    acc_tot = _per_head(p_own, vn_ref[0], _NN)
    for n in range(n_blocks):
        wgt = jnp.where(chosen[n], jnp.exp(m_blk[n] - m_tot), 0.0)
        l_tot = l_tot + wgt * l_blk[n]
        acc_tot = acc_tot + wgt * acc_blk[n]
    o_ref[0] = acc_tot / l_tot


def _moba_sample(q, k_new, v_new, cache_k, cache_v, page_table):
    batch, t_new, n_heads, dim = q.shape
    n_pool, page = cache_k.shape[:2]
    n_pages = page_table.shape[1]
    t_pad = 8
    assert dim == HEAD_DIM and MOBA_BLOCK == 2 * page and n_pages % 2 == 0 and t_new <= t_pad
    n_blocks = n_pages // 2
    assert n_blocks >= MOBA_TOPK
    cache_kt = jnp.transpose(cache_k, (0, 2, 3, 1))
    cache_vt = jnp.transpose(cache_v, (0, 2, 3, 1))
    by_head = lambda a: jnp.pad(jnp.transpose(a, (0, 2, 1, 3)), ((0, 0), (0, 0), (0, t_pad - t_new), (0, 0)))
    qh, kh, vh = by_head(q), by_head(k_new), by_head(v_new)
    qt = jnp.transpose(qh, (0, 1, 3, 2))

    def page_spec(i):
        return pl.BlockSpec((1, n_heads, dim, page), lambda b, pt: (pt[b, i], 0, 0, 0))

    pages = [page_spec(i) for i in range(n_pages)]
    row_spec = pl.BlockSpec((1, n_heads, t_pad, dim), lambda b, pt: (b, 0, 0, 0))
    out = pl.pallas_call(
        functools.partial(_moba_sample_kernel, t_new, n_blocks),
        grid_spec=pltpu.PrefetchScalarGridSpec(
            num_scalar_prefetch=1,
            grid=(batch,),
            in_specs=[row_spec, pl.BlockSpec((1, n_heads, dim, t_pad), lambda b, pt: (b, 0, 0, 0)),
                      row_spec, row_spec] + pages + pages,
            out_specs=row_spec,
        ),
        out_shape=jax.ShapeDtypeStruct((batch, n_heads, t_pad, dim), f32),
        compiler_params=_params("parallel"),
        name="moba_sample",
    )(page_table, qh, qt, kh, vh, *([cache_kt] * n_pages), *([cache_vt] * n_pages))
    return jnp.transpose(out[:, :, :t_new], (0, 2, 1, 3))


def _rwkv_pre_kernel(t_valid, t_pad, widths, p_ref, prev_ref, mu_ref, w0_ref, w2_ref, a0_ref, a2_ref,
                     g2_ref, kk_ref, ka_ref, ones_ref,
                     r_out, k_out, v_out, kk_out, b_out, lw_out, g_out, carry):
    ti = pl.program_id(1)
    bt, tt, width = p_ref.shape
    rw, dl, al, gl = widths

    @pl.when(ti == 0)
    def _():
        carry[...] = prev_ref[...]

    p = p_ref[...]
    t_idx = lax.broadcasted_iota(jnp.int32, (bt, tt, 1), 1)
    shifted = jnp.where(t_idx == 0, carry[...], pltpu.roll(p, 1, axis=1))
    carry[...] = p[:, tt - 1:tt, :]
    m = (p + (shifted - p) * mu_ref[...]).reshape(bt * tt, width)

    r = m[:, 0:rw]
    k = m[:, rw:2 * rw]
    v = m[:, 2 * rw:3 * rw]
    lora = m[:, 3 * rw:3 * rw + dl + al]
    gd = m[:, 3 * rw + dl + al:3 * rw + dl + al + gl]

    z = -(w0_ref[...] + _dot_hi(jnp.tanh(lora), w2_ref[...]))
    softplus = jnp.maximum(z, 0.0) + jnp.log(1.0 + jnp.exp(-jnp.abs(z)))
    w = -softplus - 0.5
    log_decay = -jnp.exp(w)
    a = jax.nn.sigmoid(a0_ref[...] + _dot_hi(lora, a2_ref[...]))
    g = _dot_hi(jax.nn.sigmoid(gd), g2_ref[...])
    kk = k * kk_ref[...]
    norm = jnp.sqrt(_head_sums(kk * kk, ones_ref[...]))
    kk = kk / jnp.maximum(norm, 1e-12)
    kh = k * (1.0 + (a - 1.0) * ka_ref[...])

    outs = [r, kh, v, kk, kk * a, log_decay]
    if t_valid < t_pad:
        live = (lax.broadcasted_iota(jnp.int32, (bt, tt, 1), 1) + ti * tt < t_valid)
        live = live.astype(f32).reshape(bt * tt, 1)
        outs = [x * live for x in outs]
    shape3 = (bt, tt, rw)
    for o_ref, x in zip((r_out, k_out, v_out, kk_out, b_out, lw_out, g_out), outs + [g]):
        o_ref[...] = x.reshape(shape3)


def _head_ones():
    idx = np.arange(MXU_DIM) // HEAD_DIM
    return jnp.asarray(idx[:, None] == idx[None, :], dtype=bf16)


def _rwkv_pre(p, prev, t_valid, weights, bt, tt):
    batch, t_pad, width = p.shape
    mu, w0, w2, a0, a2, g2, k_k, k_a = weights
    rw = w0.shape[-1]
    widths = (rw, w2.shape[0], a2.shape[0], g2.shape[0])
    assert 3 * rw + sum(widths[1:]) == width
    assert widths[1] + widths[2] == LANES and widths[3] % LANES == 0
    w2p = jnp.pad(w2, ((0, widths[2]), (0, 0)))
    a2p = jnp.pad(a2, ((widths[1], 0), (0, 0)))
    assert rw % MXU_DIM == 0
    consts = (mu, w0, w2p, a0, a2p, g2, k_k, k_a, _head_ones())
    out_spec = pl.BlockSpec((bt, tt, rw), lambda b, t: (b, t, 0))
    return pl.pallas_call(
        functools.partial(_rwkv_pre_kernel, t_valid, t_pad, widths),
        grid=(batch // bt, t_pad // tt),
        in_specs=[pl.BlockSpec((bt, tt, width), lambda b, t: (b, t, 0)),
                  pl.BlockSpec((bt, 1, width), lambda b, t: (b, 0, 0))]
                 + [_const_spec(c.shape) for c in consts],
        out_specs=[out_spec] * 7,
        out_shape=[jax.ShapeDtypeStruct((batch, t_pad, rw), f32)] * 7,
        scratch_shapes=[pltpu.VMEM((bt, 1, width), f32)],
        compiler_params=_params("parallel", "arbitrary"),
        name="rwkv_pre",
    )(p, prev, *consts)


def _rwkv_scan_kernel(r_ref, k_ref, v_ref, kk_ref, b_ref, lw_ref, s0_ref, o_ref, s_out, state):
    ci = pl.program_id(1)
    bt, c, width = r_ref.shape
    n_pairs = width // LANES
    c2 = HEADS_PER_TILE * c
    n_double = int(math.log2(c))
    assert 1 << n_double == c and HEADS_PER_TILE == 2

    @pl.when(ci == 0)
    def _():
        state[...] = s0_ref[...]

    ri = lax.broadcasted_iota(jnp.int32, (c, c), 0)
    cj = lax.broadcasted_iota(jnp.int32, (c, c), 1)
    tri_bf = (cj <= ri).astype(bf16)
    r2 = lax.broadcasted_iota(jnp.int32, (c2, c2), 0)
    q2 = lax.broadcasted_iota(jnp.int32, (c2, c2), 1)
    same_blk = (r2 >= c) == (q2 >= c)
    strict2 = (same_blk & (q2 < r2)).astype(f32)
    incl2 = (same_blk & (q2 <= r2)).astype(f32)
    eye2 = (q2 == r2).astype(f32)
    lane = lax.broadcasted_iota(jnp.int32, (1, LANES), 1)
    hm = [(lane // HEAD_DIM == j).astype(f32) for j in range(HEADS_PER_TILE)]
    bi_ = lax.broadcasted_iota(jnp.int32, (LANES, LANES), 0) // HEAD_DIM
    bj_ = lax.broadcasted_iota(jnp.int32, (LANES, LANES), 1) // HEAD_DIM
    block_diag = (bi_ == bj_).astype(f32)

    def stack(x):
        return jnp.concatenate([x * hm[0], x * hm[1]], axis=0)

    def twice(x):
        return jnp.concatenate([x, x], axis=0)

    def unstack(x2):
        return x2[:c] * hm[0] + x2[c:] * hm[1]

    probs = [(bi, pl.ds(pair * LANES, LANES), pair) for bi in range(bt) for pair in range(n_pairs)]
    ld = lambda ref: [ref[bi, :, sl] for bi, sl, _ in probs]
    r, k, v, kk, b, lw = ld(r_ref), ld(k_ref), ld(v_ref), ld(kk_ref), ld(b_ref), ld(lw_ref)
    nprob = len(probs)
    each = range(nprob)

    cum = [_dot_exact_lhs(tri_bf, lw[i]) for i in each]
    cum_end = [cum[i][c - 1:c, :] for i in each]
    e_neg = [jnp.exp(-cum[i]) for i in each]
    e_end = [jnp.exp(cum_end[i] - cum[i]) for i in each]
    a_mat = [kk[i] * jnp.exp(cum[i] - lw[i]) for i in each]
    p_mat = [r[i] * jnp.exp(cum[i]) for i in each]
    xa = [stack(a_mat[i]) for i in each]
    xap = [jnp.concatenate([xa[i], stack(p_mat[i])], axis=0) for i in each]
    yb = [stack(b[i] * e_neg[i]) for i in each]
    yk = [stack(k[i] * e_neg[i]) for i in each]
    v2 = [twice(v[i]) for i in each]

    gb = [_dot_lo(xap[i], yb[i], _NT) for i in each]
    gk = [_dot_lo(xap[i], yk[i], _NT) for i in each]
    l_mat = [gb[i][:c2] * strict2 for i in each]
    m_mat = [gk[i][:c2] * strict2 for i in each]
    lr_mat = [gb[i][c2:] * incl2 for i in each]
    mr_mat = [gk[i][c2:] * incl2 for i in each]

    npow = [-l_mat[i] for i in each]
    t_inv = [eye2 + npow[i] for i in each]
    for _ in range(n_double - 1):
        npow = [_dot_lo(npow[i], npow[i]) for i in each]
        t_inv = [t_inv[i] + _dot_lo(npow[i], t_inv[i]) for i in each]
    mv = [_dot_lo(m_mat[i], v2[i]) for i in each]
    tz = [_dot_lo(t_inv[i], jnp.concatenate([xa[i], mv[i]], axis=1)) for i in each]
    ta = [tz[i][:c, :LANES] + tz[i][c:, :LANES] for i in each]
    tmv = [unstack(tz[i][:, LANES:]) for i in each]

    s_prev = [state[bi, pair] for bi, _, pair in probs]
    xs = [_dot_lo(jnp.concatenate([ta[i], p_mat[i]], axis=0), s_prev[i], _NT) for i in each]
    u = [-(xs[i][:c] + tmv[i]) for i in each]
    corr = [_dot_lo(lr_mat[i], twice(u[i])) + _dot_lo(mr_mat[i], v2[i]) for i in each]
    upd = [_dot_lo(jnp.concatenate([u[i], v[i]], axis=0),
                   jnp.concatenate([b[i] * e_end[i], k[i] * e_end[i]], axis=0), _TN) for i in each]
    for i, (bi, sl, pair) in enumerate(probs):
        state[bi, pair] = s_prev[i] * jnp.exp(cum_end[i]) + upd[i] * block_diag
        o_ref[bi, :, sl] = xs[i][c:] + unstack(corr[i])

    @pl.when(ci == pl.num_programs(1) - 1)
    def _():
        s_out[...] = state[...]


def _rwkv_scan(r, k, v, kk, b, lw, s0, bt, chunk):
    batch, t_pad, width = r.shape
    n_pairs = width // LANES
    seq_spec = pl.BlockSpec((bt, chunk, width), lambda bb, c: (bb, c, 0))
    st_spec = pl.BlockSpec((bt, n_pairs, LANES, LANES), lambda bb, c: (bb, 0, 0, 0))
    return pl.pallas_call(
        _rwkv_scan_kernel,
        grid=(batch // bt, t_pad // chunk),
        in_specs=[seq_spec] * 6 + [st_spec],
        out_specs=[seq_spec, st_spec],
        out_shape=[jax.ShapeDtypeStruct((batch, t_pad, width), f32),
                   jax.ShapeDtypeStruct((batch, n_pairs, LANES, LANES), f32)],
        scratch_shapes=[pltpu.VMEM((bt, n_pairs, LANES, LANES), f32)],
        compiler_params=_params("parallel", "arbitrary"),
        name="rwkv_scan",
    )(r, k, v, kk, b, lw, s0)


def _pair_state(s):
    bsz, h, n, _ = s.shape
    s = s.reshape(bsz, h // HEADS_PER_TILE, HEADS_PER_TILE, n, n)
    eye = jnp.eye(HEADS_PER_TILE, dtype=s.dtype)
    out = jnp.einsum("bpjvk,ji->bpjvik", s, eye)
    return out.reshape(bsz, h // HEADS_PER_TILE, LANES, LANES)


def _unpair_state(s, heads):
    bsz = s.shape[0]
    s = s.reshape(bsz, heads // HEADS_PER_TILE, HEADS_PER_TILE, HEAD_DIM, HEADS_PER_TILE, HEAD_DIM)
    diag = jnp.stack([s[:, :, j, :, j, :] for j in range(HEADS_PER_TILE)], axis=2)
    return diag.reshape(bsz, heads, HEAD_DIM, HEAD_DIM)


STATE_ROWS = 8


def _rwkv_step_kernel(r_ref, k_ref, v_ref, kk_ref, b_ref, lw_ref, s_ref, o_ref, s_out):
    n_tok = r_ref.shape[0]
    n_val = s_ref.shape[1]
    decay = [jnp.exp(lw_ref[t, 0]) for t in range(n_tok)]

    def rows(c, _):
        base = pl.multiple_of(c * STATE_ROWS, STATE_ROWS)
        state = [s_ref[0, base + i] for i in range(STATE_ROWS)]
        for t in range(n_tok):
            kk, b, k, r = kk_ref[t, 0], b_ref[t, 0], k_ref[t, 0], r_ref[t, 0]
            v_rows = v_ref[t, 0, pl.ds(base, STATE_ROWS), :]
            out = []
            for i in range(STATE_ROWS):
                s_kk = jnp.sum(state[i] * kk, axis=0, keepdims=True)
                state[i] = state[i] * decay[t] - s_kk * b + v_rows[i:i + 1] * k
                out.append(jnp.sum(state[i] * r, axis=0, keepdims=True))
            o_ref[t, 0, pl.ds(base, STATE_ROWS), :] = jnp.concatenate(out, axis=0)
        for i in range(STATE_ROWS):
            s_out[0, base + i] = state[i]
        return 0

    lax.fori_loop(0, n_val // STATE_ROWS, rows, 0)


def _rwkv_step(r, k, v, kk, b, lw, s0):
    n_tok, n_heads, dim, batch = r.shape
    assert s0.shape == (n_heads, dim, dim, batch) and dim % STATE_ROWS == 0
    seq_spec = pl.BlockSpec((n_tok, 1, dim, batch), lambda h: (0, h, 0, 0))
    st_spec = pl.BlockSpec((1, dim, dim, batch), lambda h: (h, 0, 0, 0))
    return pl.pallas_call(
        _rwkv_step_kernel,
        grid=(n_heads,),
        in_specs=[seq_spec] * 6 + [st_spec],
        out_specs=[seq_spec, st_spec],
        out_shape=[jax.ShapeDtypeStruct(r.shape, f32), jax.ShapeDtypeStruct(s0.shape, f32)],
        compiler_params=_params("parallel"),
        name="rwkv_step",
    )(r, k, v, kk, b, lw, s0)


def _merge_kernel(h_ref, oa_ref, o_ref, r_ref, k_ref, v_ref, g_ref, ga_ref, gb_ref,
                  rk_ref, lnw_ref, lnb_ref, ones_ref, wba_ref, wbb_ref, wout_ref, gpost_ref,
                  out_ref):
    ones = ones_ref[...]
    inv = 1.0 / HEAD_DIM
    o = o_ref[...]
    mean = _head_sums(o, ones) * inv
    d = o - mean
    var = _head_sums(d * d, ones) * inv
    on = d * lax.rsqrt(var + GN_EPS) * lnw_ref[...] + lnb_ref[...]
    bonus = _head_sums(r_ref[...] * k_ref[...] * rk_ref[...], ones) * v_ref[...]
    ob = (on + bonus) * g_ref[...]
    merged = (ga_ref[...] * jnp.dot(oa_ref[...].astype(bf16), wba_ref[...], preferred_element_type=f32)
              + gb_ref[...] * jnp.dot(ob.astype(bf16), wbb_ref[...], preferred_element_type=f32))
    y = jnp.dot(merged.astype(bf16), wout_ref[...], preferred_element_type=f32)
    out_ref[...] = h_ref[...] + _rms(y, gpost_ref[...])


def _merge(h, oa, o, r, k, v, g, ga, gb, consts, tm):
    n, d = h.shape
    w = oa.shape[1]
    row = lambda width: pl.BlockSpec((tm, width), lambda i: (i, 0))
    return pl.pallas_call(
        _merge_kernel,
        grid=(n // tm,),
        in_specs=[row(d)] + [row(w)] * 6 + [row(d)] * 2 + [_const_spec(c.shape) for c in consts],
        out_specs=row(d),
        out_shape=jax.ShapeDtypeStruct((n, d), f32),
        compiler_params=_params("parallel"),
        name="merge",
    )(h, oa, o, r, k, v, g, ga, gb, *consts)


def _row_tile(n, target):
    t = min(n, target)
    assert n % t == 0
    return t


def _layer(x, prev_shift, wkv0, past, w, n_heads):
    (g1a, g1b, f1g, f1u, f1d, gma, gmb, w_in, w_ba, w_bb, w_out,
     mu, w0, w2, a0, a2, g2, k_k, k_a, r_k, ln_w, ln_b,
     g2a, g2b, f2g, f2u, f2d) = w
    batch, t, d = x.shape
    n = batch * t
    mw = w_ba.shape[0]
    rw = w_bb.shape[0]
    rp = mu.shape[-1]
    widths = (mw, mw, mw, rp, d, d)
    row = lambda a: a.reshape(1, -1)

    x2 = x.reshape(n, d)
    h = _ffn(x2, row(g1a), row(g1b), f1g, f1u, f1d, _row_tile(n, 512))
    kv_shape = (batch, t, n_heads, HEAD_DIM)
    if past is None:
        assert t % MOBA_BLOCK == 0
        q, p_rw, ga, gb, k_t, v_t, k_bf, vt_bf, kmean = _in_proj(
            h, row(gma), w_in, widths, MOBA_BLOCK, seq=t)
        o_a = _moba_prompt(q, k_bf, vt_bf, kmean.reshape(n // MOBA_BLOCK, mw), batch, t, n_heads)
        k, v = (jnp.transpose(a.reshape(batch, n_heads, HEAD_DIM, t), (0, 3, 1, 2)) for a in (k_t, v_t))
    else:
        cache_k, cache_v, page_table = past
        q, p_rw, ga, gb, k, v = _in_proj(h, row(gma), w_in, widths, _row_tile(n, MOBA_BLOCK))
        k, v = k.reshape(kv_shape), v.reshape(kv_shape)
        o_a = _moba_sample(q.reshape(kv_shape), k, v, cache_k, cache_v, page_table).reshape(n, mw)

    chunked = t % RWKV_CHUNK == 0
    t_pad = t if chunked else -(-t // 8) * 8
    bt_pre, tt_pre = (1, min(t, 512)) if chunked else (min(batch, 16), t_pad)
    p3 = p_rw.reshape(batch, t, rp)
    p3p = jnp.pad(p3, ((0, 0), (0, t_pad - t), (0, 0))) if t_pad != t else p3
    pre_w = (row(mu), row(w0), w2, row(a0), a2, g2, row(k_k), row(k_a))
    pre = _rwkv_pre(p3p, prev_shift.reshape(batch, 1, rp), t, pre_w, bt_pre, tt_pre)
    r_, kh, v_, g_ = (pre[i][:, :t].reshape(n, rw) for i in (0, 1, 2, 6))
    if chunked:
        o_scan, s_fin = _rwkv_scan(*pre[:6], _pair_state(wkv0.astype(f32)), min(batch, 2), RWKV_CHUNK)
        o_scan = o_scan.reshape(n, rw)
        wkv = _unpair_state(s_fin, rw // HEAD_DIM).astype(wkv0.dtype)
    else:
        heads = rw // HEAD_DIM
        lanes_last = lambda a: jnp.transpose(a[:, :t], (1, 2, 0)).reshape(t, heads, HEAD_DIM, batch)
        o_scan, s_fin = _rwkv_step(*(lanes_last(a) for a in pre[:6]),
                                   jnp.transpose(wkv0.astype(f32), (1, 2, 3, 0)))
        o_scan = jnp.transpose(o_scan.reshape(t, rw, batch), (2, 0, 1)).reshape(n, rw)
        wkv = jnp.transpose(s_fin, (3, 0, 1, 2)).astype(wkv0.dtype)
    consts = (row(r_k), row(ln_w), row(ln_b), _head_ones(), w_ba, w_bb, w_out, row(gmb))
    h2 = _merge(h, o_a, o_scan, r_, kh, v_, g_, ga, gb, consts, _row_tile(n, 256))
    y = _ffn(h2, row(g2a), row(g2b), f2g, f2u, f2d, _row_tile(n, 512))
    return y.reshape(batch, t, d), k, v, wkv, p3[:, -1]


def kernel(x_prompt, x_sample, cache_k, cache_v, state_wkv, state_shift, page_table,
           g_ffn1_pre, g_ffn1_post, w_ffn1_gate, w_ffn1_up, w_ffn1_down,
           g_mix_pre, g_mix_post, w_in, w_branch_a, w_branch_b, w_out,
           rwkv_mu, rwkv_w0, rwkv_w2, rwkv_a0, rwkv_a2, rwkv_g2, rwkv_k_k, rwkv_k_a, rwkv_r_k,
           rwkv_ln_w, rwkv_ln_b,
           g_ffn2_pre, g_ffn2_post, w_ffn2_gate, w_ffn2_up, w_ffn2_down):
    depth = w_in.shape[0]
    bp = x_prompt.shape[0]
    n_heads = cache_k.shape[3]
    rp = rwkv_mu.shape[-1]
    rw = rwkv_w0.shape[-1]
    yp, ys = x_prompt, x_sample
    outs = [[] for _ in range(8)]
    for i in range(depth):
        cast = lambda a: a[i].astype(bf16)
        w = (g_ffn1_pre[i], g_ffn1_post[i], cast(w_ffn1_gate), cast(w_ffn1_up), cast(w_ffn1_down),
             g_mix_pre[i], g_mix_post[i], cast(w_in), cast(w_branch_a), cast(w_branch_b), cast(w_out),
             rwkv_mu[i], rwkv_w0[i], rwkv_w2[i], rwkv_a0[i], rwkv_a2[i], rwkv_g2[i], rwkv_k_k[i],
             rwkv_k_a[i], rwkv_r_k[i], rwkv_ln_w[i], rwkv_ln_b[i],
             g_ffn2_pre[i], g_ffn2_post[i], cast(w_ffn2_gate), cast(w_ffn2_up), cast(w_ffn2_down))
        yp, kp, vp, wkvp, shp = _layer(
            yp, jnp.zeros((bp, rp), x_prompt.dtype),
            jnp.zeros((bp, rw // HEAD_DIM, HEAD_DIM, HEAD_DIM), state_wkv.dtype), None, w, n_heads)
        ys, ksn, vsn, wkvs, shs = _layer(
            ys, state_shift[i], state_wkv[i], (cache_k[i], cache_v[i], page_table), w, n_heads)
        for lst, val in zip(outs, (kp, vp, ksn, vsn, wkvp, shp, wkvs, shs)):
            lst.append(val)
    return (yp, ys) + tuple(jnp.stack(lst) for lst in outs)
```

```python
import functools
import math

import jax
import jax.numpy as jnp
import numpy as np
from jax import lax
from jax.experimental import pallas as pl
from jax.experimental.pallas import tpu as pltpu

HEAD_DIM = 64
LANES = 128
HEADS_PER_TILE = LANES // HEAD_DIM
MXU_DIM = 256
MOBA_BLOCK = 256
MOBA_TOPK = 3
SUM_ROWS = 16
PIPE_UNROLL = 4
RWKV_CHUNK = 64
MOBA_GROUP_HEADS = 4
RMS_EPS = 1e-6
GN_EPS = 64e-5
VMEM_LIMIT = 56 * 1024 * 1024
NEG_INF = float("-inf")
POS_INF = float("inf")
LOG2E = 1.4426950408889634

f32 = jnp.float32
bf16 = jnp.bfloat16


def _params(*sem):
    return pltpu.CompilerParams(dimension_semantics=sem, vmem_limit_bytes=VMEM_LIMIT)


def _const_spec(shape):
    nd = len(shape)
    return pl.BlockSpec(shape, lambda *_: (0,) * nd, pipeline_mode=pl.Buffered(1))


def _rms(x, g):
    return x * lax.rsqrt(jnp.mean(x * x, axis=-1, keepdims=True) + RMS_EPS) * g


def _split3(x):
    hi = x.astype(bf16)
    r1 = x - hi.astype(f32)
    mid = r1.astype(bf16)
    lo = (r1 - mid.astype(f32)).astype(bf16)
    return hi, mid, lo


_NN = (((1,), (0,)), ((), ()))
_NT = (((1,), (1,)), ((), ()))
_TN = (((0,), (0,)), ((), ()))


def _dg(a, b, dims):
    return lax.dot_general(a, b, dims, preferred_element_type=f32)


def _dot_lo(a, b, dims=_NN):
    return _dg(a.astype(bf16), b.astype(bf16), dims)


def _dot_hi(a, b, dims=_NN):
    ah, am, _ = _split3(a)
    bh, bm, _ = _split3(b)
    return _dg(ah, bh, dims) + (_dg(ah, bm, dims) + _dg(am, bh, dims))


def _dot_exact_lhs(a_bf, b, dims=_NN):
    bh, bm, bl = _split3(b)
    return _dg(a_bf, bh, dims) + (_dg(a_bf, bm, dims) + _dg(a_bf, bl, dims))


def _dot_exact_rhs(a, b_bf, dims=_NN):
    ah = a.astype(bf16)
    am = (a - ah.astype(f32)).astype(bf16)
    return _dg(ah, b_bf, dims) + _dg(am, b_bf, dims)


def _head_sums(x, ones_bf):
    g = ones_bf.shape[0]
    return jnp.concatenate([_dot_exact_rhs(x[:, c:c + g], ones_bf) for c in range(0, x.shape[1], g)],
                           axis=1)


def _ffn_kernel(x_ref, gpre_ref, gpost_ref, wg_ref, wu_ref, wd_ref, o_ref):
    x = x_ref[...]
    h = _rms(x, gpre_ref[...]).astype(bf16)
    gate = jnp.dot(h, wg_ref[...], preferred_element_type=f32)
    up = jnp.dot(h, wu_ref[...], preferred_element_type=f32)
    act = (gate * jax.nn.sigmoid(gate) * up).astype(bf16)
    y = jnp.dot(act, wd_ref[...], preferred_element_type=f32)
    o_ref[...] = x + 0.5 * _rms(y, gpost_ref[...])


def _ffn(x, g_pre, g_post, wg, wu, wd, tm):
    n, d = x.shape
    dff = wg.shape[1]
    return pl.pallas_call(
        _ffn_kernel,
        grid=(n // tm,),
        in_specs=[
            pl.BlockSpec((tm, d), lambda i: (i, 0)),
            _const_spec((1, d)), _const_spec((1, d)),
            _const_spec((d, dff)), _const_spec((d, dff)), _const_spec((dff, d)),
        ],
        out_specs=pl.BlockSpec((tm, d), lambda i: (i, 0)),
        out_shape=jax.ShapeDtypeStruct((n, d), f32),
        compiler_params=_params("parallel"),
        name="ffn",
    )(x, g_pre, g_post, wg, wu, wd)


def _in_proj_kernel(splits, h_ref, g_ref, w_ref, q_ref, p_ref, ga_ref, gb_ref, k_ref, v_ref):
    u = _rms(h_ref[...], g_ref[...]).astype(bf16)
    proj = lambda idx: _dg(u, w_ref[:, splits[idx]:splits[idx + 1]], _NN)
    q_ref[...] = proj(0)
    k_ref[...] = proj(1)
    v_ref[...] = proj(2)
    p_ref[...] = proj(3)
    ga_ref[...] = jax.nn.sigmoid(proj(4))
    gb_ref[...] = jax.nn.sigmoid(proj(5))


def _in_proj_seq_kernel(splits, blocks_per_seq, rwkv_widths, h_ref, g_ref, w_ref, prev_ref, mu_ref, *refs):
    const_refs, out_refs, carry = refs[:8], refs[8:-1], refs[-1]
    q_ref, p_ref, ga_ref, gb_ref, kt_ref, vt_ref, kb_ref, vtb_ref, km_ref = out_refs[:9]
    @pl.when(pl.program_id(0) % blocks_per_seq == 0)
    def _():
        carry[...] = prev_ref[0]

    u = _rms(h_ref[...], g_ref[...]).astype(bf16)
    proj = lambda idx: _dg(u, w_ref[:, splits[idx]:splits[idx + 1]], _NN)
    p = proj(3)
    p_ref[...] = p
    rows = p.shape[0]
    first = lax.broadcasted_iota(jnp.int32, (rows, 1), 0) == 0
    shifted = jnp.where(first, carry[...], pltpu.roll(p, 1, axis=0))
    carry[...] = p[rows - 1:rows, :]
    feats = _rwkv_features(p + (shifted - p) * mu_ref[...], rwkv_widths, *const_refs)
    for o_ref, x in zip(out_refs[9:], feats):
        o_ref[...] = x

    q_ref[...] = proj(0)
    ga_ref[...] = jax.nn.sigmoid(proj(4))
    gb_ref[...] = jax.nn.sigmoid(proj(5))
    k, v = proj(1), proj(2)
    v_t = v.T
    kt_ref[0] = k.T
    vt_ref[0] = v_t
    kb_ref[...] = k.astype(bf16)
    vtb_ref[0] = v_t.astype(bf16)
    km_ref[0] = jnp.mean(k, axis=0, keepdims=True)


def _proj_splits(widths, w_in):
    splits = tuple(int(s) for s in np.concatenate([[0], np.cumsum(widths)]))
    assert splits[-1] == w_in.shape[1] and all(s % LANES == 0 for s in splits)
    return splits


def _in_proj(h, g, w_in, widths, tm):
    n, d = h.shape
    order = (0, 3, 4, 5, 1, 2)
    return pl.pallas_call(
        functools.partial(_in_proj_kernel, _proj_splits(widths, w_in)),
        grid=(n // tm,),
        in_specs=[
            pl.BlockSpec((tm, d), lambda i: (i, 0)),
            _const_spec((1, d)),
            _const_spec(w_in.shape),
        ],
        out_specs=[pl.BlockSpec((tm, widths[j]), lambda i: (i, 0)) for j in order],
        out_shape=[jax.ShapeDtypeStruct((n, widths[j]), f32) for j in order],
        compiler_params=_params("parallel"),
        name="in_proj",
    )(h, g, w_in)


def _in_proj_seq(h, g, w_in, widths, seq, prev, rwkv_weights):
    n, d = h.shape
    tm = MOBA_BLOCK
    assert seq % tm == 0
    nbs = seq // tm
    mw = widths[1]
    rwkv_widths, (mu, *consts) = _rwkv_consts(rwkv_weights)
    rw, rp = rwkv_widths[0], mu.shape[-1]
    row = lambda w, dt=f32: (pl.BlockSpec((tm, w), lambda i: (i, 0)), jax.ShapeDtypeStruct((n, w), dt))
    t_spec = pl.BlockSpec((1, mw, tm), lambda i: (i // nbs, 0, i % nbs))
    t_shape = jax.ShapeDtypeStruct((n // seq, mw, seq), f32)
    outs = [row(widths[0]), row(widths[3]), row(widths[4]), row(widths[5]),
            (t_spec, t_shape), (t_spec, t_shape), row(mw, bf16),
            (pl.BlockSpec((1, mw, tm), lambda i: (i, 0, 0)), jax.ShapeDtypeStruct((n // tm, mw, tm), bf16)),
            (pl.BlockSpec((1, 1, mw), lambda i: (i, 0, 0)), jax.ShapeDtypeStruct((n // tm, 1, mw), f32))]
    outs += [row(rw)] * 7
    return pl.pallas_call(
        functools.partial(_in_proj_seq_kernel, _proj_splits(widths, w_in), nbs, rwkv_widths),
        grid=(n // tm,),
        in_specs=[
            pl.BlockSpec((tm, d), lambda i: (i, 0)),
            _const_spec((1, d)),
            _const_spec(w_in.shape),
            pl.BlockSpec((1, 1, rp), lambda i: (i // nbs, 0, 0)),
            _const_spec(mu.shape),
        ] + [_const_spec(c.shape) for c in consts],
        out_specs=[o[0] for o in outs],
        out_shape=[o[1] for o in outs],
        scratch_shapes=[pltpu.VMEM((1, rp), f32)],
        compiler_params=_params("arbitrary"),
        name="in_proj_seq",
    )(h, g, w_in, prev, mu, *consts)


def _top_blocks(gate, topk):
    blk = lax.broadcasted_iota(jnp.int32, gate.shape, 0).astype(f32)
    sel = jnp.zeros(gate.shape, f32)
    big = float(gate.shape[0])
    for _ in range(topk):
        m = jnp.max(gate, axis=0, keepdims=True)
        cand = (gate == m) & (gate > NEG_INF)
        idx = jnp.min(jnp.where(cand, blk, big), axis=0, keepdims=True)
        hit = blk == idx
        sel = jnp.where(hit, 1.0, sel)
        gate = jnp.where(hit, NEG_INF, gate)
    return sel


def _moba_prompt_kernel(n_heads, q_ref, kb_ref, vt_ref, km_ref, o_ref, off_sc, s_sc):
    group = pl.program_id(1)
    qi = pl.program_id(2)
    tq = q_ref.shape[0]
    nb = km_ref.shape[0]
    heads = q_ref.shape[1] // HEAD_DIM
    tile = lambda j: pl.ds((j // HEADS_PER_TILE) * LANES, LANES)
    lane = lax.broadcasted_iota(jnp.int32, (1, LANES), 1)
    key_i = lax.broadcasted_iota(jnp.int32, (MOBA_BLOCK, tq), 0)
    qry_i = lax.broadcasted_iota(jnp.int32, (MOBA_BLOCK, tq), 1)
    rel = (qry_i - key_i).astype(f32)
    causal = key_i <= qry_i
    blk_row = lax.broadcasted_iota(jnp.int32, (nb, tq), 0)

    own = pl.ds(pl.multiple_of(qi * MOBA_BLOCK, MOBA_BLOCK), MOBA_BLOCK)
    ones_rows = jnp.ones((SUM_ROWS, MOBA_BLOCK), bf16)
    v_rows = lambda n, j: jnp.concatenate([vt_ref[n, pl.ds(j * HEAD_DIM, HEAD_DIM), :], ones_rows], axis=0)

    consts, carry = [], []
    for j in range(heads):
        q = q_ref[:, tile(j)]
        head_mask = (lane // HEAD_DIM == j % HEADS_PER_TILE).astype(f32)
        head = jnp.full((1, 1), group * heads + j + 1, jnp.int32).astype(f32)
        slope2 = jnp.exp2(head * (-8.0 / n_heads)) * LOG2E
        gate = _dot_hi(km_ref[:, tile(j)] * head_mask, q, _NT)
        gate = jnp.where(blk_row < qi, gate, NEG_INF)
        off_sc[j] = jnp.where(_top_blocks(gate, MOBA_TOPK) > 0.5, 0.0, POS_INF)
        qs = (q * head_mask * (HEAD_DIM ** -0.5 * LOG2E)).astype(bf16)
        alibi = slope2 * rel

        s = _dg(kb_ref[own, tile(j)], qs, _NT) - alibi
        s = jnp.where(causal, s, NEG_INF)
        m = jnp.max(s, axis=0, keepdims=True)
        p = jnp.exp2(s - m).astype(bf16)
        acc = _dg(v_rows(qi, j), p, _NN)
        consts.append((qs, alibi, slope2))
        carry += [m, acc]

    def score(n, slot):
        nc = jnp.minimum(n, nb - 1)
        rows = pl.ds(pl.multiple_of(nc * MOBA_BLOCK, MOBA_BLOCK), MOBA_BLOCK)
        raw = []
        for j in range(heads):
            s = _dg(kb_ref[rows, tile(j)], consts[j][0], _NT) - consts[j][1]
            s_sc[slot, j] = s
            raw.append(jnp.max(s, axis=0, keepdims=True))
        return raw

    def attend(n, slot, raw, stats):
        nc = jnp.minimum(n, nb - 1)
        out = []
        for j in range(heads):
            slope2 = consts[j][2]
            m, acc = stats[2 * j:2 * j + 2]
            off = jnp.where(n < qi, off_sc[j, pl.ds(nc, 1), :], POS_INF)
            far = ((qi - n) * MOBA_BLOCK).astype(f32) * slope2
            m_new = jnp.maximum(m, raw[j] - far - off)
            alpha = jnp.exp2(m - m_new)
            p = jnp.exp2(s_sc[slot, j] - (m_new + far + off)).astype(bf16)
            out += [m_new, acc * alpha + _dg(v_rows(nc, j), p, _NN)]
        return out

    def body(i, carry):
        stats, raw = list(carry[:-heads]), list(carry[-heads:])
        for u in range(PIPE_UNROLL):
            n = PIPE_UNROLL * i + u
            raw_next = score(n + 1, (u + 1) % 2)
            stats = attend(n, u % 2, raw, stats)
            raw = raw_next
        return tuple(stats + raw)

    trips = (qi + PIPE_UNROLL - 1) // PIPE_UNROLL
    carry = lax.fori_loop(0, trips, body, tuple(carry + score(0, 0)))
    accs = [carry[2 * j + 1] for j in range(heads)]
    out_t = jnp.concatenate([acc[:HEAD_DIM] / acc[HEAD_DIM:HEAD_DIM + 1] for acc in accs], axis=0)
    o_ref[...] = out_t.T


def _moba_prompt(q, k_bf, v_t, kmean, batch, seq, n_heads):
    n, w = q.shape
    wg = MOBA_GROUP_HEADS * HEAD_DIM
    assert w % wg == 0 and wg % LANES == 0
    nb = seq // MOBA_BLOCK
    tq = MOBA_BLOCK
    return pl.pallas_call(
        functools.partial(_moba_prompt_kernel, n_heads),
        grid=(batch, w // wg, nb),
        in_specs=[
            pl.BlockSpec((tq, wg), lambda b, p, i: (b * nb + i, p)),
            pl.BlockSpec((seq, wg), lambda b, p, i: (b, p)),
            pl.BlockSpec((nb, wg, MOBA_BLOCK), lambda b, p, i: (b, p, 0)),
            pl.BlockSpec((nb, wg), lambda b, p, i: (b, p)),
        ],
        out_specs=pl.BlockSpec((tq, wg), lambda b, p, i: (b * nb + i, p)),
        out_shape=jax.ShapeDtypeStruct((n, w), f32),
        scratch_shapes=[pltpu.VMEM((MOBA_GROUP_HEADS, nb, tq), f32),
                        pltpu.VMEM((2, MOBA_GROUP_HEADS, MOBA_BLOCK, tq), f32)],
        compiler_params=_params("parallel", "parallel", "arbitrary"),
        name="moba_prompt",
    )(q, k_bf, v_t, kmean)


def _per_head(a, b, dims):
    a, b = a.astype(bf16), b.astype(bf16)
    return jnp.stack([_dg(a[h], b[h], dims) for h in range(a.shape[0])], axis=0)


def _moba_sample_kernel(t_new, n_blocks, pt_ref, q_ref, qt_ref, kn_ref, vn_ref, *refs):
    del pt_ref
    k_refs, v_refs, o_ref = refs[:2 * n_blocks], refs[2 * n_blocks:4 * n_blocks], refs[4 * n_blocks]
    n_heads, t_pad = q_ref.shape[1], q_ref.shape[2]
    page = k_refs[0].shape[3]
    q = q_ref[0]
    qt = qt_ref[0]
    head = lax.broadcasted_iota(jnp.int32, (n_heads, 1, 1), 0)
    slope = jnp.exp2((head + 1).astype(f32) * (-8.0 / n_heads))
    t_q = lax.broadcasted_iota(jnp.int32, (1, t_pad, 1), 1)
    qs = (q * (HEAD_DIM ** -0.5)).astype(bf16)
    tok = lax.broadcasted_iota(jnp.int32, (1, 1, page), 2)
    past_len = n_blocks * MOBA_BLOCK

    def scores(kt, first_pos):
        dist = (past_len - first_pos + t_q - tok).astype(f32)
        return _per_head(qs, kt, _NN) - slope * dist

    blocks, pages = range(n_blocks), range(2 * n_blocks)
    kt = [k_refs[i][0] for i in pages]
    s = [scores(kt[i], i * page) for i in pages]
    row_max = [jnp.max(s[i], axis=-1, keepdims=True) for i in pages]
    m_blk = [jnp.maximum(row_max[2 * n], row_max[2 * n + 1]) for n in blocks]
    p = [jnp.exp(s[i] - m_blk[i // 2]) for i in pages]
    row_sum = [jnp.sum(p[i], axis=-1, keepdims=True) for i in pages]
    l_blk = [row_sum[2 * n] + row_sum[2 * n + 1] for n in blocks]
    pv = [_per_head(p[i], v_refs[i][0], _NT) for i in pages]
    acc_blk = [pv[2 * n] + pv[2 * n + 1] for n in blocks]
    kmean = [jnp.sum(kt[2 * n] + kt[2 * n + 1], axis=-1, keepdims=True) * (1.0 / MOBA_BLOCK)
             for n in blocks]
    gates = [jnp.sum(qt * kmean[n], axis=1, keepdims=True) for n in blocks]

    t_k = lax.broadcasted_iota(jnp.int32, (1, t_pad, t_pad), 2)
    s_own = _per_head(qs, kn_ref[0], _NT) - slope * (t_q - t_k).astype(f32)
    s_own = jnp.where((t_k <= t_q) & (t_k < t_new), s_own, NEG_INF)
    m_tot = jnp.max(s_own, axis=-1, keepdims=True)
    eye = (lax.broadcasted_iota(jnp.int32, (1, t_pad, t_pad), 1) == t_k).astype(f32)
    chosen = []
    for n in range(n_blocks):
        rank = jnp.zeros_like(gates[n])
        for i in range(n_blocks):
            if i == n:
                continue
            ahead = (gates[i] > gates[n]) | ((gates[i] == gates[n]) & (i < n))
            rank = rank + ahead.astype(f32)
        picked = (rank < MOBA_TOPK).astype(f32)
        chosen.append(jnp.sum(eye * picked, axis=-1, keepdims=True) > 0.5)
    for n in range(n_blocks):
        m_tot = jnp.maximum(m_tot, jnp.where(chosen[n], m_blk[n], NEG_INF))
    p_own = jnp.exp(s_own - m_tot)
    l_tot = jnp.sum(p_own, axis=-1, keepdims=True)
    acc_tot = _per_head(p_own, vn_ref[0], _NN)
    for n in range(n_blocks):
        wgt = jnp.where(chosen[n], jnp.exp(m_blk[n] - m_tot), 0.0)
        l_tot = l_tot + wgt * l_blk[n]
        acc_tot = acc_tot + wgt * acc_blk[n]
    o_ref[0] = acc_tot / l_tot


def _moba_sample(q, k_new, v_new, cache_k, cache_v, page_table):
    batch, t_new, n_heads, dim = q.shape
    n_pool, page = cache_k.shape[:2]
    n_pages = page_table.shape[1]
    t_pad = 8
    assert dim == HEAD_DIM and MOBA_BLOCK == 2 * page and n_pages % 2 == 0 and t_new <= t_pad
    n_blocks = n_pages // 2
    assert n_blocks >= MOBA_TOPK
    cache_kt = jnp.transpose(cache_k, (0, 2, 3, 1))
    cache_vt = jnp.transpose(cache_v, (0, 2, 3, 1))
    by_head = lambda a: jnp.pad(jnp.transpose(a, (0, 2, 1, 3)), ((0, 0), (0, 0), (0, t_pad - t_new), (0, 0)))
    qh, kh, vh = by_head(q), by_head(k_new), by_head(v_new)
    qt = jnp.transpose(qh, (0, 1, 3, 2))

    def page_spec(i):
        return pl.BlockSpec((1, n_heads, dim, page), lambda b, pt: (pt[b, i], 0, 0, 0))

    pages = [page_spec(i) for i in range(n_pages)]
    row_spec = pl.BlockSpec((1, n_heads, t_pad, dim), lambda b, pt: (b, 0, 0, 0))
    out = pl.pallas_call(
        functools.partial(_moba_sample_kernel, t_new, n_blocks),
        grid_spec=pltpu.PrefetchScalarGridSpec(
            num_scalar_prefetch=1,
            grid=(batch,),
            in_specs=[row_spec, pl.BlockSpec((1, n_heads, dim, t_pad), lambda b, pt: (b, 0, 0, 0)),
                      row_spec, row_spec] + pages + pages,
            out_specs=row_spec,
        ),
        out_shape=jax.ShapeDtypeStruct((batch, n_heads, t_pad, dim), f32),
        compiler_params=_params("parallel"),
        name="moba_sample",
    )(page_table, qh, qt, kh, vh, *([cache_kt] * n_pages), *([cache_vt] * n_pages))
    return jnp.transpose(out[:, :, :t_new], (0, 2, 1, 3))


def _rwkv_features(m, widths, w0_ref, w2_ref, a0_ref, a2_ref, g2_ref, kk_ref, ka_ref, ones_ref):
    rw, dl, al, gl = widths
    r = m[:, 0:rw]
    k = m[:, rw:2 * rw]
    v = m[:, 2 * rw:3 * rw]
    lora = m[:, 3 * rw:3 * rw + dl + al]
    gd = m[:, 3 * rw + dl + al:3 * rw + dl + al + gl]

    z = -(w0_ref[...] + _dot_hi(jnp.tanh(lora), w2_ref[...]))
    softplus = jnp.maximum(z, 0.0) + jnp.log(1.0 + jnp.exp(-jnp.abs(z)))
    w = -softplus - 0.5
    log_decay = -jnp.exp(w)
    a = jax.nn.sigmoid(a0_ref[...] + _dot_hi(lora, a2_ref[...]))
    g = _dot_hi(jax.nn.sigmoid(gd), g2_ref[...])
    kk = k * kk_ref[...]
    norm = jnp.sqrt(_head_sums(kk * kk, ones_ref[...]))
    kk = kk / jnp.maximum(norm, 1e-12)
    kh = k * (1.0 + (a - 1.0) * ka_ref[...])
    return [r, kh, v, kk, kk * a, log_decay, g]


def _rwkv_pre_kernel(t_valid, t_pad, widths, p_ref, prev_ref, mu_ref, *refs):
    const_refs, out_refs, carry = refs[:8], refs[8:15], refs[15]
    ti = pl.program_id(1)
    bt, tt, width = p_ref.shape
    rw = widths[0]

    @pl.when(ti == 0)
    def _():
        carry[...] = prev_ref[...]

    p = p_ref[...]
    t_idx = lax.broadcasted_iota(jnp.int32, (bt, tt, 1), 1)
    shifted = jnp.where(t_idx == 0, carry[...], pltpu.roll(p, 1, axis=1))
    carry[...] = p[:, tt - 1:tt, :]
    m = (p + (shifted - p) * mu_ref[...]).reshape(bt * tt, width)
    *outs, g = _rwkv_features(m, widths, *const_refs)
    if t_valid < t_pad:
        live = (lax.broadcasted_iota(jnp.int32, (bt, tt, 1), 1) + ti * tt < t_valid)
        live = live.astype(f32).reshape(bt * tt, 1)
        outs = [x * live for x in outs]
    shape3 = (bt, tt, rw)
    for o_ref, x in zip(out_refs, outs + [g]):
        o_ref[...] = x.reshape(shape3)


def _head_ones():
    idx = np.arange(MXU_DIM) // HEAD_DIM
    return jnp.asarray(idx[:, None] == idx[None, :], dtype=bf16)


def _rwkv_consts(weights):
    mu, w0, w2, a0, a2, g2, k_k, k_a = weights
    rw = w0.shape[-1]
    widths = (rw, w2.shape[0], a2.shape[0], g2.shape[0])
    assert 3 * rw + sum(widths[1:]) == mu.shape[-1]
    assert widths[1] + widths[2] == LANES and widths[3] % LANES == 0 and rw % MXU_DIM == 0
    w2p = jnp.pad(w2, ((0, widths[2]), (0, 0)))
    a2p = jnp.pad(a2, ((widths[1], 0), (0, 0)))
    return widths, (mu, w0, w2p, a0, a2p, g2, k_k, k_a, _head_ones())


def _rwkv_pre(p, prev, t_valid, weights, bt, tt):
    batch, t_pad, width = p.shape
    widths, consts = _rwkv_consts(weights)
    rw = widths[0]
    out_spec = pl.BlockSpec((bt, tt, rw), lambda b, t: (b, t, 0))
    return pl.pallas_call(
        functools.partial(_rwkv_pre_kernel, t_valid, t_pad, widths),
        grid=(batch // bt, t_pad // tt),
        in_specs=[pl.BlockSpec((bt, tt, width), lambda b, t: (b, t, 0)),
                  pl.BlockSpec((bt, 1, width), lambda b, t: (b, 0, 0))]
                 + [_const_spec(c.shape) for c in consts],
        out_specs=[out_spec] * 7,
        out_shape=[jax.ShapeDtypeStruct((batch, t_pad, rw), f32)] * 7,
        scratch_shapes=[pltpu.VMEM((bt, 1, width), f32)],
        compiler_params=_params("parallel", "arbitrary"),
        name="rwkv_pre",
    )(p, prev, *consts)


def _rwkv_scan_kernel(r_ref, k_ref, v_ref, kk_ref, b_ref, lw_ref, s0_ref, o_ref, s_out, state):
    ci = pl.program_id(1)
    bt, c, width = r_ref.shape
    n_pairs = width // LANES
    c2 = HEADS_PER_TILE * c
    n_double = int(math.log2(c))
    assert 1 << n_double == c and HEADS_PER_TILE == 2

    @pl.when(ci == 0)
    def _():
        state[...] = s0_ref[...]

    ri = lax.broadcasted_iota(jnp.int32, (c, c), 0)
    cj = lax.broadcasted_iota(jnp.int32, (c, c), 1)
    tri_bf = (cj <= ri).astype(bf16)
    r2 = lax.broadcasted_iota(jnp.int32, (c2, c2), 0)
    q2 = lax.broadcasted_iota(jnp.int32, (c2, c2), 1)
    same_blk = (r2 >= c) == (q2 >= c)
    strict2 = (same_blk & (q2 < r2)).astype(f32)
    incl2 = (same_blk & (q2 <= r2)).astype(f32)
    eye2 = (q2 == r2).astype(f32)
    lane = lax.broadcasted_iota(jnp.int32, (1, LANES), 1)
    hm = [(lane // HEAD_DIM == j).astype(f32) for j in range(HEADS_PER_TILE)]
    bi_ = lax.broadcasted_iota(jnp.int32, (LANES, LANES), 0) // HEAD_DIM
    bj_ = lax.broadcasted_iota(jnp.int32, (LANES, LANES), 1) // HEAD_DIM
    block_diag = (bi_ == bj_).astype(f32)

    def stack(x):
        return jnp.concatenate([x * hm[0], x * hm[1]], axis=0)

    def twice(x):
        return jnp.concatenate([x, x], axis=0)

    def unstack(x2):
        return x2[:c] * hm[0] + x2[c:] * hm[1]

    probs = [(bi, pl.ds(pair * LANES, LANES), pair) for bi in range(bt) for pair in range(n_pairs)]
    ld = lambda ref: [ref[bi, :, sl] for bi, sl, _ in probs]
    r, k, v, kk, b, lw = ld(r_ref), ld(k_ref), ld(v_ref), ld(kk_ref), ld(b_ref), ld(lw_ref)
    nprob = len(probs)
    each = range(nprob)

    cum = [_dot_exact_lhs(tri_bf, lw[i]) for i in each]
    cum_end = [cum[i][c - 1:c, :] for i in each]
    e_neg = [jnp.exp(-cum[i]) for i in each]
    e_end = [jnp.exp(cum_end[i] - cum[i]) for i in each]
    a_mat = [kk[i] * jnp.exp(cum[i] - lw[i]) for i in each]
    p_mat = [r[i] * jnp.exp(cum[i]) for i in each]
    xa = [stack(a_mat[i]) for i in each]
    xap = [jnp.concatenate([xa[i], stack(p_mat[i])], axis=0) for i in each]
    yb = [stack(b[i] * e_neg[i]) for i in each]
    yk = [stack(k[i] * e_neg[i]) for i in each]
    v2 = [twice(v[i]) for i in each]

    gb = [_dot_lo(xap[i], yb[i], _NT) for i in each]
    gk = [_dot_lo(xap[i], yk[i], _NT) for i in each]
    l_mat = [gb[i][:c2] * strict2 for i in each]
    m_mat = [gk[i][:c2] * strict2 for i in each]
    lr_mat = [gb[i][c2:] * incl2 for i in each]
    mr_mat = [gk[i][c2:] * incl2 for i in each]

    npow = [-l_mat[i] for i in each]
    t_inv = [eye2 + npow[i] for i in each]
    for _ in range(n_double - 1):
        npow = [_dot_lo(npow[i], npow[i]) for i in each]
        t_inv = [t_inv[i] + _dot_lo(npow[i], t_inv[i]) for i in each]
    mv = [_dot_lo(m_mat[i], v2[i]) for i in each]
    tz = [_dot_lo(t_inv[i], jnp.concatenate([xa[i], mv[i]], axis=1)) for i in each]
    ta = [tz[i][:c, :LANES] + tz[i][c:, :LANES] for i in each]
    tmv = [unstack(tz[i][:, LANES:]) for i in each]

    s_prev = [state[bi, pair] for bi, _, pair in probs]
    xs = [_dot_lo(jnp.concatenate([ta[i], p_mat[i]], axis=0), s_prev[i], _NT) for i in each]
    u = [-(xs[i][:c] + tmv[i]) for i in each]
    corr = [_dot_lo(lr_mat[i], twice(u[i])) + _dot_lo(mr_mat[i], v2[i]) for i in each]
    upd = [_dot_lo(jnp.concatenate([u[i], v[i]], axis=0),
                   jnp.concatenate([b[i] * e_end[i], k[i] * e_end[i]], axis=0), _TN) for i in each]
    for i, (bi, sl, pair) in enumerate(probs):
        state[bi, pair] = s_prev[i] * jnp.exp(cum_end[i]) + upd[i] * block_diag
        o_ref[bi, :, sl] = xs[i][c:] + unstack(corr[i])

    @pl.when(ci == pl.num_programs(1) - 1)
    def _():
        s_out[...] = state[...]


def _rwkv_scan(r, k, v, kk, b, lw, s0, bt, chunk):
    batch, t_pad, width = r.shape
    n_pairs = width // LANES
    seq_spec = pl.BlockSpec((bt, chunk, width), lambda bb, c: (bb, c, 0))
    st_spec = pl.BlockSpec((bt, n_pairs, LANES, LANES), lambda bb, c: (bb, 0, 0, 0))
    return pl.pallas_call(
        _rwkv_scan_kernel,
        grid=(batch // bt, t_pad // chunk),
        in_specs=[seq_spec] * 6 + [st_spec],
        out_specs=[seq_spec, st_spec],
        out_shape=[jax.ShapeDtypeStruct((batch, t_pad, width), f32),
                   jax.ShapeDtypeStruct((batch, n_pairs, LANES, LANES), f32)],
        scratch_shapes=[pltpu.VMEM((bt, n_pairs, LANES, LANES), f32)],
        compiler_params=_params("parallel", "arbitrary"),
        name="rwkv_scan",
    )(r, k, v, kk, b, lw, s0)


def _pair_state(s):
    bsz, h, n, _ = s.shape
    s = s.reshape(bsz, h // HEADS_PER_TILE, HEADS_PER_TILE, n, n)
    eye = jnp.eye(HEADS_PER_TILE, dtype=s.dtype)
    out = jnp.einsum("bpjvk,ji->bpjvik", s, eye)
    return out.reshape(bsz, h // HEADS_PER_TILE, LANES, LANES)


def _unpair_state(s, heads):
    bsz = s.shape[0]
    s = s.reshape(bsz, heads // HEADS_PER_TILE, HEADS_PER_TILE, HEAD_DIM, HEADS_PER_TILE, HEAD_DIM)
    diag = jnp.stack([s[:, :, j, :, j, :] for j in range(HEADS_PER_TILE)], axis=2)
    return diag.reshape(bsz, heads, HEAD_DIM, HEAD_DIM)


STATE_ROWS = 8


def _rwkv_step_kernel(r_ref, k_ref, v_ref, kk_ref, b_ref, lw_ref, s_ref, o_ref, s_out):
    n_tok = r_ref.shape[0]
    n_val = s_ref.shape[1]
    decay = [jnp.exp(lw_ref[t, 0]) for t in range(n_tok)]

    def rows(c, _):
        base = pl.multiple_of(c * STATE_ROWS, STATE_ROWS)
        state = [s_ref[0, base + i] for i in range(STATE_ROWS)]
        for t in range(n_tok):
            kk, b, k, r = kk_ref[t, 0], b_ref[t, 0], k_ref[t, 0], r_ref[t, 0]
            v_rows = v_ref[t, 0, pl.ds(base, STATE_ROWS), :]
            out = []
            for i in range(STATE_ROWS):
                s_kk = jnp.sum(state[i] * kk, axis=0, keepdims=True)
                state[i] = state[i] * decay[t] - s_kk * b + v_rows[i:i + 1] * k
                out.append(jnp.sum(state[i] * r, axis=0, keepdims=True))
            o_ref[t, 0, pl.ds(base, STATE_ROWS), :] = jnp.concatenate(out, axis=0)
        for i in range(STATE_ROWS):
            s_out[0, base + i] = state[i]
        return 0

    lax.fori_loop(0, n_val // STATE_ROWS, rows, 0)


def _rwkv_step(r, k, v, kk, b, lw, s0):
    n_tok, n_heads, dim, batch = r.shape
    assert s0.shape == (n_heads, dim, dim, batch) and dim % STATE_ROWS == 0
    seq_spec = pl.BlockSpec((n_tok, 1, dim, batch), lambda h: (0, h, 0, 0))
    st_spec = pl.BlockSpec((1, dim, dim, batch), lambda h: (h, 0, 0, 0))
    return pl.pallas_call(
        _rwkv_step_kernel,
        grid=(n_heads,),
        in_specs=[seq_spec] * 6 + [st_spec],
        out_specs=[seq_spec, st_spec],
        out_shape=[jax.ShapeDtypeStruct(r.shape, f32), jax.ShapeDtypeStruct(s0.shape, f32)],
        compiler_params=_params("parallel"),
        name="rwkv_step",
    )(r, k, v, kk, b, lw, s0)


def _merge_kernel(h_ref, oa_ref, o_ref, r_ref, k_ref, v_ref, g_ref, ga_ref, gb_ref,
                  rk_ref, lnw_ref, lnb_ref, ones_ref, wba_ref, wbb_ref, wout_ref, gpost_ref,
                  out_ref):
    ones = ones_ref[...]
    inv = 1.0 / HEAD_DIM
    o = o_ref[...]
    mean = _head_sums(o, ones) * inv
    d = o - mean
    var = _head_sums(d * d, ones) * inv
    on = d * lax.rsqrt(var + GN_EPS) * lnw_ref[...] + lnb_ref[...]
    bonus = _head_sums(r_ref[...] * k_ref[...] * rk_ref[...], ones) * v_ref[...]
    ob = (on + bonus) * g_ref[...]
    merged = (ga_ref[...] * jnp.dot(oa_ref[...].astype(bf16), wba_ref[...], preferred_element_type=f32)
              + gb_ref[...] * jnp.dot(ob.astype(bf16), wbb_ref[...], preferred_element_type=f32))
    y = jnp.dot(merged.astype(bf16), wout_ref[...], preferred_element_type=f32)
    out_ref[...] = h_ref[...] + _rms(y, gpost_ref[...])


def _merge(h, oa, o, r, k, v, g, ga, gb, consts, tm):
    n, d = h.shape
    w = oa.shape[1]
    row = lambda width: pl.BlockSpec((tm, width), lambda i: (i, 0))
    return pl.pallas_call(
        _merge_kernel,
        grid=(n // tm,),
        in_specs=[row(d)] + [row(w)] * 6 + [row(d)] * 2 + [_const_spec(c.shape) for c in consts],
        out_specs=row(d),
        out_shape=jax.ShapeDtypeStruct((n, d), f32),
        compiler_params=_params("parallel"),
        name="merge",
    )(h, oa, o, r, k, v, g, ga, gb, *consts)


def _row_tile(n, target):
    t = min(n, target)
    assert n % t == 0
    return t


def _layer(x, prev_shift, wkv0, past, w, n_heads):
    (g1a, g1b, f1g, f1u, f1d, gma, gmb, w_in, w_ba, w_bb, w_out,
     mu, w0, w2, a0, a2, g2, k_k, k_a, r_k, ln_w, ln_b,
     g2a, g2b, f2g, f2u, f2d) = w
    batch, t, d = x.shape
    n = batch * t
    mw = w_ba.shape[0]
    rw = w_bb.shape[0]
    rp = mu.shape[-1]
    widths = (mw, mw, mw, rp, d, d)
    row = lambda a: a.reshape(1, -1)

    x2 = x.reshape(n, d)
    h = _ffn(x2, row(g1a), row(g1b), f1g, f1u, f1d, _row_tile(n, 512))
    kv_shape = (batch, t, n_heads, HEAD_DIM)
    pre_w = (row(mu), row(w0), w2, row(a0), a2, g2, row(k_k), row(k_a))
    prev = prev_shift.reshape(batch, 1, rp)
    if past is None:
        assert t % MOBA_BLOCK == 0 and t % RWKV_CHUNK == 0
        q, p_rw, ga, gb, k_t, v_t, k_bf, vt_bf, kmean, *pre = _in_proj_seq(
            h, row(gma), w_in, widths, t, prev, pre_w)
        o_a = _moba_prompt(q, k_bf, vt_bf, kmean.reshape(n // MOBA_BLOCK, mw), batch, t, n_heads)
        k, v = (jnp.transpose(a.reshape(batch, n_heads, HEAD_DIM, t), (0, 3, 1, 2)) for a in (k_t, v_t))
        r_, kh, v_, g_ = (pre[i] for i in (0, 1, 2, 6))
        o_scan, s_fin = _rwkv_scan(*(a.reshape(batch, t, rw) for a in pre[:6]),
                                   _pair_state(wkv0.astype(f32)), min(batch, 2), RWKV_CHUNK)
        o_scan = o_scan.reshape(n, rw)
        wkv = _unpair_state(s_fin, rw // HEAD_DIM).astype(wkv0.dtype)
    else:
        cache_k, cache_v, page_table = past
        q, p_rw, ga, gb, k, v = _in_proj(h, row(gma), w_in, widths, _row_tile(n, MOBA_BLOCK))
        k, v = k.reshape(kv_shape), v.reshape(kv_shape)
        o_a = _moba_sample(q.reshape(kv_shape), k, v, cache_k, cache_v, page_table).reshape(n, mw)
        t_pad = -(-t // 8) * 8
        p3p = jnp.pad(p_rw.reshape(batch, t, rp), ((0, 0), (0, t_pad - t), (0, 0)))
        pre = _rwkv_pre(p3p, prev, t, pre_w, min(batch, 16), t_pad)
        r_, kh, v_, g_ = (pre[i][:, :t].reshape(n, rw) for i in (0, 1, 2, 6))
        heads = rw // HEAD_DIM
        lanes_last = lambda a: jnp.transpose(a[:, :t], (1, 2, 0)).reshape(t, heads, HEAD_DIM, batch)
        o_scan, s_fin = _rwkv_step(*(lanes_last(a) for a in pre[:6]),
                                   jnp.transpose(wkv0.astype(f32), (1, 2, 3, 0)))
        o_scan = jnp.transpose(o_scan.reshape(t, rw, batch), (2, 0, 1)).reshape(n, rw)
        wkv = jnp.transpose(s_fin, (3, 0, 1, 2)).astype(wkv0.dtype)
    p3 = p_rw.reshape(batch, t, rp)
    consts = (row(r_k), row(ln_w), row(ln_b), _head_ones(), w_ba, w_bb, w_out, row(gmb))
    h2 = _merge(h, o_a, o_scan, r_, kh, v_, g_, ga, gb, consts, _row_tile(n, 256))
    y = _ffn(h2, row(g2a), row(g2b), f2g, f2u, f2d, _row_tile(n, 512))
    return y.reshape(batch, t, d), k, v, wkv, p3[:, -1]


def kernel(x_prompt, x_sample, cache_k, cache_v, state_wkv, state_shift, page_table,
           g_ffn1_pre, g_ffn1_post, w_ffn1_gate, w_ffn1_up, w_ffn1_down,
           g_mix_pre, g_mix_post, w_in, w_branch_a, w_branch_b, w_out,
           rwkv_mu, rwkv_w0, rwkv_w2, rwkv_a0, rwkv_a2, rwkv_g2, rwkv_k_k, rwkv_k_a, rwkv_r_k,
           rwkv_ln_w, rwkv_ln_b,
           g_ffn2_pre, g_ffn2_post, w_ffn2_gate, w_ffn2_up, w_ffn2_down):
    depth = w_in.shape[0]
    bp = x_prompt.shape[0]
    n_heads = cache_k.shape[3]
    rp = rwkv_mu.shape[-1]
    rw = rwkv_w0.shape[-1]
    yp, ys = x_prompt, x_sample
    outs = [[] for _ in range(8)]
    for i in range(depth):
        cast = lambda a: a[i].astype(bf16)
        w = (g_ffn1_pre[i], g_ffn1_post[i], cast(w_ffn1_gate), cast(w_ffn1_up), cast(w_ffn1_down),
             g_mix_pre[i], g_mix_post[i], cast(w_in), cast(w_branch_a), cast(w_branch_b), cast(w_out),
             rwkv_mu[i], rwkv_w0[i], rwkv_w2[i], rwkv_a0[i], rwkv_a2[i], rwkv_g2[i], rwkv_k_k[i],
             rwkv_k_a[i], rwkv_r_k[i], rwkv_ln_w[i], rwkv_ln_b[i],
             g_ffn2_pre[i], g_ffn2_post[i], cast(w_ffn2_gate), cast(w_ffn2_up), cast(w_ffn2_down))
        yp, kp, vp, wkvp, shp = _layer(
            yp, jnp.zeros((bp, rp), x_prompt.dtype),
            jnp.zeros((bp, rw // HEAD_DIM, HEAD_DIM, HEAD_DIM), state_wkv.dtype), None, w, n_heads)
        ys, ksn, vsn, wkvs, shs = _layer(
            ys, state_shift[i], state_wkv[i], (cache_k[i], cache_v[i], page_table), w, n_heads)
        for lst, val in zip(outs, (kp, vp, ksn, vsn, wkvp, shp, wkvs, shs)):
            lst.append(val)
    return (yp, ys) + tuple(jnp.stack(lst) for lst in outs)
```

```python
import functools
import math

import jax
import jax.numpy as jnp
import numpy as np
from jax import lax
from jax.experimental import pallas as pl
from jax.experimental.pallas import tpu as pltpu

HEAD_DIM = 64
LANES = 128
HEADS_PER_TILE = LANES // HEAD_DIM
MXU_DIM = 256
MOBA_BLOCK = 256
MOBA_TOPK = 3
SUM_ROWS = 16
PIPE_UNROLL = 4
RWKV_CHUNK = 64
MOBA_GROUP_HEADS = 4
RMS_EPS = 1e-6
GN_EPS = 64e-5
VMEM_LIMIT = 56 * 1024 * 1024
NEG_INF = float("-inf")
POS_INF = float("inf")
LOG2E = 1.4426950408889634

f32 = jnp.float32
bf16 = jnp.bfloat16


def _params(*sem):
    return pltpu.CompilerParams(dimension_semantics=sem, vmem_limit_bytes=VMEM_LIMIT)


def _const_spec(shape):
    nd = len(shape)
    return pl.BlockSpec(shape, lambda *_: (0,) * nd, pipeline_mode=pl.Buffered(1))


def _rms(x, g):
    return x * lax.rsqrt(jnp.mean(x * x, axis=-1, keepdims=True) + RMS_EPS) * g


def _split3(x):
    hi = x.astype(bf16)
    r1 = x - hi.astype(f32)
    mid = r1.astype(bf16)
    lo = (r1 - mid.astype(f32)).astype(bf16)
    return hi, mid, lo


_NN = (((1,), (0,)), ((), ()))
_NT = (((1,), (1,)), ((), ()))
_TN = (((0,), (0,)), ((), ()))


def _dg(a, b, dims):
    return lax.dot_general(a, b, dims, preferred_element_type=f32)


def _dot_lo(a, b, dims=_NN):
    return _dg(a.astype(bf16), b.astype(bf16), dims)


def _dot_hi(a, b, dims=_NN):
    ah, am, _ = _split3(a)
    bh, bm, _ = _split3(b)
    return _dg(ah, bh, dims) + (_dg(ah, bm, dims) + _dg(am, bh, dims))


def _dot_exact_lhs(a_bf, b, dims=_NN):
    bh, bm, bl = _split3(b)
    return _dg(a_bf, bh, dims) + (_dg(a_bf, bm, dims) + _dg(a_bf, bl, dims))


def _dot_exact_rhs(a, b_bf, dims=_NN):
    ah = a.astype(bf16)
    am = (a - ah.astype(f32)).astype(bf16)
    return _dg(ah, b_bf, dims) + _dg(am, b_bf, dims)


def _head_sums(x, ones_bf):
    g = ones_bf.shape[0]
    return jnp.concatenate([_dot_exact_rhs(x[:, c:c + g], ones_bf) for c in range(0, x.shape[1], g)],
                           axis=1)


def _ffn_kernel(x_ref, gpre_ref, gpost_ref, wg_ref, wu_ref, wd_ref, o_ref):
    x = x_ref[...]
    h = _rms(x, gpre_ref[...]).astype(bf16)
    gate = jnp.dot(h, wg_ref[...], preferred_element_type=f32)
    up = jnp.dot(h, wu_ref[...], preferred_element_type=f32)
    act = (gate * jax.nn.sigmoid(gate) * up).astype(bf16)
    y = jnp.dot(act, wd_ref[...], preferred_element_type=f32)
    o_ref[...] = x + 0.5 * _rms(y, gpost_ref[...])


def _ffn(x, g_pre, g_post, wg, wu, wd, tm):
    n, d = x.shape
    dff = wg.shape[1]
    return pl.pallas_call(
        _ffn_kernel,
        grid=(n // tm,),
        in_specs=[
            pl.BlockSpec((tm, d), lambda i: (i, 0)),
            _const_spec((1, d)), _const_spec((1, d)),
            _const_spec((d, dff)), _const_spec((d, dff)), _const_spec((dff, d)),
        ],
        out_specs=pl.BlockSpec((tm, d), lambda i: (i, 0)),
        out_shape=jax.ShapeDtypeStruct((n, d), f32),
        compiler_params=_params("parallel"),
        name="ffn",
    )(x, g_pre, g_post, wg, wu, wd)


def _in_proj_kernel(splits, h_ref, g_ref, w_ref, q_ref, p_ref, ga_ref, gb_ref, k_ref, v_ref):
    u = _rms(h_ref[...], g_ref[...]).astype(bf16)
    proj = lambda idx: _dg(u, w_ref[:, splits[idx]:splits[idx + 1]], _NN)
    q_ref[...] = proj(0)
    k_ref[...] = proj(1)
    v_ref[...] = proj(2)
    p_ref[...] = proj(3)
    ga_ref[...] = jax.nn.sigmoid(proj(4))
    gb_ref[...] = jax.nn.sigmoid(proj(5))


def _in_proj_seq_kernel(splits, blocks_per_seq, rwkv_widths, h_ref, g_ref, w_ref, prev_ref, mu_ref, *refs):
    const_refs, out_refs, carry = refs[:8], refs[8:-1], refs[-1]
    q_ref, p_ref, ga_ref, gb_ref, kt_ref, vt_ref, kb_ref, vtb_ref, km_ref = out_refs[:9]
    @pl.when(pl.program_id(0) % blocks_per_seq == 0)
    def _():
        carry[...] = prev_ref[0]

    u = _rms(h_ref[...], g_ref[...]).astype(bf16)
    proj = lambda idx: _dg(u, w_ref[:, splits[idx]:splits[idx + 1]], _NN)
    p = proj(3)
    p_ref[...] = p
    rows = p.shape[0]
    first = lax.broadcasted_iota(jnp.int32, (rows, 1), 0) == 0
    shifted = jnp.where(first, carry[...], pltpu.roll(p, 1, axis=0))
    carry[...] = p[rows - 1:rows, :]
    feats = _rwkv_features(p + (shifted - p) * mu_ref[...], rwkv_widths, *const_refs)
    for o_ref, x in zip(out_refs[9:], feats):
        o_ref[...] = x

    q_ref[...] = proj(0)
    ga_ref[...] = jax.nn.sigmoid(proj(4))
    gb_ref[...] = jax.nn.sigmoid(proj(5))
    k, v = proj(1), proj(2)
    v_t = v.T
    kt_ref[0] = k.T
    vt_ref[0] = v_t
    kb_ref[...] = k.astype(bf16)
    vtb_ref[0] = v_t.astype(bf16)
    km_ref[0] = jnp.mean(k, axis=0, keepdims=True)


def _proj_splits(widths, w_in):
    splits = tuple(int(s) for s in np.concatenate([[0], np.cumsum(widths)]))
    assert splits[-1] == w_in.shape[1] and all(s % LANES == 0 for s in splits)
    return splits


def _in_proj(h, g, w_in, widths, tm):
    n, d = h.shape
    order = (0, 3, 4, 5, 1, 2)
    return pl.pallas_call(
        functools.partial(_in_proj_kernel, _proj_splits(widths, w_in)),
        grid=(n // tm,),
        in_specs=[
            pl.BlockSpec((tm, d), lambda i: (i, 0)),
            _const_spec((1, d)),
            _const_spec(w_in.shape),
        ],
        out_specs=[pl.BlockSpec((tm, widths[j]), lambda i: (i, 0)) for j in order],
        out_shape=[jax.ShapeDtypeStruct((n, widths[j]), f32) for j in order],
        compiler_params=_params("parallel"),
        name="in_proj",
    )(h, g, w_in)


def _in_proj_seq(h, g, w_in, widths, seq, prev, rwkv_weights):
    n, d = h.shape
    tm = MOBA_BLOCK
    assert seq % tm == 0
    nbs = seq // tm
    mw = widths[1]
    rwkv_widths, (mu, *consts) = _rwkv_consts(rwkv_weights)
    rw, rp = rwkv_widths[0], mu.shape[-1]
    row = lambda w, dt=f32: (pl.BlockSpec((tm, w), lambda i: (i, 0)), jax.ShapeDtypeStruct((n, w), dt))
    t_spec = pl.BlockSpec((1, mw, tm), lambda i: (i // nbs, 0, i % nbs))
    t_shape = jax.ShapeDtypeStruct((n // seq, mw, seq), f32)
    outs = [row(widths[0]), row(widths[3]), row(widths[4]), row(widths[5]),
            (t_spec, t_shape), (t_spec, t_shape), row(mw, bf16),
            (pl.BlockSpec((1, mw, tm), lambda i: (i, 0, 0)), jax.ShapeDtypeStruct((n // tm, mw, tm), bf16)),
            (pl.BlockSpec((1, 1, mw), lambda i: (i, 0, 0)), jax.ShapeDtypeStruct((n // tm, 1, mw), f32))]
    outs += [row(rw)] * 7
    return pl.pallas_call(
        functools.partial(_in_proj_seq_kernel, _proj_splits(widths, w_in), nbs, rwkv_widths),
        grid=(n // tm,),
        in_specs=[
            pl.BlockSpec((tm, d), lambda i: (i, 0)),
            _const_spec((1, d)),
            _const_spec(w_in.shape),
            pl.BlockSpec((1, 1, rp), lambda i: (i // nbs, 0, 0)),
            _const_spec(mu.shape),
        ] + [_const_spec(c.shape) for c in consts],
        out_specs=[o[0] for o in outs],
        out_shape=[o[1] for o in outs],
        scratch_shapes=[pltpu.VMEM((1, rp), f32)],
        compiler_params=_params("arbitrary"),
        name="in_proj_seq",
    )(h, g, w_in, prev, mu, *consts)


def _top_blocks(gate, topk):
    blk = lax.broadcasted_iota(jnp.int32, gate.shape, 0).astype(f32)
    sel = jnp.zeros(gate.shape, f32)
    big = float(gate.shape[0])
    for _ in range(topk):
        m = jnp.max(gate, axis=0, keepdims=True)
        cand = (gate == m) & (gate > NEG_INF)
        idx = jnp.min(jnp.where(cand, blk, big), axis=0, keepdims=True)
        hit = blk == idx
        sel = jnp.where(hit, 1.0, sel)
        gate = jnp.where(hit, NEG_INF, gate)
    return sel


def _moba_prompt_kernel(n_heads, q_ref, kb_ref, vt_ref, km_ref, o_ref, off_sc, s_sc):
    group = pl.program_id(1)
    qi = pl.program_id(2)
    tq = q_ref.shape[0]
    nb = km_ref.shape[0]
    heads = q_ref.shape[1] // HEAD_DIM
    tile = lambda j: pl.ds((j // HEADS_PER_TILE) * LANES, LANES)
    lane = lax.broadcasted_iota(jnp.int32, (1, LANES), 1)
    key_i = lax.broadcasted_iota(jnp.int32, (MOBA_BLOCK, tq), 0)
    qry_i = lax.broadcasted_iota(jnp.int32, (MOBA_BLOCK, tq), 1)
    rel = (qry_i - key_i).astype(f32)
    causal = key_i <= qry_i
    blk_row = lax.broadcasted_iota(jnp.int32, (nb, tq), 0)

    own = pl.ds(pl.multiple_of(qi * MOBA_BLOCK, MOBA_BLOCK), MOBA_BLOCK)
    ones_rows = jnp.ones((SUM_ROWS, MOBA_BLOCK), bf16)
    with_sum = lambda vt: jnp.concatenate([vt, ones_rows], axis=0)

    consts, carry = [], []
    for j in range(heads):
        q = q_ref[:, tile(j)]
        head_mask = (lane // HEAD_DIM == j % HEADS_PER_TILE).astype(f32)
        head = jnp.full((1, 1), group * heads + j + 1, jnp.int32).astype(f32)
        slope2 = jnp.exp2(head * (-8.0 / n_heads)) * LOG2E
        gate = _dot_hi(km_ref[:, tile(j)] * head_mask, q, _NT)
        gate = jnp.where(blk_row < qi, gate, NEG_INF)
        off_sc[j] = jnp.where(_top_blocks(gate, MOBA_TOPK) > 0.5, 0.0, POS_INF)
        qs = (q * head_mask * (HEAD_DIM ** -0.5 * LOG2E)).astype(bf16)
        alibi = slope2 * rel

        s = _dg(kb_ref[own, tile(j)], qs, _NT) - alibi
        s = jnp.where(causal, s, NEG_INF)
        m = jnp.max(s, axis=0, keepdims=True)
        p = jnp.exp2(s - m).astype(bf16)
        acc = _dg(with_sum(vt_ref[qi, tile(j), :]), p, _NN)
        consts.append((qs, alibi, slope2))
        carry += [m, acc]

    def score(n, slot):
        nc = jnp.minimum(n, nb - 1)
        rows = pl.ds(pl.multiple_of(nc * MOBA_BLOCK, MOBA_BLOCK), MOBA_BLOCK)
        raw = []
        for j in range(heads):
            s = _dg(kb_ref[rows, tile(j)], consts[j][0], _NT) - consts[j][1]
            s_sc[slot, j] = s
            raw.append(jnp.max(s, axis=0, keepdims=True))
        return raw

    def attend(n, slot, raw, stats):
        nc = jnp.minimum(n, nb - 1)
        out = []
        for j in range(heads):
            slope2 = consts[j][2]
            m, acc = stats[2 * j:2 * j + 2]
            off = jnp.where(n < qi, off_sc[j, pl.ds(nc, 1), :], POS_INF)
            far = ((qi - n) * MOBA_BLOCK).astype(f32) * slope2
            m_new = jnp.maximum(m, raw[j] - far - off)
            alpha = jnp.exp2(m - m_new)
            p = jnp.exp2(s_sc[slot, j] - (m_new + far + off)).astype(bf16)
            out += [m_new, acc * alpha + _dg(with_sum(vt_ref[nc, tile(j), :]), p, _NN)]
        return out

    def body(i, carry):
        stats, raw = list(carry[:-heads]), list(carry[-heads:])
        for u in range(PIPE_UNROLL):
            n = PIPE_UNROLL * i + u
            raw_next = score(n + 1, (u + 1) % 2)
            stats = attend(n, u % 2, raw, stats)
            raw = raw_next
        return tuple(stats + raw)

    trips = (qi + PIPE_UNROLL - 1) // PIPE_UNROLL
    carry = lax.fori_loop(0, trips, body, tuple(carry + score(0, 0)))
    accs = [carry[2 * j + 1] for j in range(heads)]
    in_tile = lambda j: slice((j % HEADS_PER_TILE) * HEAD_DIM, (j % HEADS_PER_TILE + 1) * HEAD_DIM)
    out_t = jnp.concatenate([accs[j][in_tile(j)] / accs[j][LANES:LANES + 1] for j in range(heads)], axis=0)
    o_ref[...] = out_t.T


def _moba_prompt(q, k_bf, v_t, kmean, batch, seq, n_heads):
    n, w = q.shape
    wg = MOBA_GROUP_HEADS * HEAD_DIM
    assert w % wg == 0 and wg % LANES == 0
    nb = seq // MOBA_BLOCK
    tq = MOBA_BLOCK
    return pl.pallas_call(
        functools.partial(_moba_prompt_kernel, n_heads),
        grid=(batch, w // wg, nb),
        in_specs=[
            pl.BlockSpec((tq, wg), lambda b, p, i: (b * nb + i, p)),
            pl.BlockSpec((seq, wg), lambda b, p, i: (b, p)),
            pl.BlockSpec((nb, wg, MOBA_BLOCK), lambda b, p, i: (b, p, 0)),
            pl.BlockSpec((nb, wg), lambda b, p, i: (b, p)),
        ],
        out_specs=pl.BlockSpec((tq, wg), lambda b, p, i: (b * nb + i, p)),
        out_shape=jax.ShapeDtypeStruct((n, w), f32),
        scratch_shapes=[pltpu.VMEM((MOBA_GROUP_HEADS, nb, tq), f32),
                        pltpu.VMEM((2, MOBA_GROUP_HEADS, MOBA_BLOCK, tq), f32)],
        compiler_params=_params("parallel", "parallel", "arbitrary"),
        name="moba_prompt",
    )(q, k_bf, v_t, kmean)


def _moba_sample_kernel(t_new, n_blocks, n_heads, pt_ref, q_ref, kn_ref, vn_ref, *refs):
    del pt_ref
    k_refs, v_refs, o_ref = refs[:2 * n_blocks], refs[2 * n_blocks:4 * n_blocks], refs[4 * n_blocks]
    rows, width = q_ref.shape[1], q_ref.shape[2]
    page = k_refs[0].shape[3]
    head_bits = n_heads.bit_length() - 1
    r_idx = lax.broadcasted_iota(jnp.int32, (rows, 1), 0)
    head = r_idx & (n_heads - 1)
    t_q = r_idx >> head_bits
    slope = jnp.exp2((head + 1).astype(f32) * (-8.0 / n_heads))
    qs = (q_ref[0] * (HEAD_DIM ** -0.5)).astype(bf16)
    tok = lax.broadcasted_iota(jnp.int32, (1, page), 1)
    past_len = n_blocks * MOBA_BLOCK
    flat = lambda ref: ref[0].reshape(width, page).astype(bf16)

    blocks, pages = range(n_blocks), range(2 * n_blocks)
    qk = [_dg(qs, flat(k_refs[i]), _NN) for i in pages]
    s = [qk[i] - slope * (past_len - i * page + t_q - tok).astype(f32) for i in pages]
    row_max = [jnp.max(s[i], axis=-1, keepdims=True) for i in pages]
    m_blk = [jnp.maximum(row_max[2 * n], row_max[2 * n + 1]) for n in blocks]
    p = [jnp.exp(s[i] - m_blk[i // 2]) for i in pages]
    row_sum = [jnp.sum(p[i], axis=-1, keepdims=True) for i in pages]
    l_blk = [row_sum[2 * n] + row_sum[2 * n + 1] for n in blocks]
    pv = [_dg(p[i].astype(bf16), flat(v_refs[i]), _NT) for i in pages]
    acc_blk = [pv[2 * n] + pv[2 * n + 1] for n in blocks]
    qk_sum = [jnp.sum(qk[i], axis=-1, keepdims=True) for i in pages]
    gates = [qk_sum[2 * n] + qk_sum[2 * n + 1] for n in blocks]

    t_k = lax.broadcasted_iota(jnp.int32, (1, kn_ref.shape[1]), 1)
    s_own = _dg(qs, kn_ref[0].astype(bf16), _NT) - slope * (t_q - t_k).astype(f32)
    s_own = jnp.where((t_k <= t_q) & (t_k < t_new), s_own, NEG_INF)
    m_tot = jnp.max(s_own, axis=-1, keepdims=True)
    chosen = []
    for n in range(n_blocks):
        rank = jnp.zeros_like(gates[n])
        for i in range(n_blocks):
            if i == n:
                continue
            ahead = (gates[i] > gates[n]) | ((gates[i] == gates[n]) & (i < n))
            rank = rank + ahead.astype(f32)
        chosen.append(rank < MOBA_TOPK)
    for n in range(n_blocks):
        m_tot = jnp.maximum(m_tot, jnp.where(chosen[n], m_blk[n], NEG_INF))
    p_own = jnp.exp(s_own - m_tot)
    l_tot = jnp.sum(p_own, axis=-1, keepdims=True)
    acc_tot = _dg(p_own.astype(bf16), vn_ref[0].astype(bf16), _NN)
    for n in range(n_blocks):
        wgt = jnp.where(chosen[n], jnp.exp(m_blk[n] - m_tot), 0.0)
        l_tot = l_tot + wgt * l_blk[n]
        acc_tot = acc_tot + wgt * acc_blk[n]
    own_head = (lax.broadcasted_iota(jnp.int32, (1, width), 1) // HEAD_DIM) == head
    out = jnp.where(own_head, acc_tot / l_tot, 0.0)
    o_ref[0] = jnp.sum(out.reshape(rows // n_heads, n_heads, width), axis=1)


def _moba_sample(q, k_new, v_new, cache_k, cache_v, page_table):
    batch, t_new, width = q.shape
    n_pool, page, n_heads, dim = cache_k.shape
    n_pages = page_table.shape[1]
    key_rows = 8
    assert dim == HEAD_DIM and width == n_heads * dim and MOBA_BLOCK == 2 * page and n_pages % 2 == 0
    assert t_new <= key_rows and n_heads & (n_heads - 1) == 0 and (t_new * n_heads) % 8 == 0
    n_blocks = n_pages // 2
    assert n_blocks >= MOBA_TOPK
    cache_kt = jnp.transpose(cache_k, (0, 2, 3, 1))
    cache_vt = jnp.transpose(cache_v, (0, 2, 3, 1))
    head_lanes = (jnp.arange(width)[None, :] // dim == jnp.arange(n_heads)[:, None]).astype(q.dtype)
    q_rows = (q[:, :, None, :] * head_lanes).reshape(batch, t_new * n_heads, width)
    pad_keys = lambda a: jnp.pad(a, ((0, 0), (0, key_rows - t_new), (0, 0)))

    def page_spec(i):
        return pl.BlockSpec((1, n_heads, dim, page), lambda b, pt: (pt[b, i], 0, 0, 0))

    pages = [page_spec(i) for i in range(n_pages)]
    rows_spec = lambda r: pl.BlockSpec((1, r, width), lambda b, pt: (b, 0, 0))
    return pl.pallas_call(
        functools.partial(_moba_sample_kernel, t_new, n_blocks, n_heads),
        grid_spec=pltpu.PrefetchScalarGridSpec(
            num_scalar_prefetch=1,
            grid=(batch,),
            in_specs=[rows_spec(t_new * n_heads), rows_spec(key_rows), rows_spec(key_rows)] + pages + pages,
            out_specs=rows_spec(t_new),
        ),
        out_shape=jax.ShapeDtypeStruct((batch, t_new, width), f32),
        compiler_params=_params("parallel"),
        name="moba_sample",
    )(page_table, q_rows, pad_keys(k_new), pad_keys(v_new), *([cache_kt] * n_pages), *([cache_vt] * n_pages))


def _rwkv_features(m, widths, w0_ref, w2_ref, a0_ref, a2_ref, g2_ref, kk_ref, ka_ref, ones_ref):
    rw, dl, al, gl = widths
    r = m[:, 0:rw]
    k = m[:, rw:2 * rw]
    v = m[:, 2 * rw:3 * rw]
    lora = m[:, 3 * rw:3 * rw + dl + al]
    gd = m[:, 3 * rw + dl + al:3 * rw + dl + al + gl]

    z = -(w0_ref[...] + _dot_hi(jnp.tanh(lora), w2_ref[...]))
    softplus = jnp.maximum(z, 0.0) + jnp.log(1.0 + jnp.exp(-jnp.abs(z)))
    w = -softplus - 0.5
    log_decay = -jnp.exp(w)
    a = jax.nn.sigmoid(a0_ref[...] + _dot_hi(lora, a2_ref[...]))
    g = _dot_hi(jax.nn.sigmoid(gd), g2_ref[...])
    kk = k * kk_ref[...]
    norm = jnp.sqrt(_head_sums(kk * kk, ones_ref[...]))
    kk = kk / jnp.maximum(norm, 1e-12)
    kh = k * (1.0 + (a - 1.0) * ka_ref[...])
    return [r, kh, v, kk, kk * a, log_decay, g]


def _rwkv_pre_kernel(t_valid, t_pad, widths, p_ref, prev_ref, mu_ref, *refs):
    const_refs, out_refs, carry = refs[:8], refs[8:15], refs[15]
    ti = pl.program_id(1)
    bt, tt, width = p_ref.shape
    rw = widths[0]

    @pl.when(ti == 0)
    def _():
        carry[...] = prev_ref[...]

    p = p_ref[...]
    t_idx = lax.broadcasted_iota(jnp.int32, (bt, tt, 1), 1)
    shifted = jnp.where(t_idx == 0, carry[...], pltpu.roll(p, 1, axis=1))
    carry[...] = p[:, tt - 1:tt, :]
    m = (p + (shifted - p) * mu_ref[...]).reshape(bt * tt, width)
    *outs, g = _rwkv_features(m, widths, *const_refs)
    if t_valid < t_pad:
        live = (lax.broadcasted_iota(jnp.int32, (bt, tt, 1), 1) + ti * tt < t_valid)
        live = live.astype(f32).reshape(bt * tt, 1)
        outs = [x * live for x in outs]
    shape3 = (bt, tt, rw)
    for o_ref, x in zip(out_refs, outs + [g]):
        o_ref[...] = x.reshape(shape3)


def _head_ones():
    idx = np.arange(MXU_DIM) // HEAD_DIM
    return jnp.asarray(idx[:, None] == idx[None, :], dtype=bf16)


def _rwkv_consts(weights):
    mu, w0, w2, a0, a2, g2, k_k, k_a = weights
    rw = w0.shape[-1]
    widths = (rw, w2.shape[0], a2.shape[0], g2.shape[0])
    assert 3 * rw + sum(widths[1:]) == mu.shape[-1]
    assert widths[1] + widths[2] == LANES and widths[3] % LANES == 0 and rw % MXU_DIM == 0
    w2p = jnp.pad(w2, ((0, widths[2]), (0, 0)))
    a2p = jnp.pad(a2, ((widths[1], 0), (0, 0)))
    return widths, (mu, w0, w2p, a0, a2p, g2, k_k, k_a, _head_ones())


def _rwkv_pre(p, prev, t_valid, weights, bt, tt):
    batch, t_pad, width = p.shape
    widths, consts = _rwkv_consts(weights)
    rw = widths[0]
    out_spec = pl.BlockSpec((bt, tt, rw), lambda b, t: (b, t, 0))
    return pl.pallas_call(
        functools.partial(_rwkv_pre_kernel, t_valid, t_pad, widths),
        grid=(batch // bt, t_pad // tt),
        in_specs=[pl.BlockSpec((bt, tt, width), lambda b, t: (b, t, 0)),
                  pl.BlockSpec((bt, 1, width), lambda b, t: (b, 0, 0))]
                 + [_const_spec(c.shape) for c in consts],
        out_specs=[out_spec] * 7,
        out_shape=[jax.ShapeDtypeStruct((batch, t_pad, rw), f32)] * 7,
        scratch_shapes=[pltpu.VMEM((bt, 1, width), f32)],
        compiler_params=_params("parallel", "arbitrary"),
        name="rwkv_pre",
    )(p, prev, *consts)


def _rwkv_scan_kernel(r_ref, k_ref, v_ref, kk_ref, b_ref, lw_ref, s0_ref, o_ref, s_out, state):
    ci = pl.program_id(1)
    bt, c, width = r_ref.shape
    n_pairs = width // LANES
    c2 = HEADS_PER_TILE * c
    n_double = int(math.log2(c))
    assert 1 << n_double == c and HEADS_PER_TILE == 2

    @pl.when(ci == 0)
    def _():
        state[...] = s0_ref[...]

    ri = lax.broadcasted_iota(jnp.int32, (c, c), 0)
    cj = lax.broadcasted_iota(jnp.int32, (c, c), 1)
    tri_bf = (cj <= ri).astype(bf16)
    r2 = lax.broadcasted_iota(jnp.int32, (c2, c2), 0)
    q2 = lax.broadcasted_iota(jnp.int32, (c2, c2), 1)
    same_blk = (r2 >= c) == (q2 >= c)
    strict2 = (same_blk & (q2 < r2)).astype(f32)
    incl2 = (same_blk & (q2 <= r2)).astype(f32)
    eye2 = (q2 == r2).astype(f32)
    lane = lax.broadcasted_iota(jnp.int32, (1, LANES), 1)
    hm = [(lane // HEAD_DIM == j).astype(f32) for j in range(HEADS_PER_TILE)]
    bi_ = lax.broadcasted_iota(jnp.int32, (LANES, LANES), 0) // HEAD_DIM
    bj_ = lax.broadcasted_iota(jnp.int32, (LANES, LANES), 1) // HEAD_DIM
    block_diag = (bi_ == bj_).astype(f32)

    def stack(x):
        return jnp.concatenate([x * hm[0], x * hm[1]], axis=0)

    def twice(x):
        return jnp.concatenate([x, x], axis=0)

    def unstack(x2):
        return x2[:c] * hm[0] + x2[c:] * hm[1]

    probs = [(bi, pl.ds(pair * LANES, LANES), pair) for bi in range(bt) for pair in range(n_pairs)]
    ld = lambda ref: [ref[bi, :, sl] for bi, sl, _ in probs]
    r, k, v, kk, b, lw = ld(r_ref), ld(k_ref), ld(v_ref), ld(kk_ref), ld(b_ref), ld(lw_ref)
    nprob = len(probs)
    each = range(nprob)

    cum = [_dot_exact_lhs(tri_bf, lw[i]) for i in each]
    cum_end = [cum[i][c - 1:c, :] for i in each]
    e_neg = [jnp.exp(-cum[i]) for i in each]
    e_end = [jnp.exp(cum_end[i] - cum[i]) for i in each]
    a_mat = [kk[i] * jnp.exp(cum[i] - lw[i]) for i in each]
    p_mat = [r[i] * jnp.exp(cum[i]) for i in each]
    xa = [stack(a_mat[i]) for i in each]
    xap = [jnp.concatenate([xa[i], stack(p_mat[i])], axis=0) for i in each]
    yb = [stack(b[i] * e_neg[i]) for i in each]
    yk = [stack(k[i] * e_neg[i]) for i in each]
    v2 = [twice(v[i]) for i in each]

    gb = [_dot_lo(xap[i], yb[i], _NT) for i in each]
    gk = [_dot_lo(xap[i], yk[i], _NT) for i in each]
    l_mat = [gb[i][:c2] * strict2 for i in each]
    m_mat = [gk[i][:c2] * strict2 for i in each]
    lr_mat = [gb[i][c2:] * incl2 for i in each]
    mr_mat = [gk[i][c2:] * incl2 for i in each]

    npow = [-l_mat[i] for i in each]
    t_inv = [eye2 + npow[i] for i in each]
    for _ in range(n_double - 1):
        npow = [_dot_lo(npow[i], npow[i]) for i in each]
        t_inv = [t_inv[i] + _dot_lo(npow[i], t_inv[i]) for i in each]
    mv = [_dot_lo(m_mat[i], v2[i]) for i in each]
    tz = [_dot_lo(t_inv[i], jnp.concatenate([xa[i], mv[i]], axis=1)) for i in each]
    ta = [tz[i][:c, :LANES] + tz[i][c:, :LANES] for i in each]
    tmv = [unstack(tz[i][:, LANES:]) for i in each]

    s_prev = [state[bi, pair] for bi, _, pair in probs]
    xs = [_dot_lo(jnp.concatenate([ta[i], p_mat[i]], axis=0), s_prev[i], _NT) for i in each]
    u = [-(xs[i][:c] + tmv[i]) for i in each]
    corr = [_dot_lo(lr_mat[i], twice(u[i])) + _dot_lo(mr_mat[i], v2[i]) for i in each]
    upd = [_dot_lo(jnp.concatenate([u[i], v[i]], axis=0),
                   jnp.concatenate([b[i] * e_end[i], k[i] * e_end[i]], axis=0), _TN) for i in each]
    for i, (bi, sl, pair) in enumerate(probs):
        state[bi, pair] = s_prev[i] * jnp.exp(cum_end[i]) + upd[i] * block_diag
        o_ref[bi, :, sl] = xs[i][c:] + unstack(corr[i])

    @pl.when(ci == pl.num_programs(1) - 1)
    def _():
        s_out[...] = state[...]


def _rwkv_scan(r, k, v, kk, b, lw, s0, bt, chunk):
    batch, t_pad, width = r.shape
    n_pairs = width // LANES
    seq_spec = pl.BlockSpec((bt, chunk, width), lambda bb, c: (bb, c, 0))
    st_spec = pl.BlockSpec((bt, n_pairs, LANES, LANES), lambda bb, c: (bb, 0, 0, 0))
    return pl.pallas_call(
        _rwkv_scan_kernel,
        grid=(batch // bt, t_pad // chunk),
        in_specs=[seq_spec] * 6 + [st_spec],
        out_specs=[seq_spec, st_spec],
        out_shape=[jax.ShapeDtypeStruct((batch, t_pad, width), f32),
                   jax.ShapeDtypeStruct((batch, n_pairs, LANES, LANES), f32)],
        scratch_shapes=[pltpu.VMEM((bt, n_pairs, LANES, LANES), f32)],
        compiler_params=_params("parallel", "arbitrary"),
        name="rwkv_scan",
    )(r, k, v, kk, b, lw, s0)


def _pair_state(s):
    bsz, h, n, _ = s.shape
    s = s.reshape(bsz, h // HEADS_PER_TILE, HEADS_PER_TILE, n, n)
    eye = jnp.eye(HEADS_PER_TILE, dtype=s.dtype)
    out = jnp.einsum("bpjvk,ji->bpjvik", s, eye)
    return out.reshape(bsz, h // HEADS_PER_TILE, LANES, LANES)


def _unpair_state(s, heads):
    bsz = s.shape[0]
    s = s.reshape(bsz, heads // HEADS_PER_TILE, HEADS_PER_TILE, HEAD_DIM, HEADS_PER_TILE, HEAD_DIM)
    diag = jnp.stack([s[:, :, j, :, j, :] for j in range(HEADS_PER_TILE)], axis=2)
    return diag.reshape(bsz, heads, HEAD_DIM, HEAD_DIM)


STATE_ROWS = 8


def _rwkv_step_kernel(r_ref, k_ref, v_ref, kk_ref, b_ref, lw_ref, s_ref, o_ref, s_out):
    n_tok = r_ref.shape[0]
    n_val = s_ref.shape[1]
    decay = [jnp.exp(lw_ref[t, 0]) for t in range(n_tok)]

    def rows(c, _):
        base = pl.multiple_of(c * STATE_ROWS, STATE_ROWS)
        state = [s_ref[0, base + i] for i in range(STATE_ROWS)]
        for t in range(n_tok):
            kk, b, k, r = kk_ref[t, 0], b_ref[t, 0], k_ref[t, 0], r_ref[t, 0]
            v_rows = v_ref[t, 0, pl.ds(base, STATE_ROWS), :]
            out = []
            for i in range(STATE_ROWS):
                s_kk = jnp.sum(state[i] * kk, axis=0, keepdims=True)
                state[i] = state[i] * decay[t] - s_kk * b + v_rows[i:i + 1] * k
                out.append(jnp.sum(state[i] * r, axis=0, keepdims=True))
            o_ref[t, 0, pl.ds(base, STATE_ROWS), :] = jnp.concatenate(out, axis=0)
        for i in range(STATE_ROWS):
            s_out[0, base + i] = state[i]
        return 0

    lax.fori_loop(0, n_val // STATE_ROWS, rows, 0)


def _rwkv_step(r, k, v, kk, b, lw, s0):
    n_tok, n_heads, dim, batch = r.shape
    assert s0.shape == (n_heads, dim, dim, batch) and dim % STATE_ROWS == 0
    seq_spec = pl.BlockSpec((n_tok, 1, dim, batch), lambda h: (0, h, 0, 0))
    st_spec = pl.BlockSpec((1, dim, dim, batch), lambda h: (h, 0, 0, 0))
    return pl.pallas_call(
        _rwkv_step_kernel,
        grid=(n_heads,),
        in_specs=[seq_spec] * 6 + [st_spec],
        out_specs=[seq_spec, st_spec],
        out_shape=[jax.ShapeDtypeStruct(r.shape, f32), jax.ShapeDtypeStruct(s0.shape, f32)],
        compiler_params=_params("parallel"),
        name="rwkv_step",
    )(r, k, v, kk, b, lw, s0)


def _merge_kernel(h_ref, oa_ref, o_ref, r_ref, k_ref, v_ref, g_ref, ga_ref, gb_ref,
                  rk_ref, lnw_ref, lnb_ref, ones_ref, wba_ref, wbb_ref, wout_ref, gpost_ref,
                  out_ref):
    ones = ones_ref[...]
    inv = 1.0 / HEAD_DIM
    o = o_ref[...]
    mean = _head_sums(o, ones) * inv
    d = o - mean
    var = _head_sums(d * d, ones) * inv
    on = d * lax.rsqrt(var + GN_EPS) * lnw_ref[...] + lnb_ref[...]
    bonus = _head_sums(r_ref[...] * k_ref[...] * rk_ref[...], ones) * v_ref[...]
    ob = (on + bonus) * g_ref[...]
    merged = (ga_ref[...] * jnp.dot(oa_ref[...].astype(bf16), wba_ref[...], preferred_element_type=f32)
              + gb_ref[...] * jnp.dot(ob.astype(bf16), wbb_ref[...], preferred_element_type=f32))
    y = jnp.dot(merged.astype(bf16), wout_ref[...], preferred_element_type=f32)
    out_ref[...] = h_ref[...] + _rms(y, gpost_ref[...])


def _merge(h, oa, o, r, k, v, g, ga, gb, consts, tm):
    n, d = h.shape
    w = oa.shape[1]
    row = lambda width: pl.BlockSpec((tm, width), lambda i: (i, 0))
    return pl.pallas_call(
        _merge_kernel,
        grid=(n // tm,),
        in_specs=[row(d)] + [row(w)] * 6 + [row(d)] * 2 + [_const_spec(c.shape) for c in consts],
        out_specs=row(d),
        out_shape=jax.ShapeDtypeStruct((n, d), f32),
        compiler_params=_params("parallel"),
        name="merge",
    )(h, oa, o, r, k, v, g, ga, gb, *consts)


def _row_tile(n, target):
    t = min(n, target)
    assert n % t == 0
    return t


def _layer(x, prev_shift, wkv0, past, w, n_heads):
    (g1a, g1b, f1g, f1u, f1d, gma, gmb, w_in, w_ba, w_bb, w_out,
     mu, w0, w2, a0, a2, g2, k_k, k_a, r_k, ln_w, ln_b,
     g2a, g2b, f2g, f2u, f2d) = w
    batch, t, d = x.shape
    n = batch * t
    mw = w_ba.shape[0]
    rw = w_bb.shape[0]
    rp = mu.shape[-1]
    widths = (mw, mw, mw, rp, d, d)
    row = lambda a: a.reshape(1, -1)

    x2 = x.reshape(n, d)
    h = _ffn(x2, row(g1a), row(g1b), f1g, f1u, f1d, _row_tile(n, 512))
    kv_shape = (batch, t, n_heads, HEAD_DIM)
    pre_w = (row(mu), row(w0), w2, row(a0), a2, g2, row(k_k), row(k_a))
    prev = prev_shift.reshape(batch, 1, rp)
    if past is None:
        assert t % MOBA_BLOCK == 0 and t % RWKV_CHUNK == 0
        q, p_rw, ga, gb, k_t, v_t, k_bf, vt_bf, kmean, *pre = _in_proj_seq(
            h, row(gma), w_in, widths, t, prev, pre_w)
        o_a = _moba_prompt(q, k_bf, vt_bf, kmean.reshape(n // MOBA_BLOCK, mw), batch, t, n_heads)
        k, v = (jnp.transpose(a.reshape(batch, n_heads, HEAD_DIM, t), (0, 3, 1, 2)) for a in (k_t, v_t))
        r_, kh, v_, g_ = (pre[i] for i in (0, 1, 2, 6))
        o_scan, s_fin = _rwkv_scan(*(a.reshape(batch, t, rw) for a in pre[:6]),
                                   _pair_state(wkv0.astype(f32)), min(batch, 2), RWKV_CHUNK)
        o_scan = o_scan.reshape(n, rw)
        wkv = _unpair_state(s_fin, rw // HEAD_DIM).astype(wkv0.dtype)
    else:
        cache_k, cache_v, page_table = past
        q, p_rw, ga, gb, k, v = _in_proj(h, row(gma), w_in, widths, _row_tile(n, MOBA_BLOCK))
        rows3 = lambda a: a.reshape(batch, t, mw)
        o_a = _moba_sample(rows3(q), rows3(k), rows3(v), cache_k, cache_v, page_table).reshape(n, mw)
        k, v = k.reshape(kv_shape), v.reshape(kv_shape)
        t_pad = -(-t // 8) * 8
        p3p = jnp.pad(p_rw.reshape(batch, t, rp), ((0, 0), (0, t_pad - t), (0, 0)))
        pre = _rwkv_pre(p3p, prev, t, pre_w, min(batch, 16), t_pad)
        r_, kh, v_, g_ = (pre[i][:, :t].reshape(n, rw) for i in (0, 1, 2, 6))
        heads = rw // HEAD_DIM
        lanes_last = lambda a: jnp.transpose(a[:, :t], (1, 2, 0)).reshape(t, heads, HEAD_DIM, batch)
        o_scan, s_fin = _rwkv_step(*(lanes_last(a) for a in pre[:6]),
                                   jnp.transpose(wkv0.astype(f32), (1, 2, 3, 0)))
        o_scan = jnp.transpose(o_scan.reshape(t, rw, batch), (2, 0, 1)).reshape(n, rw)
        wkv = jnp.transpose(s_fin, (3, 0, 1, 2)).astype(wkv0.dtype)
    p3 = p_rw.reshape(batch, t, rp)
    consts = (row(r_k), row(ln_w), row(ln_b), _head_ones(), w_ba, w_bb, w_out, row(gmb))
    h2 = _merge(h, o_a, o_scan, r_, kh, v_, g_, ga, gb, consts, _row_tile(n, 256))
    y = _ffn(h2, row(g2a), row(g2b), f2g, f2u, f2d, _row_tile(n, 512))
    return y.reshape(batch, t, d), k, v, wkv, p3[:, -1]


def kernel(x_prompt, x_sample, cache_k, cache_v, state_wkv, state_shift, page_table,
           g_ffn1_pre, g_ffn1_post, w_ffn1_gate, w_ffn1_up, w_ffn1_down,
           g_mix_pre, g_mix_post, w_in, w_branch_a, w_branch_b, w_out,
           rwkv_mu, rwkv_w0, rwkv_w2, rwkv_a0, rwkv_a2, rwkv_g2, rwkv_k_k, rwkv_k_a, rwkv_r_k,
           rwkv_ln_w, rwkv_ln_b,
           g_ffn2_pre, g_ffn2_post, w_ffn2_gate, w_ffn2_up, w_ffn2_down):
    depth = w_in.shape[0]
    bp = x_prompt.shape[0]
    n_heads = cache_k.shape[3]
    rp = rwkv_mu.shape[-1]
    rw = rwkv_w0.shape[-1]
    yp, ys = x_prompt, x_sample
    outs = [[] for _ in range(8)]
    for i in range(depth):
        cast = lambda a: a[i].astype(bf16)
        w = (g_ffn1_pre[i], g_ffn1_post[i], cast(w_ffn1_gate), cast(w_ffn1_up), cast(w_ffn1_down),
             g_mix_pre[i], g_mix_post[i], cast(w_in), cast(w_branch_a), cast(w_branch_b), cast(w_out),
             rwkv_mu[i], rwkv_w0[i], rwkv_w2[i], rwkv_a0[i], rwkv_a2[i], rwkv_g2[i], rwkv_k_k[i],
             rwkv_k_a[i], rwkv_r_k[i], rwkv_ln_w[i], rwkv_ln_b[i],
             g_ffn2_pre[i], g_ffn2_post[i], cast(w_ffn2_gate), cast(w_ffn2_up), cast(w_ffn2_down))
        yp, kp, vp, wkvp, shp = _layer(
            yp, jnp.zeros((bp, rp), x_prompt.dtype),
            jnp.zeros((bp, rw // HEAD_DIM, HEAD_DIM, HEAD_DIM), state_wkv.dtype), None, w, n_heads)
        ys, ksn, vsn, wkvs, shs = _layer(
            ys, state_shift[i], state_wkv[i], (cache_k[i], cache_v[i], page_table), w, n_heads)
        for lst, val in zip(outs, (kp, vp, ksn, vsn, wkvp, shp, wkvs, shs)):
            lst.append(val)
    return (yp, ys) + tuple(jnp.stack(lst) for lst in outs)
```

```python
import functools
import math

import jax
import jax.numpy as jnp
import numpy as np
from jax import lax
from jax.experimental import pallas as pl
from jax.experimental.pallas import tpu as pltpu

HEAD_DIM = 64
LANES = 128
HEADS_PER_TILE = LANES // HEAD_DIM
MXU_DIM = 256
MOBA_BLOCK = 256
MOBA_TOPK = 3
SUM_ROWS = 16
PIPE_UNROLL = 4
RWKV_CHUNK = 64
MOBA_GROUP_HEADS = 4
RMS_EPS = 1e-6
GN_EPS = 64e-5
VMEM_LIMIT = 56 * 1024 * 1024
NEG_INF = float("-inf")
POS_INF = float("inf")
LOG2E = 1.4426950408889634

f32 = jnp.float32
bf16 = jnp.bfloat16


def _params(*sem):
    return pltpu.CompilerParams(dimension_semantics=sem, vmem_limit_bytes=VMEM_LIMIT)


def _const_spec(shape):
    nd = len(shape)
    return pl.BlockSpec(shape, lambda *_: (0,) * nd, pipeline_mode=pl.Buffered(1))


def _rms(x, g):
    return x * lax.rsqrt(jnp.mean(x * x, axis=-1, keepdims=True) + RMS_EPS) * g


def _split3(x):
    hi = x.astype(bf16)
    r1 = x - hi.astype(f32)
    mid = r1.astype(bf16)
    lo = (r1 - mid.astype(f32)).astype(bf16)
    return hi, mid, lo


_NN = (((1,), (0,)), ((), ()))
_NT = (((1,), (1,)), ((), ()))
_TN = (((0,), (0,)), ((), ()))


def _dg(a, b, dims):
    return lax.dot_general(a, b, dims, preferred_element_type=f32)


def _dot_lo(a, b, dims=_NN):
    return _dg(a.astype(bf16), b.astype(bf16), dims)


def _dot_hi(a, b, dims=_NN):
    ah, am, _ = _split3(a)
    bh, bm, _ = _split3(b)
    return _dg(ah, bh, dims) + (_dg(ah, bm, dims) + _dg(am, bh, dims))


def _dot_exact_lhs(a_bf, b, dims=_NN):
    bh, bm, bl = _split3(b)
    return _dg(a_bf, bh, dims) + (_dg(a_bf, bm, dims) + _dg(a_bf, bl, dims))


def _dot_exact_rhs(a, b_bf, dims=_NN):
    ah = a.astype(bf16)
    am = (a - ah.astype(f32)).astype(bf16)
    return _dg(ah, b_bf, dims) + _dg(am, b_bf, dims)


def _head_sums(x, ones_bf):
    g = ones_bf.shape[0]
    return jnp.concatenate([_dot_exact_rhs(x[:, c:c + g], ones_bf) for c in range(0, x.shape[1], g)],
                           axis=1)


def _ffn_kernel(x_ref, gpre_ref, gpost_ref, wg_ref, wu_ref, wd_ref, o_ref):
    x = x_ref[...]
    h = _rms(x, gpre_ref[...]).astype(bf16)
    gate = jnp.dot(h, wg_ref[...], preferred_element_type=f32)
    up = jnp.dot(h, wu_ref[...], preferred_element_type=f32)
    act = (gate * jax.nn.sigmoid(gate) * up).astype(bf16)
    y = jnp.dot(act, wd_ref[...], preferred_element_type=f32)
    o_ref[...] = x + 0.5 * _rms(y, gpost_ref[...])


def _ffn(x, g_pre, g_post, wg, wu, wd, tm):
    n, d = x.shape
    dff = wg.shape[1]
    return pl.pallas_call(
        _ffn_kernel,
        grid=(n // tm,),
        in_specs=[
            pl.BlockSpec((tm, d), lambda i: (i, 0)),
            _const_spec((1, d)), _const_spec((1, d)),
            _const_spec((d, dff)), _const_spec((d, dff)), _const_spec((dff, d)),
        ],
        out_specs=pl.BlockSpec((tm, d), lambda i: (i, 0)),
        out_shape=jax.ShapeDtypeStruct((n, d), f32),
        compiler_params=_params("parallel"),
        name="ffn",
    )(x, g_pre, g_post, wg, wu, wd)


def _in_proj_kernel(splits, h_ref, g_ref, w_ref, q_ref, p_ref, ga_ref, gb_ref, k_ref, v_ref):
    u = _rms(h_ref[...], g_ref[...]).astype(bf16)
    proj = lambda idx: _dg(u, w_ref[:, splits[idx]:splits[idx + 1]], _NN)
    q_ref[...] = proj(0)
    k_ref[...] = proj(1)
    v_ref[...] = proj(2)
    p_ref[...] = proj(3)
    ga_ref[...] = jax.nn.sigmoid(proj(4))
    gb_ref[...] = jax.nn.sigmoid(proj(5))


def _in_proj_seq_kernel(splits, blocks_per_seq, rwkv_widths, h_ref, g_ref, w_ref, prev_ref, mu_ref, *refs):
    const_refs, out_refs, carry = refs[:8], refs[8:-1], refs[-1]
    q_ref, p_ref, ga_ref, gb_ref, kt_ref, vt_ref, kb_ref, vtb_ref, km_ref = out_refs[:9]
    @pl.when(pl.program_id(0) % blocks_per_seq == 0)
    def _():
        carry[...] = prev_ref[0]

    u = _rms(h_ref[...], g_ref[...]).astype(bf16)
    proj = lambda idx: _dg(u, w_ref[:, splits[idx]:splits[idx + 1]], _NN)
    p = proj(3)
    p_ref[...] = p
    rows = p.shape[0]
    first = lax.broadcasted_iota(jnp.int32, (rows, 1), 0) == 0
    shifted = jnp.where(first, carry[...], pltpu.roll(p, 1, axis=0))
    carry[...] = p[rows - 1:rows, :]
    feats = _rwkv_features(p + (shifted - p) * mu_ref[...], rwkv_widths, *const_refs)
    for o_ref, x in zip(out_refs[9:], feats):
        o_ref[...] = x

    q_ref[...] = proj(0)
    ga_ref[...] = jax.nn.sigmoid(proj(4))
    gb_ref[...] = jax.nn.sigmoid(proj(5))
    k, v = proj(1), proj(2)
    v_t = v.T
    kt_ref[0] = k.T
    vt_ref[0] = v_t
    kb_ref[...] = k.astype(bf16)
    vtb_ref[0] = v_t.astype(bf16)
    km_ref[0] = jnp.mean(k, axis=0, keepdims=True)


def _proj_splits(widths, w_in):
    splits = tuple(int(s) for s in np.concatenate([[0], np.cumsum(widths)]))
    assert splits[-1] == w_in.shape[1] and all(s % LANES == 0 for s in splits)
    return splits


def _in_proj(h, g, w_in, widths, tm):
    n, d = h.shape
    order = (0, 3, 4, 5, 1, 2)
    return pl.pallas_call(
        functools.partial(_in_proj_kernel, _proj_splits(widths, w_in)),
        grid=(n // tm,),
        in_specs=[
            pl.BlockSpec((tm, d), lambda i: (i, 0)),
            _const_spec((1, d)),
            _const_spec(w_in.shape),
        ],
        out_specs=[pl.BlockSpec((tm, widths[j]), lambda i: (i, 0)) for j in order],
        out_shape=[jax.ShapeDtypeStruct((n, widths[j]), f32) for j in order],
        compiler_params=_params("parallel"),
        name="in_proj",
    )(h, g, w_in)


def _in_proj_seq(h, g, w_in, widths, seq, prev, rwkv_weights):
    n, d = h.shape
    tm = MOBA_BLOCK
    assert seq % tm == 0
    nbs = seq // tm
    mw = widths[1]
    rwkv_widths, (mu, *consts) = _rwkv_consts(rwkv_weights)
    rw, rp = rwkv_widths[0], mu.shape[-1]
    row = lambda w, dt=f32: (pl.BlockSpec((tm, w), lambda i: (i, 0)), jax.ShapeDtypeStruct((n, w), dt))
    t_spec = pl.BlockSpec((1, mw, tm), lambda i: (i // nbs, 0, i % nbs))
    t_shape = jax.ShapeDtypeStruct((n // seq, mw, seq), f32)
    outs = [row(widths[0]), row(widths[3]), row(widths[4]), row(widths[5]),
            (t_spec, t_shape), (t_spec, t_shape), row(mw, bf16),
            (pl.BlockSpec((1, mw, tm), lambda i: (i, 0, 0)), jax.ShapeDtypeStruct((n // tm, mw, tm), bf16)),
            (pl.BlockSpec((1, 1, mw), lambda i: (i, 0, 0)), jax.ShapeDtypeStruct((n // tm, 1, mw), f32))]
    outs += [row(rw)] * 7
    return pl.pallas_call(
        functools.partial(_in_proj_seq_kernel, _proj_splits(widths, w_in), nbs, rwkv_widths),
        grid=(n // tm,),
        in_specs=[
            pl.BlockSpec((tm, d), lambda i: (i, 0)),
            _const_spec((1, d)),
            _const_spec(w_in.shape),
            pl.BlockSpec((1, 1, rp), lambda i: (i // nbs, 0, 0)),
            _const_spec(mu.shape),
        ] + [_const_spec(c.shape) for c in consts],
        out_specs=[o[0] for o in outs],
        out_shape=[o[1] for o in outs],
        scratch_shapes=[pltpu.VMEM((1, rp), f32)],
        compiler_params=_params("arbitrary"),
        name="in_proj_seq",
    )(h, g, w_in, prev, mu, *consts)


def _top_blocks(gate, topk):
    blk = lax.broadcasted_iota(jnp.int32, gate.shape, 0).astype(f32)
    sel = jnp.zeros(gate.shape, f32)
    big = float(gate.shape[0])
    for _ in range(topk):
        m = jnp.max(gate, axis=0, keepdims=True)
        cand = (gate == m) & (gate > NEG_INF)
        idx = jnp.min(jnp.where(cand, blk, big), axis=0, keepdims=True)
        hit = blk == idx
        sel = jnp.where(hit, 1.0, sel)
        gate = jnp.where(hit, NEG_INF, gate)
    return sel


def _moba_prompt_kernel(n_heads, q_ref, kb_ref, vt_ref, km_ref, o_ref, off_sc, s_sc):
    group = pl.program_id(1)
    qi = pl.program_id(2)
    tq = q_ref.shape[0]
    nb = km_ref.shape[0]
    heads = q_ref.shape[1] // HEAD_DIM
    tile = lambda j: pl.ds((j // HEADS_PER_TILE) * LANES, LANES)
    lane = lax.broadcasted_iota(jnp.int32, (1, LANES), 1)
    key_i = lax.broadcasted_iota(jnp.int32, (MOBA_BLOCK, tq), 0)
    qry_i = lax.broadcasted_iota(jnp.int32, (MOBA_BLOCK, tq), 1)
    rel = (qry_i - key_i).astype(f32)
    causal = key_i <= qry_i
    blk_row = lax.broadcasted_iota(jnp.int32, (nb, tq), 0)

    own = pl.ds(pl.multiple_of(qi * MOBA_BLOCK, MOBA_BLOCK), MOBA_BLOCK)
    ones_rows = jnp.ones((SUM_ROWS, MOBA_BLOCK), bf16)
    with_sum = lambda vt: jnp.concatenate([vt, ones_rows], axis=0)

    hs = range(heads)
    q = [q_ref[:, tile(j)] for j in hs]
    head_mask = [(lane // HEAD_DIM == j % HEADS_PER_TILE).astype(f32) for j in hs]
    head_no = [jnp.full((1, 1), group * heads + j + 1, jnp.int32).astype(f32) for j in hs]
    slope2 = [jnp.exp2(head_no[j] * (-8.0 / n_heads)) * LOG2E for j in hs]
    qs = [(q[j] * head_mask[j] * (HEAD_DIM ** -0.5 * LOG2E)).astype(bf16) for j in hs]
    qk_own = [_dg(kb_ref[own, tile(j)], qs[j], _NT) for j in hs]
    gate = [_dot_hi(km_ref[:, tile(j)] * head_mask[j], q[j], _NT) for j in hs]
    alibi = [slope2[j] * rel for j in hs]
    s = [jnp.where(causal, qk_own[j] - alibi[j], NEG_INF) for j in hs]
    m = [jnp.max(s[j], axis=0, keepdims=True) for j in hs]
    p = [jnp.exp2(s[j] - m[j]).astype(bf16) for j in hs]
    acc = [_dg(with_sum(vt_ref[qi, tile(j), :]), p[j], _NN) for j in hs]
    for j in hs:
        picked = _top_blocks(jnp.where(blk_row < qi, gate[j], NEG_INF), MOBA_TOPK)
        off_sc[j] = jnp.where(picked > 0.5, 0.0, POS_INF)
    consts = [(qs[j], alibi[j], slope2[j]) for j in hs]
    carry = [x for j in hs for x in (m[j], acc[j])]

    def score(n, slot):
        nc = jnp.minimum(n, nb - 1)
        rows = pl.ds(pl.multiple_of(nc * MOBA_BLOCK, MOBA_BLOCK), MOBA_BLOCK)
        qk = [_dg(kb_ref[rows, tile(j)], consts[j][0], _NT) for j in hs]
        s = [qk[j] - consts[j][1] for j in hs]
        for j in hs:
            s_sc[slot, j] = s[j]
        return [jnp.max(s[j], axis=0, keepdims=True) for j in hs]

    def attend(n, slot, raw, stats):
        nc = jnp.minimum(n, nb - 1)
        m, acc = stats[0::2], stats[1::2]
        off = [jnp.where(n < qi, off_sc[j, pl.ds(nc, 1), :], POS_INF) for j in hs]
        far = [((qi - n) * MOBA_BLOCK).astype(f32) * consts[j][2] for j in hs]
        m_new = [jnp.maximum(m[j], raw[j] - far[j] - off[j]) for j in hs]
        alpha = [jnp.exp2(m[j] - m_new[j]) for j in hs]
        p = [jnp.exp2(s_sc[slot, j] - (m_new[j] + far[j] + off[j])).astype(bf16) for j in hs]
        pv = [_dg(with_sum(vt_ref[nc, tile(j), :]), p[j], _NN) for j in hs]
        return [x for j in hs for x in (m_new[j], acc[j] * alpha[j] + pv[j])]

    def body(i, carry):
        stats, raw = list(carry[:-heads]), list(carry[-heads:])
        for u in range(PIPE_UNROLL):
            n = PIPE_UNROLL * i + u
            raw_next = score(n + 1, (u + 1) % 2)
            stats = attend(n, u % 2, raw, stats)
            raw = raw_next
        return tuple(stats + raw)

    trips = (qi + PIPE_UNROLL - 1) // PIPE_UNROLL
    carry = lax.fori_loop(0, trips, body, tuple(carry + score(0, 0)))
    accs = [carry[2 * j + 1] for j in range(heads)]
    in_tile = lambda j: slice((j % HEADS_PER_TILE) * HEAD_DIM, (j % HEADS_PER_TILE + 1) * HEAD_DIM)
    out_t = jnp.concatenate([accs[j][in_tile(j)] / accs[j][LANES:LANES + 1] for j in range(heads)], axis=0)
    o_ref[...] = out_t.T


def _moba_prompt(q, k_bf, v_t, kmean, batch, seq, n_heads):
    n, w = q.shape
    wg = MOBA_GROUP_HEADS * HEAD_DIM
    assert w % wg == 0 and wg % LANES == 0
    nb = seq // MOBA_BLOCK
    tq = MOBA_BLOCK
    return pl.pallas_call(
        functools.partial(_moba_prompt_kernel, n_heads),
        grid=(batch, w // wg, nb),
        in_specs=[
            pl.BlockSpec((tq, wg), lambda b, p, i: (b * nb + i, p)),
            pl.BlockSpec((seq, wg), lambda b, p, i: (b, p)),
            pl.BlockSpec((nb, wg, MOBA_BLOCK), lambda b, p, i: (b, p, 0)),
            pl.BlockSpec((nb, wg), lambda b, p, i: (b, p)),
        ],
        out_specs=pl.BlockSpec((tq, wg), lambda b, p, i: (b * nb + i, p)),
        out_shape=jax.ShapeDtypeStruct((n, w), f32),
        scratch_shapes=[pltpu.VMEM((MOBA_GROUP_HEADS, nb, tq), f32),
                        pltpu.VMEM((2, MOBA_GROUP_HEADS, MOBA_BLOCK, tq), f32)],
        compiler_params=_params("parallel", "parallel", "arbitrary"),
        name="moba_prompt",
    )(q, k_bf, v_t, kmean)


def _moba_sample_kernel(t_new, n_blocks, n_heads, pt_ref, q_ref, kn_ref, vn_ref, *refs):
    del pt_ref
    k_refs, v_refs, o_ref = refs[:2 * n_blocks], refs[2 * n_blocks:4 * n_blocks], refs[4 * n_blocks]
    rows, width = q_ref.shape[1], q_ref.shape[2]
    page = k_refs[0].shape[3]
    head_bits = n_heads.bit_length() - 1
    r_idx = lax.broadcasted_iota(jnp.int32, (rows, 1), 0)
    head = r_idx & (n_heads - 1)
    t_q = r_idx >> head_bits
    slope = jnp.exp2((head + 1).astype(f32) * (-8.0 / n_heads))
    qs = (q_ref[0] * (HEAD_DIM ** -0.5)).astype(bf16)
    tok = lax.broadcasted_iota(jnp.int32, (1, page), 1)
    past_len = n_blocks * MOBA_BLOCK
    flat = lambda ref: ref[0].reshape(width, page).astype(bf16)

    blocks, pages = range(n_blocks), range(2 * n_blocks)
    qk = [_dg(qs, flat(k_refs[i]), _NN) for i in pages]
    s = [qk[i] - slope * (past_len - i * page + t_q - tok).astype(f32) for i in pages]
    row_max = [jnp.max(s[i], axis=-1, keepdims=True) for i in pages]
    m_blk = [jnp.maximum(row_max[2 * n], row_max[2 * n + 1]) for n in blocks]
    p = [jnp.exp(s[i] - m_blk[i // 2]) for i in pages]
    row_sum = [jnp.sum(p[i], axis=-1, keepdims=True) for i in pages]
    l_blk = [row_sum[2 * n] + row_sum[2 * n + 1] for n in blocks]
    pv = [_dg(p[i].astype(bf16), flat(v_refs[i]), _NT) for i in pages]
    acc_blk = [pv[2 * n] + pv[2 * n + 1] for n in blocks]
    qk_sum = [jnp.sum(qk[i], axis=-1, keepdims=True) for i in pages]
    gates = [qk_sum[2 * n] + qk_sum[2 * n + 1] for n in blocks]

    t_k = lax.broadcasted_iota(jnp.int32, (1, kn_ref.shape[1]), 1)
    s_own = _dg(qs, kn_ref[0].astype(bf16), _NT) - slope * (t_q - t_k).astype(f32)
    s_own = jnp.where((t_k <= t_q) & (t_k < t_new), s_own, NEG_INF)
    m_tot = jnp.max(s_own, axis=-1, keepdims=True)
    chosen = []
    for n in range(n_blocks):
        rank = jnp.zeros_like(gates[n])
        for i in range(n_blocks):
            if i == n:
                continue
            ahead = (gates[i] > gates[n]) | ((gates[i] == gates[n]) & (i < n))
            rank = rank + ahead.astype(f32)
        chosen.append(rank < MOBA_TOPK)
    for n in range(n_blocks):
        m_tot = jnp.maximum(m_tot, jnp.where(chosen[n], m_blk[n], NEG_INF))
    p_own = jnp.exp(s_own - m_tot)
    l_tot = jnp.sum(p_own, axis=-1, keepdims=True)
    acc_tot = _dg(p_own.astype(bf16), vn_ref[0].astype(bf16), _NN)
    for n in range(n_blocks):
        wgt = jnp.where(chosen[n], jnp.exp(m_blk[n] - m_tot), 0.0)
        l_tot = l_tot + wgt * l_blk[n]
        acc_tot = acc_tot + wgt * acc_blk[n]
    own_head = (lax.broadcasted_iota(jnp.int32, (1, width), 1) // HEAD_DIM) == head
    out = jnp.where(own_head, acc_tot / l_tot, 0.0)
    o_ref[0] = jnp.sum(out.reshape(rows // n_heads, n_heads, width), axis=1)


def _moba_sample(q, k_new, v_new, cache_k, cache_v, page_table):
    batch, t_new, width = q.shape
    n_pool, page, n_heads, dim = cache_k.shape
    n_pages = page_table.shape[1]
    key_rows = 8
    assert dim == HEAD_DIM and width == n_heads * dim and MOBA_BLOCK == 2 * page and n_pages % 2 == 0
    assert t_new <= key_rows and n_heads & (n_heads - 1) == 0 and (t_new * n_heads) % 8 == 0
    n_blocks = n_pages // 2
    assert n_blocks >= MOBA_TOPK
    cache_kt = jnp.transpose(cache_k, (0, 2, 3, 1))
    cache_vt = jnp.transpose(cache_v, (0, 2, 3, 1))
    head_lanes = (jnp.arange(width)[None, :] // dim == jnp.arange(n_heads)[:, None]).astype(q.dtype)
    q_rows = (q[:, :, None, :] * head_lanes).reshape(batch, t_new * n_heads, width)
    pad_keys = lambda a: jnp.pad(a, ((0, 0), (0, key_rows - t_new), (0, 0)))

    def page_spec(i):
        return pl.BlockSpec((1, n_heads, dim, page), lambda b, pt: (pt[b, i], 0, 0, 0))

    pages = [page_spec(i) for i in range(n_pages)]
    rows_spec = lambda r: pl.BlockSpec((1, r, width), lambda b, pt: (b, 0, 0))
    return pl.pallas_call(
        functools.partial(_moba_sample_kernel, t_new, n_blocks, n_heads),
        grid_spec=pltpu.PrefetchScalarGridSpec(
            num_scalar_prefetch=1,
            grid=(batch,),
            in_specs=[rows_spec(t_new * n_heads), rows_spec(key_rows), rows_spec(key_rows)] + pages + pages,
            out_specs=rows_spec(t_new),
        ),
        out_shape=jax.ShapeDtypeStruct((batch, t_new, width), f32),
        compiler_params=_params("parallel"),
        name="moba_sample",
    )(page_table, q_rows, pad_keys(k_new), pad_keys(v_new), *([cache_kt] * n_pages), *([cache_vt] * n_pages))


def _rwkv_features(m, widths, w0_ref, w2_ref, a0_ref, a2_ref, g2_ref, kk_ref, ka_ref, ones_ref):
    rw, dl, al, gl = widths
    r = m[:, 0:rw]
    k = m[:, rw:2 * rw]
    v = m[:, 2 * rw:3 * rw]
    lora = m[:, 3 * rw:3 * rw + dl + al]
    gd = m[:, 3 * rw + dl + al:3 * rw + dl + al + gl]

    z = -(w0_ref[...] + _dot_hi(jnp.tanh(lora), w2_ref[...]))
    softplus = jnp.maximum(z, 0.0) + jnp.log(1.0 + jnp.exp(-jnp.abs(z)))
    w = -softplus - 0.5
    log_decay = -jnp.exp(w)
    a = jax.nn.sigmoid(a0_ref[...] + _dot_hi(lora, a2_ref[...]))
    g = _dot_hi(jax.nn.sigmoid(gd), g2_ref[...])
    kk = k * kk_ref[...]
    norm = jnp.sqrt(_head_sums(kk * kk, ones_ref[...]))
    kk = kk / jnp.maximum(norm, 1e-12)
    kh = k * (1.0 + (a - 1.0) * ka_ref[...])
    return [r, kh, v, kk, kk * a, log_decay, g]


def _rwkv_pre_kernel(t_valid, t_pad, widths, p_ref, prev_ref, mu_ref, *refs):
    const_refs, out_refs, carry = refs[:8], refs[8:15], refs[15]
    ti = pl.program_id(1)
    bt, tt, width = p_ref.shape
    rw = widths[0]

    @pl.when(ti == 0)
    def _():
        carry[...] = prev_ref[...]

    p = p_ref[...]
    t_idx = lax.broadcasted_iota(jnp.int32, (bt, tt, 1), 1)
    shifted = jnp.where(t_idx == 0, carry[...], pltpu.roll(p, 1, axis=1))
    carry[...] = p[:, tt - 1:tt, :]
    m = (p + (shifted - p) * mu_ref[...]).reshape(bt * tt, width)
    *outs, g = _rwkv_features(m, widths, *const_refs)
    if t_valid < t_pad:
        live = (lax.broadcasted_iota(jnp.int32, (bt, tt, 1), 1) + ti * tt < t_valid)
        live = live.astype(f32).reshape(bt * tt, 1)
        outs = [x * live for x in outs]
    shape3 = (bt, tt, rw)
    for o_ref, x in zip(out_refs, outs + [g]):
        o_ref[...] = x.reshape(shape3)


def _head_ones():
    idx = np.arange(MXU_DIM) // HEAD_DIM
    return jnp.asarray(idx[:, None] == idx[None, :], dtype=bf16)


def _rwkv_consts(weights):
    mu, w0, w2, a0, a2, g2, k_k, k_a = weights
    rw = w0.shape[-1]
    widths = (rw, w2.shape[0], a2.shape[0], g2.shape[0])
    assert 3 * rw + sum(widths[1:]) == mu.shape[-1]
    assert widths[1] + widths[2] == LANES and widths[3] % LANES == 0 and rw % MXU_DIM == 0
    w2p = jnp.pad(w2, ((0, widths[2]), (0, 0)))
    a2p = jnp.pad(a2, ((widths[1], 0), (0, 0)))
    return widths, (mu, w0, w2p, a0, a2p, g2, k_k, k_a, _head_ones())


def _rwkv_pre(p, prev, t_valid, weights, bt, tt):
    batch, t_pad, width = p.shape
    widths, consts = _rwkv_consts(weights)
    rw = widths[0]
    out_spec = pl.BlockSpec((bt, tt, rw), lambda b, t: (b, t, 0))
    return pl.pallas_call(
        functools.partial(_rwkv_pre_kernel, t_valid, t_pad, widths),
        grid=(batch // bt, t_pad // tt),
        in_specs=[pl.BlockSpec((bt, tt, width), lambda b, t: (b, t, 0)),
                  pl.BlockSpec((bt, 1, width), lambda b, t: (b, 0, 0))]
                 + [_const_spec(c.shape) for c in consts],
        out_specs=[out_spec] * 7,
        out_shape=[jax.ShapeDtypeStruct((batch, t_pad, rw), f32)] * 7,
        scratch_shapes=[pltpu.VMEM((bt, 1, width), f32)],
        compiler_params=_params("parallel", "arbitrary"),
        name="rwkv_pre",
    )(p, prev, *consts)


def _rwkv_scan_kernel(r_ref, k_ref, v_ref, kk_ref, b_ref, lw_ref, s0_ref, o_ref, s_out, state):
    ci = pl.program_id(1)
    bt, c, width = r_ref.shape
    n_pairs = width // LANES
    c2 = HEADS_PER_TILE * c
    n_double = int(math.log2(c))
    assert 1 << n_double == c and HEADS_PER_TILE == 2

    @pl.when(ci == 0)
    def _():
        state[...] = s0_ref[...]

    ri = lax.broadcasted_iota(jnp.int32, (c, c), 0)
    cj = lax.broadcasted_iota(jnp.int32, (c, c), 1)
    tri_bf = (cj <= ri).astype(bf16)
    r2 = lax.broadcasted_iota(jnp.int32, (c2, c2), 0)
    q2 = lax.broadcasted_iota(jnp.int32, (c2, c2), 1)
    same_blk = (r2 >= c) == (q2 >= c)
    strict2 = (same_blk & (q2 < r2)).astype(f32)
    incl2 = (same_blk & (q2 <= r2)).astype(f32)
    eye2 = (q2 == r2).astype(f32)
    lane = lax.broadcasted_iota(jnp.int32, (1, LANES), 1)
    hm = [(lane // HEAD_DIM == j).astype(f32) for j in range(HEADS_PER_TILE)]
    bi_ = lax.broadcasted_iota(jnp.int32, (LANES, LANES), 0) // HEAD_DIM
    bj_ = lax.broadcasted_iota(jnp.int32, (LANES, LANES), 1) // HEAD_DIM
    block_diag = (bi_ == bj_).astype(f32)

    def stack(x):
        return jnp.concatenate([x * hm[0], x * hm[1]], axis=0)

    def twice(x):
        return jnp.concatenate([x, x], axis=0)

    def unstack(x2):
        return x2[:c] * hm[0] + x2[c:] * hm[1]

    probs = [(bi, pl.ds(pair * LANES, LANES), pair) for bi in range(bt) for pair in range(n_pairs)]
    ld = lambda ref: [ref[bi, :, sl] for bi, sl, _ in probs]
    r, k, v, kk, b, lw = ld(r_ref), ld(k_ref), ld(v_ref), ld(kk_ref), ld(b_ref), ld(lw_ref)
    nprob = len(probs)
    each = range(nprob)

    cum = [_dot_exact_lhs(tri_bf, lw[i]) for i in each]
    cum_end = [cum[i][c - 1:c, :] for i in each]
    e_neg = [jnp.exp(-cum[i]) for i in each]
    e_end = [jnp.exp(cum_end[i] - cum[i]) for i in each]
    a_mat = [kk[i] * jnp.exp(cum[i] - lw[i]) for i in each]
    p_mat = [r[i] * jnp.exp(cum[i]) for i in each]
    xa = [stack(a_mat[i]) for i in each]
    xap = [jnp.concatenate([xa[i], stack(p_mat[i])], axis=0) for i in each]
    yb = [stack(b[i] * e_neg[i]) for i in each]
    yk = [stack(k[i] * e_neg[i]) for i in each]
    v2 = [twice(v[i]) for i in each]

    gb = [_dot_lo(xap[i], yb[i], _NT) for i in each]
    gk = [_dot_lo(xap[i], yk[i], _NT) for i in each]
    l_mat = [gb[i][:c2] * strict2 for i in each]
    m_mat = [gk[i][:c2] * strict2 for i in each]
    lr_mat = [gb[i][c2:] * incl2 for i in each]
    mr_mat = [gk[i][c2:] * incl2 for i in each]

    npow = [-l_mat[i] for i in each]
    t_inv = [eye2 + npow[i] for i in each]
    for _ in range(n_double - 1):
        npow = [_dot_lo(npow[i], npow[i]) for i in each]
        t_inv = [t_inv[i] + _dot_lo(npow[i], t_inv[i]) for i in each]
    mv = [_dot_lo(m_mat[i], v2[i]) for i in each]
    tz = [_dot_lo(t_inv[i], jnp.concatenate([xa[i], mv[i]], axis=1)) for i in each]
    ta = [tz[i][:c, :LANES] + tz[i][c:, :LANES] for i in each]
    tmv = [unstack(tz[i][:, LANES:]) for i in each]

    s_prev = [state[bi, pair] for bi, _, pair in probs]
    xs = [_dot_lo(jnp.concatenate([ta[i], p_mat[i]], axis=0), s_prev[i], _NT) for i in each]
    u = [-(xs[i][:c] + tmv[i]) for i in each]
    corr = [_dot_lo(lr_mat[i], twice(u[i])) + _dot_lo(mr_mat[i], v2[i]) for i in each]
    upd = [_dot_lo(jnp.concatenate([u[i], v[i]], axis=0),
                   jnp.concatenate([b[i] * e_end[i], k[i] * e_end[i]], axis=0), _TN) for i in each]
    for i, (bi, sl, pair) in enumerate(probs):
        state[bi, pair] = s_prev[i] * jnp.exp(cum_end[i]) + upd[i] * block_diag
        o_ref[bi, :, sl] = xs[i][c:] + unstack(corr[i])

    @pl.when(ci == pl.num_programs(1) - 1)
    def _():
        s_out[...] = state[...]


def _rwkv_scan(r, k, v, kk, b, lw, s0, bt, chunk):
    batch, t_pad, width = r.shape
    n_pairs = width // LANES
    seq_spec = pl.BlockSpec((bt, chunk, width), lambda bb, c: (bb, c, 0))
    st_spec = pl.BlockSpec((bt, n_pairs, LANES, LANES), lambda bb, c: (bb, 0, 0, 0))
    return pl.pallas_call(
        _rwkv_scan_kernel,
        grid=(batch // bt, t_pad // chunk),
        in_specs=[seq_spec] * 6 + [st_spec],
        out_specs=[seq_spec, st_spec],
        out_shape=[jax.ShapeDtypeStruct((batch, t_pad, width), f32),
                   jax.ShapeDtypeStruct((batch, n_pairs, LANES, LANES), f32)],
        scratch_shapes=[pltpu.VMEM((bt, n_pairs, LANES, LANES), f32)],
        compiler_params=_params("parallel", "arbitrary"),
        name="rwkv_scan",
    )(r, k, v, kk, b, lw, s0)


def _pair_state(s):
    bsz, h, n, _ = s.shape
    s = s.reshape(bsz, h // HEADS_PER_TILE, HEADS_PER_TILE, n, n)
    eye = jnp.eye(HEADS_PER_TILE, dtype=s.dtype)
    out = jnp.einsum("bpjvk,ji->bpjvik", s, eye)
    return out.reshape(bsz, h // HEADS_PER_TILE, LANES, LANES)


def _unpair_state(s, heads):
    bsz = s.shape[0]
    s = s.reshape(bsz, heads // HEADS_PER_TILE, HEADS_PER_TILE, HEAD_DIM, HEADS_PER_TILE, HEAD_DIM)
    diag = jnp.stack([s[:, :, j, :, j, :] for j in range(HEADS_PER_TILE)], axis=2)
    return diag.reshape(bsz, heads, HEAD_DIM, HEAD_DIM)


STATE_ROWS = 8


def _rwkv_step_kernel(r_ref, k_ref, v_ref, kk_ref, b_ref, lw_ref, s_ref, o_ref, s_out):
    n_tok = r_ref.shape[0]
    n_val = s_ref.shape[1]
    decay = [jnp.exp(lw_ref[t, 0]) for t in range(n_tok)]

    def rows(c, _):
        base = pl.multiple_of(c * STATE_ROWS, STATE_ROWS)
        state = [s_ref[0, base + i] for i in range(STATE_ROWS)]
        for t in range(n_tok):
            kk, b, k, r = kk_ref[t, 0], b_ref[t, 0], k_ref[t, 0], r_ref[t, 0]
            v_rows = v_ref[t, 0, pl.ds(base, STATE_ROWS), :]
            out = []
            for i in range(STATE_ROWS):
                s_kk = jnp.sum(state[i] * kk, axis=0, keepdims=True)
                state[i] = state[i] * decay[t] - s_kk * b + v_rows[i:i + 1] * k
                out.append(jnp.sum(state[i] * r, axis=0, keepdims=True))
            o_ref[t, 0, pl.ds(base, STATE_ROWS), :] = jnp.concatenate(out, axis=0)
        for i in range(STATE_ROWS):
            s_out[0, base + i] = state[i]
        return 0

    lax.fori_loop(0, n_val // STATE_ROWS, rows, 0)


def _rwkv_step(r, k, v, kk, b, lw, s0):
    n_tok, n_heads, dim, batch = r.shape
    assert s0.shape == (n_heads, dim, dim, batch) and dim % STATE_ROWS == 0
    seq_spec = pl.BlockSpec((n_tok, 1, dim, batch), lambda h: (0, h, 0, 0))
    st_spec = pl.BlockSpec((1, dim, dim, batch), lambda h: (h, 0, 0, 0))
    return pl.pallas_call(
        _rwkv_step_kernel,
        grid=(n_heads,),
        in_specs=[seq_spec] * 6 + [st_spec],
        out_specs=[seq_spec, st_spec],
        out_shape=[jax.ShapeDtypeStruct(r.shape, f32), jax.ShapeDtypeStruct(s0.shape, f32)],
        compiler_params=_params("parallel"),
        name="rwkv_step",
    )(r, k, v, kk, b, lw, s0)


def _merge_kernel(h_ref, oa_ref, o_ref, r_ref, k_ref, v_ref, g_ref, ga_ref, gb_ref,
                  rk_ref, lnw_ref, lnb_ref, ones_ref, wba_ref, wbb_ref, wout_ref, gpost_ref,
                  out_ref):
    ones = ones_ref[...]
    inv = 1.0 / HEAD_DIM
    branch_a = ga_ref[...] * jnp.dot(oa_ref[...].astype(bf16), wba_ref[...], preferred_element_type=f32)
    o = o_ref[...]
    rkr = r_ref[...] * k_ref[...] * rk_ref[...]
    mean = _head_sums(o, ones) * inv
    bonus = _head_sums(rkr, ones) * v_ref[...]
    d = o - mean
    var = _head_sums(d * d, ones) * inv
    on = d * lax.rsqrt(var + GN_EPS) * lnw_ref[...] + lnb_ref[...]
    ob = (on + bonus) * g_ref[...]
    merged = branch_a + gb_ref[...] * jnp.dot(ob.astype(bf16), wbb_ref[...], preferred_element_type=f32)
    y = jnp.dot(merged.astype(bf16), wout_ref[...], preferred_element_type=f32)
    out_ref[...] = h_ref[...] + _rms(y, gpost_ref[...])


def _merge(h, oa, o, r, k, v, g, ga, gb, consts, tm):
    n, d = h.shape
    w = oa.shape[1]
    row = lambda width: pl.BlockSpec((tm, width), lambda i: (i, 0))
    return pl.pallas_call(
        _merge_kernel,
        grid=(n // tm,),
        in_specs=[row(d)] + [row(w)] * 6 + [row(d)] * 2 + [_const_spec(c.shape) for c in consts],
        out_specs=row(d),
        out_shape=jax.ShapeDtypeStruct((n, d), f32),
        compiler_params=_params("parallel"),
        name="merge",
    )(h, oa, o, r, k, v, g, ga, gb, *consts)


def _row_tile(n, target):
    t = min(n, target)
    assert n % t == 0
    return t


def _layer(x, prev_shift, wkv0, past, w, n_heads):
    (g1a, g1b, f1g, f1u, f1d, gma, gmb, w_in, w_ba, w_bb, w_out,
     mu, w0, w2, a0, a2, g2, k_k, k_a, r_k, ln_w, ln_b,
     g2a, g2b, f2g, f2u, f2d) = w
    batch, t, d = x.shape
    n = batch * t
    mw = w_ba.shape[0]
    rw = w_bb.shape[0]
    rp = mu.shape[-1]
    widths = (mw, mw, mw, rp, d, d)
    row = lambda a: a.reshape(1, -1)

    x2 = x.reshape(n, d)
    h = _ffn(x2, row(g1a), row(g1b), f1g, f1u, f1d, _row_tile(n, 512))
    kv_shape = (batch, t, n_heads, HEAD_DIM)
    pre_w = (row(mu), row(w0), w2, row(a0), a2, g2, row(k_k), row(k_a))
    prev = prev_shift.reshape(batch, 1, rp)
    if past is None:
        assert t % MOBA_BLOCK == 0 and t % RWKV_CHUNK == 0
        q, p_rw, ga, gb, k_t, v_t, k_bf, vt_bf, kmean, *pre = _in_proj_seq(
            h, row(gma), w_in, widths, t, prev, pre_w)
        o_a = _moba_prompt(q, k_bf, vt_bf, kmean.reshape(n // MOBA_BLOCK, mw), batch, t, n_heads)
        k, v = (jnp.transpose(a.reshape(batch, n_heads, HEAD_DIM, t), (0, 3, 1, 2)) for a in (k_t, v_t))
        r_, kh, v_, g_ = (pre[i] for i in (0, 1, 2, 6))
        o_scan, s_fin = _rwkv_scan(*(a.reshape(batch, t, rw) for a in pre[:6]),
                                   _pair_state(wkv0.astype(f32)), min(batch, 2), RWKV_CHUNK)
        o_scan = o_scan.reshape(n, rw)
        wkv = _unpair_state(s_fin, rw // HEAD_DIM).astype(wkv0.dtype)
    else:
        cache_k, cache_v, page_table = past
        q, p_rw, ga, gb, k, v = _in_proj(h, row(gma), w_in, widths, _row_tile(n, MOBA_BLOCK))
        rows3 = lambda a: a.reshape(batch, t, mw)
        o_a = _moba_sample(rows3(q), rows3(k), rows3(v), cache_k, cache_v, page_table).reshape(n, mw)
        k, v = k.reshape(kv_shape), v.reshape(kv_shape)
        t_pad = -(-t // 8) * 8
        p3p = jnp.pad(p_rw.reshape(batch, t, rp), ((0, 0), (0, t_pad - t), (0, 0)))
        pre = _rwkv_pre(p3p, prev, t, pre_w, min(batch, 16), t_pad)
        r_, kh, v_, g_ = (pre[i][:, :t].reshape(n, rw) for i in (0, 1, 2, 6))
        heads = rw // HEAD_DIM
        lanes_last = lambda a: jnp.transpose(a[:, :t], (1, 2, 0)).reshape(t, heads, HEAD_DIM, batch)
        o_scan, s_fin = _rwkv_step(*(lanes_last(a) for a in pre[:6]),
                                   jnp.transpose(wkv0.astype(f32), (1, 2, 3, 0)))
        o_scan = jnp.transpose(o_scan.reshape(t, rw, batch), (2, 0, 1)).reshape(n, rw)
        wkv = jnp.transpose(s_fin, (3, 0, 1, 2)).astype(wkv0.dtype)
    p3 = p_rw.reshape(batch, t, rp)
    consts = (row(r_k), row(ln_w), row(ln_b), _head_ones(), w_ba, w_bb, w_out, row(gmb))
    h2 = _merge(h, o_a, o_scan, r_, kh, v_, g_, ga, gb, consts, _row_tile(n, 256))
    y = _ffn(h2, row(g2a), row(g2b), f2g, f2u, f2d, _row_tile(n, 512))
    return y.reshape(batch, t, d), k, v, wkv, p3[:, -1]


def kernel(x_prompt, x_sample, cache_k, cache_v, state_wkv, state_shift, page_table,
           g_ffn1_pre, g_ffn1_post, w_ffn1_gate, w_ffn1_up, w_ffn1_down,
           g_mix_pre, g_mix_post, w_in, w_branch_a, w_branch_b, w_out,
           rwkv_mu, rwkv_w0, rwkv_w2, rwkv_a0, rwkv_a2, rwkv_g2, rwkv_k_k, rwkv_k_a, rwkv_r_k,
           rwkv_ln_w, rwkv_ln_b,
           g_ffn2_pre, g_ffn2_post, w_ffn2_gate, w_ffn2_up, w_ffn2_down):
    depth = w_in.shape[0]
    bp = x_prompt.shape[0]
    n_heads = cache_k.shape[3]
    rp = rwkv_mu.shape[-1]
    rw = rwkv_w0.shape[-1]
    yp, ys = x_prompt, x_sample
    outs = [[] for _ in range(8)]
    for i in range(depth):
        cast = lambda a: a[i].astype(bf16)
        w = (g_ffn1_pre[i], g_ffn1_post[i], cast(w_ffn1_gate), cast(w_ffn1_up), cast(w_ffn1_down),
             g_mix_pre[i], g_mix_post[i], cast(w_in), cast(w_branch_a), cast(w_branch_b), cast(w_out),
             rwkv_mu[i], rwkv_w0[i], rwkv_w2[i], rwkv_a0[i], rwkv_a2[i], rwkv_g2[i], rwkv_k_k[i],
             rwkv_k_a[i], rwkv_r_k[i], rwkv_ln_w[i], rwkv_ln_b[i],
             g_ffn2_pre[i], g_ffn2_post[i], cast(w_ffn2_gate), cast(w_ffn2_up), cast(w_ffn2_down))
        yp, kp, vp, wkvp, shp = _layer(
            yp, jnp.zeros((bp, rp), x_prompt.dtype),
            jnp.zeros((bp, rw // HEAD_DIM, HEAD_DIM, HEAD_DIM), state_wkv.dtype), None, w, n_heads)
        ys, ksn, vsn, wkvs, shs = _layer(
            ys, state_shift[i], state_wkv[i], (cache_k[i], cache_v[i], page_table), w, n_heads)
        for lst, val in zip(outs, (kp, vp, ksn, vsn, wkvp, shp, wkvs, shs)):
            lst.append(val)
    return (yp, ys) + tuple(jnp.stack(lst) for lst in outs)
```

```python
import functools
import math

import jax
import jax.numpy as jnp
import numpy as np
from jax import lax
from jax.experimental import pallas as pl
from jax.experimental.pallas import tpu as pltpu

HEAD_DIM = 64
LANES = 128
HEADS_PER_TILE = LANES // HEAD_DIM
MXU_DIM = 256
MOBA_BLOCK = 256
MOBA_TOPK = 3
SUM_ROWS = 16
PIPE_UNROLL = 4
RWKV_CHUNK = 64
MOBA_GROUP_HEADS = 4
RMS_EPS = 1e-6
GN_EPS = 64e-5
VMEM_LIMIT = 56 * 1024 * 1024
NEG_INF = float("-inf")
POS_INF = float("inf")
LOG2E = 1.4426950408889634

f32 = jnp.float32
bf16 = jnp.bfloat16


def _params(*sem):
    return pltpu.CompilerParams(dimension_semantics=sem, vmem_limit_bytes=VMEM_LIMIT)


def _const_spec(shape):
    nd = len(shape)
    return pl.BlockSpec(shape, lambda *_: (0,) * nd, pipeline_mode=pl.Buffered(1))


def _rms(x, g):
    return x * lax.rsqrt(jnp.mean(x * x, axis=-1, keepdims=True) + RMS_EPS) * g


def _split3(x):
    hi = x.astype(bf16)
    r1 = x - hi.astype(f32)
    mid = r1.astype(bf16)
    lo = (r1 - mid.astype(f32)).astype(bf16)
    return hi, mid, lo


_NN = (((1,), (0,)), ((), ()))
_NT = (((1,), (1,)), ((), ()))
_TN = (((0,), (0,)), ((), ()))


def _dg(a, b, dims):
    return lax.dot_general(a, b, dims, preferred_element_type=f32)


def _dot_lo(a, b, dims=_NN):
    return _dg(a.astype(bf16), b.astype(bf16), dims)


def _dot_hi(a, b, dims=_NN):
    ah, am, _ = _split3(a)
    bh, bm, _ = _split3(b)
    return _dg(ah, bh, dims) + (_dg(ah, bm, dims) + _dg(am, bh, dims))


def _dot_exact_rhs(a, b_bf, dims=_NN):
    ah = a.astype(bf16)
    am = (a - ah.astype(f32)).astype(bf16)
    return _dg(ah, b_bf, dims) + _dg(am, b_bf, dims)


def _head_sums(x, ones_bf):
    g = ones_bf.shape[0]
    return jnp.concatenate([_dot_exact_rhs(x[:, c:c + g], ones_bf) for c in range(0, x.shape[1], g)],
                           axis=1)


def _ffn_kernel(x_ref, gpre_ref, gpost_ref, wg_ref, wu_ref, wd_ref, o_ref):
    x = x_ref[...]
    h = _rms(x, gpre_ref[...]).astype(bf16)
    gate = jnp.dot(h, wg_ref[...], preferred_element_type=f32)
    up = jnp.dot(h, wu_ref[...], preferred_element_type=f32)
    act = (gate * jax.nn.sigmoid(gate) * up).astype(bf16)
    y = jnp.dot(act, wd_ref[...], preferred_element_type=f32)
    o_ref[...] = x + 0.5 * _rms(y, gpost_ref[...])


def _ffn(x, g_pre, g_post, wg, wu, wd, tm):
    n, d = x.shape
    dff = wg.shape[1]
    return pl.pallas_call(
        _ffn_kernel,
        grid=(n // tm,),
        in_specs=[
            pl.BlockSpec((tm, d), lambda i: (i, 0)),
            _const_spec((1, d)), _const_spec((1, d)),
            _const_spec((d, dff)), _const_spec((d, dff)), _const_spec((dff, d)),
        ],
        out_specs=pl.BlockSpec((tm, d), lambda i: (i, 0)),
        out_shape=jax.ShapeDtypeStruct((n, d), f32),
        compiler_params=_params("parallel"),
        name="ffn",
    )(x, g_pre, g_post, wg, wu, wd)


def _in_proj_kernel(splits, h_ref, g_ref, w_ref, q_ref, p_ref, ga_ref, gb_ref, k_ref, v_ref):
    u = _rms(h_ref[...], g_ref[...]).astype(bf16)
    proj = lambda idx: _dg(u, w_ref[:, splits[idx]:splits[idx + 1]], _NN)
    q_ref[...] = proj(0)
    k_ref[...] = proj(1)
    v_ref[...] = proj(2)
    p_ref[...] = proj(3)
    ga_ref[...] = jax.nn.sigmoid(proj(4))
    gb_ref[...] = jax.nn.sigmoid(proj(5))


def _in_proj_seq_kernel(splits, blocks_per_seq, rwkv_widths, h_ref, g_ref, w_ref, prev_ref, mu_ref, *refs):
    const_refs, out_refs, carry = refs[:8], refs[8:-1], refs[-1]
    q_ref, p_ref, ga_ref, gb_ref, kt_ref, vt_ref, kb_ref, vtb_ref, km_ref = out_refs[:9]
    @pl.when(pl.program_id(0) % blocks_per_seq == 0)
    def _():
        carry[...] = prev_ref[0]

    u = _rms(h_ref[...], g_ref[...]).astype(bf16)
    proj = lambda idx: _dg(u, w_ref[:, splits[idx]:splits[idx + 1]], _NN)
    p = proj(3)
    p_ref[...] = p
    rows = p.shape[0]
    first = lax.broadcasted_iota(jnp.int32, (rows, 1), 0) == 0
    shifted = jnp.where(first, carry[...], pltpu.roll(p, 1, axis=0))
    carry[...] = p[rows - 1:rows, :]
    feats = _rwkv_features(p + (shifted - p) * mu_ref[...], rwkv_widths, *const_refs)
    for o_ref, x in zip(out_refs[9:], feats):
        o_ref[...] = x

    k, v = proj(1), proj(2)
    v_t = v.T
    kt_ref[0] = k.T
    vt_ref[0] = v_t
    kb_ref[...] = k.astype(bf16)
    vtb_ref[0] = v_t.astype(bf16)
    km_ref[0] = jnp.mean(k, axis=0, keepdims=True)
    q_ref[...] = proj(0)
    ga_ref[...] = jax.nn.sigmoid(proj(4))
    gb_ref[...] = jax.nn.sigmoid(proj(5))


def _proj_splits(widths, w_in):
    splits = tuple(int(s) for s in np.concatenate([[0], np.cumsum(widths)]))
    assert splits[-1] == w_in.shape[1] and all(s % LANES == 0 for s in splits)
    return splits


def _in_proj(h, g, w_in, widths, tm):
    n, d = h.shape
    order = (0, 3, 4, 5, 1, 2)
    return pl.pallas_call(
        functools.partial(_in_proj_kernel, _proj_splits(widths, w_in)),
        grid=(n // tm,),
        in_specs=[
            pl.BlockSpec((tm, d), lambda i: (i, 0)),
            _const_spec((1, d)),
            _const_spec(w_in.shape),
        ],
        out_specs=[pl.BlockSpec((tm, widths[j]), lambda i: (i, 0)) for j in order],
        out_shape=[jax.ShapeDtypeStruct((n, widths[j]), f32) for j in order],
        compiler_params=_params("parallel"),
        name="in_proj",
    )(h, g, w_in)


def _in_proj_seq(h, g, w_in, widths, seq, prev, rwkv_weights):
    n, d = h.shape
    tm = MOBA_BLOCK
    assert seq % tm == 0
    nbs = seq // tm
    mw = widths[1]
    rwkv_widths, (mu, *consts) = _rwkv_consts(rwkv_weights)
    rw, rp = rwkv_widths[0], mu.shape[-1]
    row = lambda w, dt=f32: (pl.BlockSpec((tm, w), lambda i: (i, 0)), jax.ShapeDtypeStruct((n, w), dt))
    t_spec = pl.BlockSpec((1, mw, tm), lambda i: (i // nbs, 0, i % nbs))
    t_shape = jax.ShapeDtypeStruct((n // seq, mw, seq), f32)
    outs = [row(widths[0]), row(widths[3]), row(widths[4]), row(widths[5]),
            (t_spec, t_shape), (t_spec, t_shape), row(mw, bf16),
            (pl.BlockSpec((1, mw, tm), lambda i: (i, 0, 0)), jax.ShapeDtypeStruct((n // tm, mw, tm), bf16)),
            (pl.BlockSpec((1, 1, mw), lambda i: (i, 0, 0)), jax.ShapeDtypeStruct((n // tm, 1, mw), f32))]
    outs += [row(rw)] * 7
    return pl.pallas_call(
        functools.partial(_in_proj_seq_kernel, _proj_splits(widths, w_in), nbs, rwkv_widths),
        grid=(n // tm,),
        in_specs=[
            pl.BlockSpec((tm, d), lambda i: (i, 0)),
            _const_spec((1, d)),
            _const_spec(w_in.shape),
            pl.BlockSpec((1, 1, rp), lambda i: (i // nbs, 0, 0)),
            _const_spec(mu.shape),
        ] + [_const_spec(c.shape) for c in consts],
        out_specs=[o[0] for o in outs],
        out_shape=[o[1] for o in outs],
        scratch_shapes=[pltpu.VMEM((1, rp), f32)],
        compiler_params=_params("arbitrary"),
        name="in_proj_seq",
    )(h, g, w_in, prev, mu, *consts)


def _top_blocks(gate, topk):
    blk = lax.broadcasted_iota(jnp.int32, gate.shape, 0).astype(f32)
    sel = jnp.zeros(gate.shape, f32)
    big = float(gate.shape[0])
    for _ in range(topk):
        m = jnp.max(gate, axis=0, keepdims=True)
        cand = (gate == m) & (gate > NEG_INF)
        idx = jnp.min(jnp.where(cand, blk, big), axis=0, keepdims=True)
        hit = blk == idx
        sel = jnp.where(hit, 1.0, sel)
        gate = jnp.where(hit, NEG_INF, gate)
    return sel


def _moba_prompt_kernel(n_heads, q_ref, kb_ref, vt_ref, km_ref, o_ref, off_sc, s_sc):
    group = pl.program_id(1)
    qi = pl.program_id(2)
    tq = q_ref.shape[0]
    nb = km_ref.shape[0]
    heads = q_ref.shape[1] // HEAD_DIM
    tile = lambda j: pl.ds((j // HEADS_PER_TILE) * LANES, LANES)
    lane = lax.broadcasted_iota(jnp.int32, (1, LANES), 1)
    key_i = lax.broadcasted_iota(jnp.int32, (MOBA_BLOCK, tq), 0)
    qry_i = lax.broadcasted_iota(jnp.int32, (MOBA_BLOCK, tq), 1)
    rel = (qry_i - key_i).astype(f32)
    causal = key_i <= qry_i
    blk_row = lax.broadcasted_iota(jnp.int32, (nb, tq), 0)

    own = pl.ds(pl.multiple_of(qi * MOBA_BLOCK, MOBA_BLOCK), MOBA_BLOCK)
    ones_rows = jnp.ones((SUM_ROWS, MOBA_BLOCK), bf16)
    with_sum = lambda vt: jnp.concatenate([vt, ones_rows], axis=0)

    hs = range(heads)
    q = [q_ref[:, tile(j)] for j in hs]
    head_mask = [(lane // HEAD_DIM == j % HEADS_PER_TILE).astype(f32) for j in hs]
    head_no = [jnp.full((1, 1), group * heads + j + 1, jnp.int32).astype(f32) for j in hs]
    slope2 = [jnp.exp2(head_no[j] * (-8.0 / n_heads)) * LOG2E for j in hs]
    qs = [(q[j] * head_mask[j] * (HEAD_DIM ** -0.5 * LOG2E)).astype(bf16) for j in hs]
    qk_own = [_dg(kb_ref[own, tile(j)], qs[j], _NT) for j in hs]
    gate = [_dot_hi(km_ref[:, tile(j)] * head_mask[j], q[j], _NT) for j in hs]
    alibi = [slope2[j] * rel for j in hs]
    s = [jnp.where(causal, qk_own[j] - alibi[j], NEG_INF) for j in hs]
    m = [jnp.max(s[j], axis=0, keepdims=True) for j in hs]
    p = [jnp.exp2(s[j] - m[j]).astype(bf16) for j in hs]
    acc = [_dg(with_sum(vt_ref[qi, tile(j), :]), p[j], _NN) for j in hs]
    for j in hs:
        picked = _top_blocks(jnp.where(blk_row < qi, gate[j], NEG_INF), MOBA_TOPK)
        off_sc[j] = jnp.where(picked > 0.5, 0.0, POS_INF)
    consts = [(qs[j], alibi[j], slope2[j]) for j in hs]
    carry = [x for j in hs for x in (m[j], acc[j])]

    def score(n, slot):
        nc = jnp.minimum(n, nb - 1)
        rows = pl.ds(pl.multiple_of(nc * MOBA_BLOCK, MOBA_BLOCK), MOBA_BLOCK)
        qk = [_dg(kb_ref[rows, tile(j)], consts[j][0], _NT) for j in hs]
        s = [qk[j] - consts[j][1] for j in hs]
        for j in hs:
            s_sc[slot, j] = s[j]
        return [jnp.max(s[j], axis=0, keepdims=True) for j in hs]

    def attend(n, slot, raw, stats):
        nc = jnp.minimum(n, nb - 1)
        m, acc = stats[0::2], stats[1::2]
        off = [jnp.where(n < qi, off_sc[j, pl.ds(nc, 1), :], POS_INF) for j in hs]
        far = [((qi - n) * MOBA_BLOCK).astype(f32) * consts[j][2] for j in hs]
        m_new = [jnp.maximum(m[j], raw[j] - far[j] - off[j]) for j in hs]
        alpha = [jnp.exp2(m[j] - m_new[j]) for j in hs]
        p = [jnp.exp2(s_sc[slot, j] - (m_new[j] + far[j] + off[j])).astype(bf16) for j in hs]
        pv = [_dg(with_sum(vt_ref[nc, tile(j), :]), p[j], _NN) for j in hs]
        return [x for j in hs for x in (m_new[j], acc[j] * alpha[j] + pv[j])]

    def body(i, carry):
        stats, raw = list(carry[:-heads]), list(carry[-heads:])
        for u in range(PIPE_UNROLL):
            n = PIPE_UNROLL * i + u
            raw_next = score(n + 1, (u + 1) % 2)
            stats = attend(n, u % 2, raw, stats)
            raw = raw_next
        return tuple(stats + raw)

    trips = (qi + PIPE_UNROLL - 1) // PIPE_UNROLL
    carry = lax.fori_loop(0, trips, body, tuple(carry + score(0, 0)))
    accs = [carry[2 * j + 1] for j in range(heads)]
    in_tile = lambda j: slice((j % HEADS_PER_TILE) * HEAD_DIM, (j % HEADS_PER_TILE + 1) * HEAD_DIM)
    out_t = jnp.concatenate([accs[j][in_tile(j)] / accs[j][LANES:LANES + 1] for j in range(heads)], axis=0)
    o_ref[...] = out_t.T


def _moba_prompt(q, k_bf, v_t, kmean, batch, seq, n_heads):
    n, w = q.shape
    wg = MOBA_GROUP_HEADS * HEAD_DIM
    assert w % wg == 0 and wg % LANES == 0
    nb = seq // MOBA_BLOCK
    tq = MOBA_BLOCK
    return pl.pallas_call(
        functools.partial(_moba_prompt_kernel, n_heads),
        grid=(batch, w // wg, nb),
        in_specs=[
            pl.BlockSpec((tq, wg), lambda b, p, i: (b * nb + i, p)),
            pl.BlockSpec((seq, wg), lambda b, p, i: (b, p)),
            pl.BlockSpec((nb, wg, MOBA_BLOCK), lambda b, p, i: (b, p, 0)),
            pl.BlockSpec((nb, wg), lambda b, p, i: (b, p)),
        ],
        out_specs=pl.BlockSpec((tq, wg), lambda b, p, i: (b * nb + i, p)),
        out_shape=jax.ShapeDtypeStruct((n, w), f32),
        scratch_shapes=[pltpu.VMEM((MOBA_GROUP_HEADS, nb, tq), f32),
                        pltpu.VMEM((2, MOBA_GROUP_HEADS, MOBA_BLOCK, tq), f32)],
        compiler_params=_params("parallel", "parallel", "arbitrary"),
        name="moba_prompt",
    )(q, k_bf, v_t, kmean)


def _moba_sample_kernel(t_new, n_blocks, n_heads, pt_ref, q_ref, kn_ref, vn_ref, *refs):
    del pt_ref
    k_refs, v_refs, o_ref = refs[:2 * n_blocks], refs[2 * n_blocks:4 * n_blocks], refs[4 * n_blocks]
    rows, width = q_ref.shape[1], q_ref.shape[2]
    page = k_refs[0].shape[3]
    head_bits = n_heads.bit_length() - 1
    r_idx = lax.broadcasted_iota(jnp.int32, (rows, 1), 0)
    head = r_idx & (n_heads - 1)
    t_q = r_idx >> head_bits
    slope = jnp.exp2((head + 1).astype(f32) * (-8.0 / n_heads))
    qs = (q_ref[0] * (HEAD_DIM ** -0.5)).astype(bf16)
    tok = lax.broadcasted_iota(jnp.int32, (1, page), 1)
    past_len = n_blocks * MOBA_BLOCK
    flat = lambda ref: ref[0].reshape(width, page).astype(bf16)

    blocks, pages = range(n_blocks), range(2 * n_blocks)
    qk = [_dg(qs, flat(k_refs[i]), _NN) for i in pages]
    s = [qk[i] - slope * (past_len - i * page + t_q - tok).astype(f32) for i in pages]
    row_max = [jnp.max(s[i], axis=-1, keepdims=True) for i in pages]
    m_blk = [jnp.maximum(row_max[2 * n], row_max[2 * n + 1]) for n in blocks]
    p = [jnp.exp(s[i] - m_blk[i // 2]) for i in pages]
    row_sum = [jnp.sum(p[i], axis=-1, keepdims=True) for i in pages]
    l_blk = [row_sum[2 * n] + row_sum[2 * n + 1] for n in blocks]
    pv = [_dg(p[i].astype(bf16), flat(v_refs[i]), _NT) for i in pages]
    acc_blk = [pv[2 * n] + pv[2 * n + 1] for n in blocks]
    qk_sum = [jnp.sum(qk[i], axis=-1, keepdims=True) for i in pages]
    gates = [qk_sum[2 * n] + qk_sum[2 * n + 1] for n in blocks]

    t_k = lax.broadcasted_iota(jnp.int32, (1, kn_ref.shape[1]), 1)
    s_own = _dg(qs, kn_ref[0].astype(bf16), _NT) - slope * (t_q - t_k).astype(f32)
    s_own = jnp.where((t_k <= t_q) & (t_k < t_new), s_own, NEG_INF)
    m_tot = jnp.max(s_own, axis=-1, keepdims=True)
    chosen = []
    for n in range(n_blocks):
        rank = jnp.zeros_like(gates[n])
        for i in range(n_blocks):
            if i == n:
                continue
            ahead = (gates[i] > gates[n]) | ((gates[i] == gates[n]) & (i < n))
            rank = rank + ahead.astype(f32)
        chosen.append(rank < MOBA_TOPK)
    for n in range(n_blocks):
        m_tot = jnp.maximum(m_tot, jnp.where(chosen[n], m_blk[n], NEG_INF))
    p_own = jnp.exp(s_own - m_tot)
    l_tot = jnp.sum(p_own, axis=-1, keepdims=True)
    acc_tot = _dg(p_own.astype(bf16), vn_ref[0].astype(bf16), _NN)
    for n in range(n_blocks):
        wgt = jnp.where(chosen[n], jnp.exp(m_blk[n] - m_tot), 0.0)
        l_tot = l_tot + wgt * l_blk[n]
        acc_tot = acc_tot + wgt * acc_blk[n]
    own_head = (lax.broadcasted_iota(jnp.int32, (1, width), 1) // HEAD_DIM) == head
    out = jnp.where(own_head, acc_tot / l_tot, 0.0)
    o_ref[0] = jnp.sum(out.reshape(rows // n_heads, n_heads, width), axis=1)


def _moba_sample(q, k_new, v_new, cache_k, cache_v, page_table):
    batch, t_new, width = q.shape
    n_pool, page, n_heads, dim = cache_k.shape
    n_pages = page_table.shape[1]
    key_rows = 8
    assert dim == HEAD_DIM and width == n_heads * dim and MOBA_BLOCK == 2 * page and n_pages % 2 == 0
    assert t_new <= key_rows and n_heads & (n_heads - 1) == 0 and (t_new * n_heads) % 8 == 0
    n_blocks = n_pages // 2
    assert n_blocks >= MOBA_TOPK
    cache_kt = jnp.transpose(cache_k, (0, 2, 3, 1))
    cache_vt = jnp.transpose(cache_v, (0, 2, 3, 1))
    head_lanes = (jnp.arange(width)[None, :] // dim == jnp.arange(n_heads)[:, None]).astype(q.dtype)
    q_rows = (q[:, :, None, :] * head_lanes).reshape(batch, t_new * n_heads, width)
    pad_keys = lambda a: jnp.pad(a, ((0, 0), (0, key_rows - t_new), (0, 0)))

    def page_spec(i):
        return pl.BlockSpec((1, n_heads, dim, page), lambda b, pt: (pt[b, i], 0, 0, 0))

    pages = [page_spec(i) for i in range(n_pages)]
    rows_spec = lambda r: pl.BlockSpec((1, r, width), lambda b, pt: (b, 0, 0))
    return pl.pallas_call(
        functools.partial(_moba_sample_kernel, t_new, n_blocks, n_heads),
        grid_spec=pltpu.PrefetchScalarGridSpec(
            num_scalar_prefetch=1,
            grid=(batch,),
            in_specs=[rows_spec(t_new * n_heads), rows_spec(key_rows), rows_spec(key_rows)] + pages + pages,
            out_specs=rows_spec(t_new),
        ),
        out_shape=jax.ShapeDtypeStruct((batch, t_new, width), f32),
        compiler_params=_params("parallel"),
        name="moba_sample",
    )(page_table, q_rows, pad_keys(k_new), pad_keys(v_new), *([cache_kt] * n_pages), *([cache_vt] * n_pages))


def _rwkv_features(m, widths, w0_ref, w2_ref, a0_ref, a2_ref, g2_ref, kk_ref, ka_ref, ones_ref):
    rw, dl, al, gl = widths
    r = m[:, 0:rw]
    k = m[:, rw:2 * rw]
    v = m[:, 2 * rw:3 * rw]
    lora = m[:, 3 * rw:3 * rw + dl + al]
    gd = m[:, 3 * rw + dl + al:3 * rw + dl + al + gl]

    z = -(w0_ref[...] + _dot_hi(jnp.tanh(lora), w2_ref[...]))
    softplus = jnp.maximum(z, 0.0) + jnp.log(1.0 + jnp.exp(-jnp.abs(z)))
    w = -softplus - 0.5
    log_decay = -jnp.exp(w)
    a = jax.nn.sigmoid(a0_ref[...] + _dot_hi(lora, a2_ref[...]))
    g = _dot_hi(jax.nn.sigmoid(gd), g2_ref[...])
    kk = k * kk_ref[...]
    norm = jnp.sqrt(_head_sums(kk * kk, ones_ref[...]))
    kk = kk / jnp.maximum(norm, 1e-12)
    kh = k * (1.0 + (a - 1.0) * ka_ref[...])
    return [r, kh, v, kk, kk * a, log_decay, g]


def _rwkv_pre_kernel(t_valid, t_pad, widths, p_ref, prev_ref, mu_ref, *refs):
    const_refs, out_refs, carry = refs[:8], refs[8:15], refs[15]
    ti = pl.program_id(1)
    bt, tt, width = p_ref.shape
    rw = widths[0]

    @pl.when(ti == 0)
    def _():
        carry[...] = prev_ref[...]

    p = p_ref[...]
    t_idx = lax.broadcasted_iota(jnp.int32, (bt, tt, 1), 1)
    shifted = jnp.where(t_idx == 0, carry[...], pltpu.roll(p, 1, axis=1))
    carry[...] = p[:, tt - 1:tt, :]
    m = (p + (shifted - p) * mu_ref[...]).reshape(bt * tt, width)
    *outs, g = _rwkv_features(m, widths, *const_refs)
    if t_valid < t_pad:
        live = (lax.broadcasted_iota(jnp.int32, (bt, tt, 1), 1) + ti * tt < t_valid)
        live = live.astype(f32).reshape(bt * tt, 1)
        outs = [x * live for x in outs]
    shape3 = (bt, tt, rw)
    for o_ref, x in zip(out_refs, outs + [g]):
        o_ref[...] = x.reshape(shape3)


def _head_ones():
    idx = np.arange(MXU_DIM) // HEAD_DIM
    return jnp.asarray(idx[:, None] == idx[None, :], dtype=bf16)


def _rwkv_consts(weights):
    mu, w0, w2, a0, a2, g2, k_k, k_a = weights
    rw = w0.shape[-1]
    widths = (rw, w2.shape[0], a2.shape[0], g2.shape[0])
    assert 3 * rw + sum(widths[1:]) == mu.shape[-1]
    assert widths[1] + widths[2] == LANES and widths[3] % LANES == 0 and rw % MXU_DIM == 0
    w2p = jnp.pad(w2, ((0, widths[2]), (0, 0)))
    a2p = jnp.pad(a2, ((widths[1], 0), (0, 0)))
    return widths, (mu, w0, w2p, a0, a2p, g2, k_k, k_a, _head_ones())


def _rwkv_pre(p, prev, t_valid, weights, bt, tt):
    batch, t_pad, width = p.shape
    widths, consts = _rwkv_consts(weights)
    rw = widths[0]
    out_spec = pl.BlockSpec((bt, tt, rw), lambda b, t: (b, t, 0))
    return pl.pallas_call(
        functools.partial(_rwkv_pre_kernel, t_valid, t_pad, widths),
        grid=(batch // bt, t_pad // tt),
        in_specs=[pl.BlockSpec((bt, tt, width), lambda b, t: (b, t, 0)),
                  pl.BlockSpec((bt, 1, width), lambda b, t: (b, 0, 0))]
                 + [_const_spec(c.shape) for c in consts],
        out_specs=[out_spec] * 7,
        out_shape=[jax.ShapeDtypeStruct((batch, t_pad, rw), f32)] * 7,
        scratch_shapes=[pltpu.VMEM((bt, 1, width), f32)],
        compiler_params=_params("parallel", "arbitrary"),
        name="rwkv_pre",
    )(p, prev, *consts)


def _rwkv_scan_kernel(r_ref, k_ref, v_ref, kk_ref, b_ref, lw_ref, s0_ref, o_ref, s_out, state):
    ci = pl.program_id(1)
    bt, c, width = r_ref.shape
    n_pairs = width // LANES
    c2 = HEADS_PER_TILE * c
    n_double = int(math.log2(c))
    assert 1 << n_double == c and HEADS_PER_TILE == 2

    @pl.when(ci == 0)
    def _():
        state[...] = s0_ref[...]

    r2 = lax.broadcasted_iota(jnp.int32, (c2, c2), 0)
    q2 = lax.broadcasted_iota(jnp.int32, (c2, c2), 1)
    same_blk = (r2 >= c) == (q2 >= c)
    strict2 = (same_blk & (q2 < r2)).astype(f32)
    incl2 = (same_blk & (q2 <= r2)).astype(f32)
    eye2 = (q2 == r2).astype(f32)
    lane = lax.broadcasted_iota(jnp.int32, (1, LANES), 1)
    hm = [(lane // HEAD_DIM == j).astype(f32) for j in range(HEADS_PER_TILE)]
    bi_ = lax.broadcasted_iota(jnp.int32, (LANES, LANES), 0) // HEAD_DIM
    bj_ = lax.broadcasted_iota(jnp.int32, (LANES, LANES), 1) // HEAD_DIM
    block_diag = (bi_ == bj_).astype(f32)

    def stack(x):
        return jnp.concatenate([x * hm[0], x * hm[1]], axis=0)

    def twice(x):
        return jnp.concatenate([x, x], axis=0)

    def unstack(x2):
        return x2[:c] * hm[0] + x2[c:] * hm[1]

    probs = [(bi, pl.ds(pair * LANES, LANES), pair) for bi in range(bt) for pair in range(n_pairs)]
    ld = lambda ref: [ref[bi, :, sl] for bi, sl, _ in probs]
    r, k, v, kk, b, lw = ld(r_ref), ld(k_ref), ld(v_ref), ld(kk_ref), ld(b_ref), ld(lw_ref)
    nprob = len(probs)
    each = range(nprob)

    t_row = lax.broadcasted_iota(jnp.int32, (c, LANES), 0)
    cum = lw
    for step in range(n_double):
        cum = [cum[i] + jnp.where(t_row >= (1 << step), pltpu.roll(cum[i], 1 << step, axis=0), 0.0)
               for i in each]
    cum_end = [cum[i][c - 1:c, :] for i in each]
    e_neg = [jnp.exp(-cum[i]) for i in each]
    e_end = [jnp.exp(cum_end[i] - cum[i]) for i in each]
    a_mat = [kk[i] * jnp.exp(cum[i] - lw[i]) for i in each]
    p_mat = [r[i] * jnp.exp(cum[i]) for i in each]
    xa = [stack(a_mat[i]) for i in each]
    xap = [jnp.concatenate([xa[i], stack(p_mat[i])], axis=0) for i in each]
    yb = [stack(b[i] * e_neg[i]) for i in each]
    yk = [stack(k[i] * e_neg[i]) for i in each]
    v2 = [twice(v[i]) for i in each]

    gb = [_dot_lo(xap[i], yb[i], _NT) for i in each]
    gk = [_dot_lo(xap[i], yk[i], _NT) for i in each]
    l_mat = [gb[i][:c2] * strict2 for i in each]
    m_mat = [gk[i][:c2] * strict2 for i in each]
    lr_mat = [gb[i][c2:] * incl2 for i in each]
    mr_mat = [gk[i][c2:] * incl2 for i in each]

    npow = [-l_mat[i] for i in each]
    t_inv = [eye2 + npow[i] for i in each]
    for _ in range(n_double - 1):
        npow = [_dot_lo(npow[i], npow[i]) for i in each]
        t_inv = [t_inv[i] + _dot_lo(npow[i], t_inv[i]) for i in each]
    mv = [_dot_lo(m_mat[i], v2[i]) for i in each]
    tz = [_dot_lo(t_inv[i], jnp.concatenate([xa[i], mv[i]], axis=1)) for i in each]
    ta = [tz[i][:c, :LANES] + tz[i][c:, :LANES] for i in each]
    tmv = [unstack(tz[i][:, LANES:]) for i in each]

    s_prev = [state[bi, pair] for bi, _, pair in probs]
    xs = [_dot_lo(jnp.concatenate([ta[i], p_mat[i]], axis=0), s_prev[i], _NT) for i in each]
    u = [-(xs[i][:c] + tmv[i]) for i in each]
    corr = [_dot_lo(lr_mat[i], twice(u[i])) + _dot_lo(mr_mat[i], v2[i]) for i in each]
    upd = [_dot_lo(jnp.concatenate([u[i], v[i]], axis=0),
                   jnp.concatenate([b[i] * e_end[i], k[i] * e_end[i]], axis=0), _TN) for i in each]
    for i, (bi, sl, pair) in enumerate(probs):
        state[bi, pair] = s_prev[i] * jnp.exp(cum_end[i]) + upd[i] * block_diag
        o_ref[bi, :, sl] = xs[i][c:] + unstack(corr[i])

    @pl.when(ci == pl.num_programs(1) - 1)
    def _():
        s_out[...] = state[...]


def _rwkv_scan(r, k, v, kk, b, lw, s0, bt, chunk):
    batch, t_pad, width = r.shape
    n_pairs = width // LANES
    seq_spec = pl.BlockSpec((bt, chunk, width), lambda bb, c: (bb, c, 0))
    st_spec = pl.BlockSpec((bt, n_pairs, LANES, LANES), lambda bb, c: (bb, 0, 0, 0))
    return pl.pallas_call(
        _rwkv_scan_kernel,
        grid=(batch // bt, t_pad // chunk),
        in_specs=[seq_spec] * 6 + [st_spec],
        out_specs=[seq_spec, st_spec],
        out_shape=[jax.ShapeDtypeStruct((batch, t_pad, width), f32),
                   jax.ShapeDtypeStruct((batch, n_pairs, LANES, LANES), f32)],
        scratch_shapes=[pltpu.VMEM((bt, n_pairs, LANES, LANES), f32)],
        compiler_params=_params("parallel", "arbitrary"),
        name="rwkv_scan",
    )(r, k, v, kk, b, lw, s0)


def _pair_state(s):
    bsz, h, n, _ = s.shape
    s = s.reshape(bsz, h // HEADS_PER_TILE, HEADS_PER_TILE, n, n)
    eye = jnp.eye(HEADS_PER_TILE, dtype=s.dtype)
    out = jnp.einsum("bpjvk,ji->bpjvik", s, eye)
    return out.reshape(bsz, h // HEADS_PER_TILE, LANES, LANES)


def _unpair_state(s, heads):
    bsz = s.shape[0]
    s = s.reshape(bsz, heads // HEADS_PER_TILE, HEADS_PER_TILE, HEAD_DIM, HEADS_PER_TILE, HEAD_DIM)
    diag = jnp.stack([s[:, :, j, :, j, :] for j in range(HEADS_PER_TILE)], axis=2)
    return diag.reshape(bsz, heads, HEAD_DIM, HEAD_DIM)


STATE_ROWS = 8


def _rwkv_step_kernel(r_ref, k_ref, v_ref, kk_ref, b_ref, lw_ref, s_ref, o_ref, s_out):
    n_tok = r_ref.shape[0]
    n_val = s_ref.shape[1]
    decay = [jnp.exp(lw_ref[t, 0]) for t in range(n_tok)]

    def rows(c, _):
        base = pl.multiple_of(c * STATE_ROWS, STATE_ROWS)
        state = [s_ref[0, base + i] for i in range(STATE_ROWS)]
        for t in range(n_tok):
            kk, b, k, r = kk_ref[t, 0], b_ref[t, 0], k_ref[t, 0], r_ref[t, 0]
            v_rows = v_ref[t, 0, pl.ds(base, STATE_ROWS), :]
            out = []
            for i in range(STATE_ROWS):
                s_kk = jnp.sum(state[i] * kk, axis=0, keepdims=True)
                state[i] = state[i] * decay[t] - s_kk * b + v_rows[i:i + 1] * k
                out.append(jnp.sum(state[i] * r, axis=0, keepdims=True))
            o_ref[t, 0, pl.ds(base, STATE_ROWS), :] = jnp.concatenate(out, axis=0)
        for i in range(STATE_ROWS):
            s_out[0, base + i] = state[i]
        return 0

    lax.fori_loop(0, n_val // STATE_ROWS, rows, 0)


def _rwkv_step(r, k, v, kk, b, lw, s0):
    n_tok, n_heads, dim, batch = r.shape
    assert s0.shape == (n_heads, dim, dim, batch) and dim % STATE_ROWS == 0
    seq_spec = pl.BlockSpec((n_tok, 1, dim, batch), lambda h: (0, h, 0, 0))
    st_spec = pl.BlockSpec((1, dim, dim, batch), lambda h: (h, 0, 0, 0))
    return pl.pallas_call(
        _rwkv_step_kernel,
        grid=(n_heads,),
        in_specs=[seq_spec] * 6 + [st_spec],
        out_specs=[seq_spec, st_spec],
        out_shape=[jax.ShapeDtypeStruct(r.shape, f32), jax.ShapeDtypeStruct(s0.shape, f32)],
        compiler_params=_params("parallel"),
        name="rwkv_step",
    )(r, k, v, kk, b, lw, s0)


def _merge_kernel(h_ref, oa_ref, o_ref, r_ref, k_ref, v_ref, g_ref, ga_ref, gb_ref,
                  rk_ref, lnw_ref, lnb_ref, ones_ref, wba_ref, wbb_ref, wout_ref, gpost_ref,
                  out_ref):
    ones = ones_ref[...]
    inv = 1.0 / HEAD_DIM
    branch_a = ga_ref[...] * jnp.dot(oa_ref[...].astype(bf16), wba_ref[...], preferred_element_type=f32)
    o = o_ref[...]
    rkr = r_ref[...] * k_ref[...] * rk_ref[...]
    mean = _head_sums(o, ones) * inv
    bonus = _head_sums(rkr, ones) * v_ref[...]
    d = o - mean
    var = _head_sums(d * d, ones) * inv
    on = d * lax.rsqrt(var + GN_EPS) * lnw_ref[...] + lnb_ref[...]
    ob = (on + bonus) * g_ref[...]
    merged = branch_a + gb_ref[...] * jnp.dot(ob.astype(bf16), wbb_ref[...], preferred_element_type=f32)
    y = jnp.dot(merged.astype(bf16), wout_ref[...], preferred_element_type=f32)
    out_ref[...] = h_ref[...] + _rms(y, gpost_ref[...])


def _merge(h, oa, o, r, k, v, g, ga, gb, consts, tm):
    n, d = h.shape
    w = oa.shape[1]
    row = lambda width: pl.BlockSpec((tm, width), lambda i: (i, 0))
    return pl.pallas_call(
        _merge_kernel,
        grid=(n // tm,),
        in_specs=[row(d)] + [row(w)] * 6 + [row(d)] * 2 + [_const_spec(c.shape) for c in consts],
        out_specs=row(d),
        out_shape=jax.ShapeDtypeStruct((n, d), f32),
        compiler_params=_params("parallel"),
        name="merge",
    )(h, oa, o, r, k, v, g, ga, gb, *consts)


def _row_tile(n, target):
    t = min(n, target)
    assert n % t == 0
    return t


def _layer(x, prev_shift, wkv0, past, w, n_heads):
    (g1a, g1b, f1g, f1u, f1d, gma, gmb, w_in, w_ba, w_bb, w_out,
     mu, w0, w2, a0, a2, g2, k_k, k_a, r_k, ln_w, ln_b,
     g2a, g2b, f2g, f2u, f2d) = w
    batch, t, d = x.shape
    n = batch * t
    mw = w_ba.shape[0]
    rw = w_bb.shape[0]
    rp = mu.shape[-1]
    widths = (mw, mw, mw, rp, d, d)
    row = lambda a: a.reshape(1, -1)

    x2 = x.reshape(n, d)
    h = _ffn(x2, row(g1a), row(g1b), f1g, f1u, f1d, _row_tile(n, 512))
    kv_shape = (batch, t, n_heads, HEAD_DIM)
    pre_w = (row(mu), row(w0), w2, row(a0), a2, g2, row(k_k), row(k_a))
    prev = prev_shift.reshape(batch, 1, rp)
    if past is None:
        assert t % MOBA_BLOCK == 0 and t % RWKV_CHUNK == 0
        q, p_rw, ga, gb, k_t, v_t, k_bf, vt_bf, kmean, *pre = _in_proj_seq(
            h, row(gma), w_in, widths, t, prev, pre_w)
        o_a = _moba_prompt(q, k_bf, vt_bf, kmean.reshape(n // MOBA_BLOCK, mw), batch, t, n_heads)
        k, v = (jnp.transpose(a.reshape(batch, n_heads, HEAD_DIM, t), (0, 3, 1, 2)) for a in (k_t, v_t))
        r_, kh, v_, g_ = (pre[i] for i in (0, 1, 2, 6))
        o_scan, s_fin = _rwkv_scan(*(a.reshape(batch, t, rw) for a in pre[:6]),
                                   _pair_state(wkv0.astype(f32)), min(batch, 2), RWKV_CHUNK)
        o_scan = o_scan.reshape(n, rw)
        wkv = _unpair_state(s_fin, rw // HEAD_DIM).astype(wkv0.dtype)
    else:
        cache_k, cache_v, page_table = past
        q, p_rw, ga, gb, k, v = _in_proj(h, row(gma), w_in, widths, _row_tile(n, MOBA_BLOCK))
        rows3 = lambda a: a.reshape(batch, t, mw)
        o_a = _moba_sample(rows3(q), rows3(k), rows3(v), cache_k, cache_v, page_table).reshape(n, mw)
        k, v = k.reshape(kv_shape), v.reshape(kv_shape)
        t_pad = -(-t // 8) * 8
        p3p = jnp.pad(p_rw.reshape(batch, t, rp), ((0, 0), (0, t_pad - t), (0, 0)))
        pre = _rwkv_pre(p3p, prev, t, pre_w, min(batch, 16), t_pad)
        r_, kh, v_, g_ = (pre[i][:, :t].reshape(n, rw) for i in (0, 1, 2, 6))
        heads = rw // HEAD_DIM
        lanes_last = lambda a: jnp.transpose(a[:, :t], (1, 2, 0)).reshape(t, heads, HEAD_DIM, batch)
        o_scan, s_fin = _rwkv_step(*(lanes_last(a) for a in pre[:6]),
                                   jnp.transpose(wkv0.astype(f32), (1, 2, 3, 0)))
        o_scan = jnp.transpose(o_scan.reshape(t, rw, batch), (2, 0, 1)).reshape(n, rw)
        wkv = jnp.transpose(s_fin, (3, 0, 1, 2)).astype(wkv0.dtype)
    p3 = p_rw.reshape(batch, t, rp)
    consts = (row(r_k), row(ln_w), row(ln_b), _head_ones(), w_ba, w_bb, w_out, row(gmb))
    h2 = _merge(h, o_a, o_scan, r_, kh, v_, g_, ga, gb, consts, _row_tile(n, 512))
    y = _ffn(h2, row(g2a), row(g2b), f2g, f2u, f2d, _row_tile(n, 512))
    return y.reshape(batch, t, d), k, v, wkv, p3[:, -1]


def kernel(x_prompt, x_sample, cache_k, cache_v, state_wkv, state_shift, page_table,
           g_ffn1_pre, g_ffn1_post, w_ffn1_gate, w_ffn1_up, w_ffn1_down,
           g_mix_pre, g_mix_post, w_in, w_branch_a, w_branch_b, w_out,
           rwkv_mu, rwkv_w0, rwkv_w2, rwkv_a0, rwkv_a2, rwkv_g2, rwkv_k_k, rwkv_k_a, rwkv_r_k,
           rwkv_ln_w, rwkv_ln_b,
           g_ffn2_pre, g_ffn2_post, w_ffn2_gate, w_ffn2_up, w_ffn2_down):
    depth = w_in.shape[0]
    bp = x_prompt.shape[0]
    n_heads = cache_k.shape[3]
    rp = rwkv_mu.shape[-1]
    rw = rwkv_w0.shape[-1]
    yp, ys = x_prompt, x_sample
    outs = [[] for _ in range(8)]
    for i in range(depth):
        cast = lambda a: a[i].astype(bf16)
        w = (g_ffn1_pre[i], g_ffn1_post[i], cast(w_ffn1_gate), cast(w_ffn1_up), cast(w_ffn1_down),
             g_mix_pre[i], g_mix_post[i], cast(w_in), cast(w_branch_a), cast(w_branch_b), cast(w_out),
             rwkv_mu[i], rwkv_w0[i], rwkv_w2[i], rwkv_a0[i], rwkv_a2[i], rwkv_g2[i], rwkv_k_k[i],
             rwkv_k_a[i], rwkv_r_k[i], rwkv_ln_w[i], rwkv_ln_b[i],
             g_ffn2_pre[i], g_ffn2_post[i], cast(w_ffn2_gate), cast(w_ffn2_up), cast(w_ffn2_down))
        yp, kp, vp, wkvp, shp = _layer(
            yp, jnp.zeros((bp, rp), x_prompt.dtype),
            jnp.zeros((bp, rw // HEAD_DIM, HEAD_DIM, HEAD_DIM), state_wkv.dtype), None, w, n_heads)
        ys, ksn, vsn, wkvs, shs = _layer(
            ys, state_shift[i], state_wkv[i], (cache_k[i], cache_v[i], page_table), w, n_heads)
        for lst, val in zip(outs, (kp, vp, ksn, vsn, wkvp, shp, wkvs, shs)):
            lst.append(val)
    return (yp, ys) + tuple(jnp.stack(lst) for lst in outs)
```

```python
import functools
import math

import jax
import jax.numpy as jnp
import numpy as np
from jax import lax
from jax.experimental import pallas as pl
from jax.experimental.pallas import tpu as pltpu

HEAD_DIM = 64
LANES = 128
HEADS_PER_TILE = LANES // HEAD_DIM
MXU_DIM = 256
MOBA_BLOCK = 256
MOBA_TOPK = 3
SUM_ROWS = 16
PIPE_UNROLL = 4
RWKV_CHUNK = 64
MOBA_GROUP_HEADS = 4
RMS_EPS = 1e-6
GN_EPS = 64e-5
VMEM_LIMIT = 56 * 1024 * 1024
NEG_INF = float("-inf")
POS_INF = float("inf")
LOG2E = 1.4426950408889634

f32 = jnp.float32
bf16 = jnp.bfloat16


def _params(*sem):
    return pltpu.CompilerParams(dimension_semantics=sem, vmem_limit_bytes=VMEM_LIMIT)


def _const_spec(shape):
    nd = len(shape)
    return pl.BlockSpec(shape, lambda *_: (0,) * nd, pipeline_mode=pl.Buffered(1))


def _rms(x, g):
    return x * lax.rsqrt(jnp.mean(x * x, axis=-1, keepdims=True) + RMS_EPS) * g


def _split3(x):
    hi = x.astype(bf16)
    r1 = x - hi.astype(f32)
    mid = r1.astype(bf16)
    lo = (r1 - mid.astype(f32)).astype(bf16)
    return hi, mid, lo


_NN = (((1,), (0,)), ((), ()))
_NT = (((1,), (1,)), ((), ()))
_TN = (((0,), (0,)), ((), ()))


def _dg(a, b, dims):
    return lax.dot_general(a, b, dims, preferred_element_type=f32)


def _dot_lo(a, b, dims=_NN):
    return _dg(a.astype(bf16), b.astype(bf16), dims)


def _dot_hi(a, b, dims=_NN):
    ah, am, _ = _split3(a)
    bh, bm, _ = _split3(b)
    return _dg(ah, bh, dims) + (_dg(ah, bm, dims) + _dg(am, bh, dims))


def _dot_exact_rhs(a, b_bf, dims=_NN):
    ah = a.astype(bf16)
    am = (a - ah.astype(f32)).astype(bf16)
    return _dg(ah, b_bf, dims) + _dg(am, b_bf, dims)


def _head_sums(x, ones_bf):
    g = ones_bf.shape[0]
    return jnp.concatenate([_dot_exact_rhs(x[:, c:c + g], ones_bf) for c in range(0, x.shape[1], g)],
                           axis=1)


def _ffn_kernel(x_ref, gpre_ref, gpost_ref, wg_ref, wu_ref, wd_ref, o_ref):
    half = x_ref.shape[0] // 2
    for rows in (pl.ds(0, half), pl.ds(half, x_ref.shape[0] - half)):
        x = x_ref[rows, :]
        h = _rms(x, gpre_ref[...]).astype(bf16)
        gate = jnp.dot(h, wg_ref[...], preferred_element_type=f32)
        up = jnp.dot(h, wu_ref[...], preferred_element_type=f32)
        act = (gate * jax.nn.sigmoid(gate) * up).astype(bf16)
        y = jnp.dot(act, wd_ref[...], preferred_element_type=f32)
        o_ref[rows, :] = x + 0.5 * _rms(y, gpost_ref[...])


def _ffn(x, g_pre, g_post, wg, wu, wd, tm):
    n, d = x.shape
    dff = wg.shape[1]
    return pl.pallas_call(
        _ffn_kernel,
        grid=(n // tm,),
        in_specs=[
            pl.BlockSpec((tm, d), lambda i: (i, 0)),
            _const_spec((1, d)), _const_spec((1, d)),
            _const_spec((d, dff)), _const_spec((d, dff)), _const_spec((dff, d)),
        ],
        out_specs=pl.BlockSpec((tm, d), lambda i: (i, 0)),
        out_shape=jax.ShapeDtypeStruct((n, d), f32),
        compiler_params=_params("parallel"),
        name="ffn",
    )(x, g_pre, g_post, wg, wu, wd)


def _in_proj_kernel(splits, h_ref, g_ref, w_ref, q_ref, p_ref, ga_ref, gb_ref, k_ref, v_ref):
    u = _rms(h_ref[...], g_ref[...]).astype(bf16)
    proj = lambda idx: _dg(u, w_ref[:, splits[idx]:splits[idx + 1]], _NN)
    q_ref[...] = proj(0)
    k_ref[...] = proj(1)
    v_ref[...] = proj(2)
    p_ref[...] = proj(3)
    ga_ref[...] = jax.nn.sigmoid(proj(4))
    gb_ref[...] = jax.nn.sigmoid(proj(5))


def _in_proj_seq_kernel(splits, blocks_per_seq, rwkv_widths, h_ref, g_ref, w_ref, prev_ref, mu_ref, *refs):
    const_refs, out_refs, carry = refs[:8], refs[8:-1], refs[-1]
    q_ref, p_ref, ga_ref, gb_ref, kt_ref, vt_ref, kb_ref, vtb_ref, km_ref = out_refs[:9]
    @pl.when(pl.program_id(0) % blocks_per_seq == 0)
    def _():
        carry[...] = prev_ref[0]

    u = _rms(h_ref[...], g_ref[...]).astype(bf16)
    proj = lambda idx: _dg(u, w_ref[:, splits[idx]:splits[idx + 1]], _NN)
    p = proj(3)
    p_ref[...] = p
    rows = p.shape[0]
    first = lax.broadcasted_iota(jnp.int32, (rows, 1), 0) == 0
    shifted = jnp.where(first, carry[...], pltpu.roll(p, 1, axis=0))
    carry[...] = p[rows - 1:rows, :]
    feats = _rwkv_features(p + (shifted - p) * mu_ref[...], rwkv_widths, *const_refs)
    for o_ref, x in zip(out_refs[9:], feats):
        o_ref[...] = x

    k, v = proj(1), proj(2)
    v_t = v.T
    kt_ref[0] = k.T
    vt_ref[0] = v_t
    kb_ref[...] = k.astype(bf16)
    vtb_ref[0] = v_t.astype(bf16)
    km_ref[0] = jnp.mean(k, axis=0, keepdims=True)
    q_ref[...] = proj(0)
    ga_ref[...] = jax.nn.sigmoid(proj(4))
    gb_ref[...] = jax.nn.sigmoid(proj(5))


def _proj_splits(widths, w_in):
    splits = tuple(int(s) for s in np.concatenate([[0], np.cumsum(widths)]))
    assert splits[-1] == w_in.shape[1] and all(s % LANES == 0 for s in splits)
    return splits


def _in_proj(h, g, w_in, widths, tm):
    n, d = h.shape
    order = (0, 3, 4, 5, 1, 2)
    return pl.pallas_call(
        functools.partial(_in_proj_kernel, _proj_splits(widths, w_in)),
        grid=(n // tm,),
        in_specs=[
            pl.BlockSpec((tm, d), lambda i: (i, 0)),
            _const_spec((1, d)),
            _const_spec(w_in.shape),
        ],
        out_specs=[pl.BlockSpec((tm, widths[j]), lambda i: (i, 0)) for j in order],
        out_shape=[jax.ShapeDtypeStruct((n, widths[j]), f32) for j in order],
        compiler_params=_params("parallel"),
        name="in_proj",
    )(h, g, w_in)


def _in_proj_seq(h, g, w_in, widths, seq, prev, rwkv_weights):
    n, d = h.shape
    tm = MOBA_BLOCK
    assert seq % tm == 0
    nbs = seq // tm
    mw = widths[1]
    rwkv_widths, (mu, *consts) = _rwkv_consts(rwkv_weights)
    rw, rp = rwkv_widths[0], mu.shape[-1]
    row = lambda w, dt=f32: (pl.BlockSpec((tm, w), lambda i: (i, 0)), jax.ShapeDtypeStruct((n, w), dt))
    t_spec = pl.BlockSpec((1, mw, tm), lambda i: (i // nbs, 0, i % nbs))
    t_shape = jax.ShapeDtypeStruct((n // seq, mw, seq), f32)
    outs = [row(widths[0]), row(widths[3]), row(widths[4]), row(widths[5]),
            (t_spec, t_shape), (t_spec, t_shape), row(mw, bf16),
            (pl.BlockSpec((1, mw, tm), lambda i: (i, 0, 0)), jax.ShapeDtypeStruct((n // tm, mw, tm), bf16)),
            (pl.BlockSpec((1, 1, mw), lambda i: (i, 0, 0)), jax.ShapeDtypeStruct((n // tm, 1, mw), f32))]
    outs += [row(rw)] * 7
    return pl.pallas_call(
        functools.partial(_in_proj_seq_kernel, _proj_splits(widths, w_in), nbs, rwkv_widths),
        grid=(n // tm,),
        in_specs=[
            pl.BlockSpec((tm, d), lambda i: (i, 0)),
            _const_spec((1, d)),
            _const_spec(w_in.shape),
            pl.BlockSpec((1, 1, rp), lambda i: (i // nbs, 0, 0)),
            _const_spec(mu.shape),
        ] + [_const_spec(c.shape) for c in consts],
        out_specs=[o[0] for o in outs],
        out_shape=[o[1] for o in outs],
        scratch_shapes=[pltpu.VMEM((1, rp), f32)],
        compiler_params=_params("arbitrary"),
        name="in_proj_seq",
    )(h, g, w_in, prev, mu, *consts)


def _top_blocks(gate, topk):
    blk = lax.broadcasted_iota(jnp.int32, gate.shape, 0).astype(f32)
    sel = jnp.zeros(gate.shape, f32)
    big = float(gate.shape[0])
    for _ in range(topk):
        m = jnp.max(gate, axis=0, keepdims=True)
        cand = (gate == m) & (gate > NEG_INF)
        idx = jnp.min(jnp.where(cand, blk, big), axis=0, keepdims=True)
        hit = blk == idx
        sel = jnp.where(hit, 1.0, sel)
        gate = jnp.where(hit, NEG_INF, gate)
    return sel


def _moba_prompt_kernel(n_heads, q_ref, kb_ref, vt_ref, km_ref, o_ref, off_sc, s_sc):
    group = pl.program_id(1)
    qi = pl.program_id(2)
    tq = q_ref.shape[0]
    nb = km_ref.shape[0]
    heads = q_ref.shape[1] // HEAD_DIM
    tile = lambda j: pl.ds((j // HEADS_PER_TILE) * LANES, LANES)
    lane = lax.broadcasted_iota(jnp.int32, (1, LANES), 1)
    key_i = lax.broadcasted_iota(jnp.int32, (MOBA_BLOCK, tq), 0)
    qry_i = lax.broadcasted_iota(jnp.int32, (MOBA_BLOCK, tq), 1)
    rel = (qry_i - key_i).astype(f32)
    causal = key_i <= qry_i
    blk_row = lax.broadcasted_iota(jnp.int32, (nb, tq), 0)

    own = pl.ds(pl.multiple_of(qi * MOBA_BLOCK, MOBA_BLOCK), MOBA_BLOCK)
    ones_rows = jnp.ones((SUM_ROWS, MOBA_BLOCK), bf16)
    with_sum = lambda vt: jnp.concatenate([vt, ones_rows], axis=0)

    hs = range(heads)
    q = [q_ref[:, tile(j)] for j in hs]
    head_mask = [(lane // HEAD_DIM == j % HEADS_PER_TILE).astype(f32) for j in hs]
    head_no = [jnp.full((1, 1), group * heads + j + 1, jnp.int32).astype(f32) for j in hs]
    slope2 = [jnp.exp2(head_no[j] * (-8.0 / n_heads)) * LOG2E for j in hs]
    qs = [(q[j] * head_mask[j] * (HEAD_DIM ** -0.5 * LOG2E)).astype(bf16) for j in hs]
    qk_own = [_dg(kb_ref[own, tile(j)], qs[j], _NT) for j in hs]
    gate = [_dot_hi(km_ref[:, tile(j)] * head_mask[j], q[j], _NT) for j in hs]
    alibi = [slope2[j] * rel for j in hs]
    s = [jnp.where(causal, qk_own[j] - alibi[j], NEG_INF) for j in hs]
    m = [jnp.max(s[j], axis=0, keepdims=True) for j in hs]
    p = [jnp.exp2(s[j] - m[j]).astype(bf16) for j in hs]
    acc = [_dg(with_sum(vt_ref[qi, tile(j), :]), p[j], _NN) for j in hs]
    for j in hs:
        picked = _top_blocks(jnp.where(blk_row < qi, gate[j], NEG_INF), MOBA_TOPK)
        off_sc[j] = jnp.where(picked > 0.5, 0.0, POS_INF)
    consts = [(qs[j], alibi[j], slope2[j]) for j in hs]
    carry = [x for j in hs for x in (m[j], acc[j])]

    def score(n, slot):
        nc = jnp.minimum(n, nb - 1)
        rows = pl.ds(pl.multiple_of(nc * MOBA_BLOCK, MOBA_BLOCK), MOBA_BLOCK)
        qk = [_dg(kb_ref[rows, tile(j)], consts[j][0], _NT) for j in hs]
        s = [qk[j] - consts[j][1] for j in hs]
        for j in hs:
            s_sc[slot, j] = s[j]
        return [jnp.max(s[j], axis=0, keepdims=True) for j in hs]

    def attend(n, slot, raw, stats):
        nc = jnp.minimum(n, nb - 1)
        m, acc = stats[0::2], stats[1::2]
        off = [jnp.where(n < qi, off_sc[j, pl.ds(nc, 1), :], POS_INF) for j in hs]
        far = [((qi - n) * MOBA_BLOCK).astype(f32) * consts[j][2] for j in hs]
        m_new = [jnp.maximum(m[j], raw[j] - far[j] - off[j]) for j in hs]
        alpha = [jnp.exp2(m[j] - m_new[j]) for j in hs]
        p = [jnp.exp2(s_sc[slot, j] - (m_new[j] + far[j] + off[j])).astype(bf16) for j in hs]
        pv = [_dg(with_sum(vt_ref[nc, tile(j), :]), p[j], _NN) for j in hs]
        return [x for j in hs for x in (m_new[j], acc[j] * alpha[j] + pv[j])]

    def body(i, carry):
        stats, raw = list(carry[:-heads]), list(carry[-heads:])
        for u in range(PIPE_UNROLL):
            n = PIPE_UNROLL * i + u
            raw_next = score(n + 1, (u + 1) % 2)
            stats = attend(n, u % 2, raw, stats)
            raw = raw_next
        return tuple(stats + raw)

    trips = (qi + PIPE_UNROLL - 1) // PIPE_UNROLL
    carry = lax.fori_loop(0, trips, body, tuple(carry + score(0, 0)))
    accs = [carry[2 * j + 1] for j in range(heads)]
    in_tile = lambda j: slice((j % HEADS_PER_TILE) * HEAD_DIM, (j % HEADS_PER_TILE + 1) * HEAD_DIM)
    out_t = jnp.concatenate([accs[j][in_tile(j)] / accs[j][LANES:LANES + 1] for j in range(heads)], axis=0)
    o_ref[...] = out_t.T


def _moba_prompt(q, k_bf, v_t, kmean, batch, seq, n_heads):
    n, w = q.shape
    wg = MOBA_GROUP_HEADS * HEAD_DIM
    assert w % wg == 0 and wg % LANES == 0
    nb = seq // MOBA_BLOCK
    tq = MOBA_BLOCK
    return pl.pallas_call(
        functools.partial(_moba_prompt_kernel, n_heads),
        grid=(batch, w // wg, nb),
        in_specs=[
            pl.BlockSpec((tq, wg), lambda b, p, i: (b * nb + i, p)),
            pl.BlockSpec((seq, wg), lambda b, p, i: (b, p)),
            pl.BlockSpec((nb, wg, MOBA_BLOCK), lambda b, p, i: (b, p, 0)),
            pl.BlockSpec((nb, wg), lambda b, p, i: (b, p)),
        ],
        out_specs=pl.BlockSpec((tq, wg), lambda b, p, i: (b * nb + i, p)),
        out_shape=jax.ShapeDtypeStruct((n, w), f32),
        scratch_shapes=[pltpu.VMEM((MOBA_GROUP_HEADS, nb, tq), f32),
                        pltpu.VMEM((2, MOBA_GROUP_HEADS, MOBA_BLOCK, tq), f32)],
        compiler_params=_params("parallel", "parallel", "arbitrary"),
        name="moba_prompt",
    )(q, k_bf, v_t, kmean)


def _moba_sample_kernel(t_new, n_blocks, n_heads, pt_ref, q_ref, kn_ref, vn_ref, *refs):
    del pt_ref
    k_refs, v_refs, o_ref = refs[:2 * n_blocks], refs[2 * n_blocks:4 * n_blocks], refs[4 * n_blocks]
    rows, width = q_ref.shape[1], q_ref.shape[2]
    page = k_refs[0].shape[3]
    head_bits = n_heads.bit_length() - 1
    r_idx = lax.broadcasted_iota(jnp.int32, (rows, 1), 0)
    head = r_idx & (n_heads - 1)
    t_q = r_idx >> head_bits
    slope = jnp.exp2((head + 1).astype(f32) * (-8.0 / n_heads))
    qs = (q_ref[0] * (HEAD_DIM ** -0.5)).astype(bf16)
    tok = lax.broadcasted_iota(jnp.int32, (1, page), 1)
    past_len = n_blocks * MOBA_BLOCK
    flat = lambda ref: ref[0].reshape(width, page).astype(bf16)

    blocks, pages = range(n_blocks), range(2 * n_blocks)
    qk = [_dg(qs, flat(k_refs[i]), _NN) for i in pages]
    s = [qk[i] - slope * (past_len - i * page + t_q - tok).astype(f32) for i in pages]
    row_max = [jnp.max(s[i], axis=-1, keepdims=True) for i in pages]
    m_blk = [jnp.maximum(row_max[2 * n], row_max[2 * n + 1]) for n in blocks]
    p = [jnp.exp(s[i] - m_blk[i // 2]) for i in pages]
    row_sum = [jnp.sum(p[i], axis=-1, keepdims=True) for i in pages]
    l_blk = [row_sum[2 * n] + row_sum[2 * n + 1] for n in blocks]
    pv = [_dg(p[i].astype(bf16), flat(v_refs[i]), _NT) for i in pages]
    acc_blk = [pv[2 * n] + pv[2 * n + 1] for n in blocks]
    qk_sum = [jnp.sum(qk[i], axis=-1, keepdims=True) for i in pages]
    gates = [qk_sum[2 * n] + qk_sum[2 * n + 1] for n in blocks]

    t_k = lax.broadcasted_iota(jnp.int32, (1, kn_ref.shape[1]), 1)
    s_own = _dg(qs, kn_ref[0].astype(bf16), _NT) - slope * (t_q - t_k).astype(f32)
    s_own = jnp.where((t_k <= t_q) & (t_k < t_new), s_own, NEG_INF)
    m_tot = jnp.max(s_own, axis=-1, keepdims=True)
    chosen = []
    for n in range(n_blocks):
        rank = jnp.zeros_like(gates[n])
        for i in range(n_blocks):
            if i == n:
                continue
            ahead = (gates[i] > gates[n]) | ((gates[i] == gates[n]) & (i < n))
            rank = rank + ahead.astype(f32)
        chosen.append(rank < MOBA_TOPK)
    for n in range(n_blocks):
        m_tot = jnp.maximum(m_tot, jnp.where(chosen[n], m_blk[n], NEG_INF))
    p_own = jnp.exp(s_own - m_tot)
    l_tot = jnp.sum(p_own, axis=-1, keepdims=True)
    acc_tot = _dg(p_own.astype(bf16), vn_ref[0].astype(bf16), _NN)
    for n in range(n_blocks):
        wgt = jnp.where(chosen[n], jnp.exp(m_blk[n] - m_tot), 0.0)
        l_tot = l_tot + wgt * l_blk[n]
        acc_tot = acc_tot + wgt * acc_blk[n]
    own_head = (lax.broadcasted_iota(jnp.int32, (1, width), 1) // HEAD_DIM) == head
    out = jnp.where(own_head, acc_tot / l_tot, 0.0)
    o_ref[0] = jnp.sum(out.reshape(rows // n_heads, n_heads, width), axis=1)


def _moba_sample(q, k_new, v_new, cache_k, cache_v, page_table):
    batch, t_new, width = q.shape
    n_pool, page, n_heads, dim = cache_k.shape
    n_pages = page_table.shape[1]
    key_rows = 8
    assert dim == HEAD_DIM and width == n_heads * dim and MOBA_BLOCK == 2 * page and n_pages % 2 == 0
    assert t_new <= key_rows and n_heads & (n_heads - 1) == 0 and (t_new * n_heads) % 8 == 0
    n_blocks = n_pages // 2
    assert n_blocks >= MOBA_TOPK
    cache_kt = jnp.transpose(cache_k, (0, 2, 3, 1))
    cache_vt = jnp.transpose(cache_v, (0, 2, 3, 1))
    head_lanes = (jnp.arange(width)[None, :] // dim == jnp.arange(n_heads)[:, None]).astype(q.dtype)
    q_rows = (q[:, :, None, :] * head_lanes).reshape(batch, t_new * n_heads, width)
    pad_keys = lambda a: jnp.pad(a, ((0, 0), (0, key_rows - t_new), (0, 0)))

    def page_spec(i):
        return pl.BlockSpec((1, n_heads, dim, page), lambda b, pt: (pt[b, i], 0, 0, 0))

    pages = [page_spec(i) for i in range(n_pages)]
    rows_spec = lambda r: pl.BlockSpec((1, r, width), lambda b, pt: (b, 0, 0))
    return pl.pallas_call(
        functools.partial(_moba_sample_kernel, t_new, n_blocks, n_heads),
        grid_spec=pltpu.PrefetchScalarGridSpec(
            num_scalar_prefetch=1,
            grid=(batch,),
            in_specs=[rows_spec(t_new * n_heads), rows_spec(key_rows), rows_spec(key_rows)] + pages + pages,
            out_specs=rows_spec(t_new),
        ),
        out_shape=jax.ShapeDtypeStruct((batch, t_new, width), f32),
        compiler_params=_params("parallel"),
        name="moba_sample",
    )(page_table, q_rows, pad_keys(k_new), pad_keys(v_new), *([cache_kt] * n_pages), *([cache_vt] * n_pages))


def _rwkv_features(m, widths, w0_ref, w2_ref, a0_ref, a2_ref, g2_ref, kk_ref, ka_ref, ones_ref):
    rw, dl, al, gl = widths
    r = m[:, 0:rw]
    k = m[:, rw:2 * rw]
    v = m[:, 2 * rw:3 * rw]
    lora = m[:, 3 * rw:3 * rw + dl + al]
    gd = m[:, 3 * rw + dl + al:3 * rw + dl + al + gl]

    z = -(w0_ref[...] + _dot_hi(jnp.tanh(lora), w2_ref[...]))
    softplus = jnp.maximum(z, 0.0) + jnp.log(1.0 + jnp.exp(-jnp.abs(z)))
    w = -softplus - 0.5
    log_decay = -jnp.exp(w)
    a = jax.nn.sigmoid(a0_ref[...] + _dot_hi(lora, a2_ref[...]))
    g = _dot_hi(jax.nn.sigmoid(gd), g2_ref[...])
    kk = k * kk_ref[...]
    norm = jnp.sqrt(_head_sums(kk * kk, ones_ref[...]))
    kk = kk / jnp.maximum(norm, 1e-12)
    kh = k * (1.0 + (a - 1.0) * ka_ref[...])
    return [r, kh, v, kk, kk * a, log_decay, g]


def _rwkv_pre_kernel(t_valid, t_pad, widths, p_ref, prev_ref, mu_ref, *refs):
    const_refs, out_refs, carry = refs[:8], refs[8:15], refs[15]
    ti = pl.program_id(1)
    bt, tt, width = p_ref.shape
    rw = widths[0]

    @pl.when(ti == 0)
    def _():
        carry[...] = prev_ref[...]

    p = p_ref[...]
    t_idx = lax.broadcasted_iota(jnp.int32, (bt, tt, 1), 1)
    shifted = jnp.where(t_idx == 0, carry[...], pltpu.roll(p, 1, axis=1))
    carry[...] = p[:, tt - 1:tt, :]
    m = (p + (shifted - p) * mu_ref[...]).reshape(bt * tt, width)
    *outs, g = _rwkv_features(m, widths, *const_refs)
    if t_valid < t_pad:
        live = (lax.broadcasted_iota(jnp.int32, (bt, tt, 1), 1) + ti * tt < t_valid)
        live = live.astype(f32).reshape(bt * tt, 1)
        outs = [x * live for x in outs]
    shape3 = (bt, tt, rw)
    for o_ref, x in zip(out_refs, outs + [g]):
        o_ref[...] = x.reshape(shape3)


def _head_ones():
    idx = np.arange(MXU_DIM) // HEAD_DIM
    return jnp.asarray(idx[:, None] == idx[None, :], dtype=bf16)


def _rwkv_consts(weights):
    mu, w0, w2, a0, a2, g2, k_k, k_a = weights
    rw = w0.shape[-1]
    widths = (rw, w2.shape[0], a2.shape[0], g2.shape[0])
    assert 3 * rw + sum(widths[1:]) == mu.shape[-1]
    assert widths[1] + widths[2] == LANES and widths[3] % LANES == 0 and rw % MXU_DIM == 0
    w2p = jnp.pad(w2, ((0, widths[2]), (0, 0)))
    a2p = jnp.pad(a2, ((widths[1], 0), (0, 0)))
    return widths, (mu, w0, w2p, a0, a2p, g2, k_k, k_a, _head_ones())


def _rwkv_pre(p, prev, t_valid, weights, bt, tt):
    batch, t_pad, width = p.shape
    widths, consts = _rwkv_consts(weights)
    rw = widths[0]
    out_spec = pl.BlockSpec((bt, tt, rw), lambda b, t: (b, t, 0))
    return pl.pallas_call(
        functools.partial(_rwkv_pre_kernel, t_valid, t_pad, widths),
        grid=(batch // bt, t_pad // tt),
        in_specs=[pl.BlockSpec((bt, tt, width), lambda b, t: (b, t, 0)),
                  pl.BlockSpec((bt, 1, width), lambda b, t: (b, 0, 0))]
                 + [_const_spec(c.shape) for c in consts],
        out_specs=[out_spec] * 7,
        out_shape=[jax.ShapeDtypeStruct((batch, t_pad, rw), f32)] * 7,
        scratch_shapes=[pltpu.VMEM((bt, 1, width), f32)],
        compiler_params=_params("parallel", "arbitrary"),
        name="rwkv_pre",
    )(p, prev, *consts)


def _rwkv_scan_kernel(r_ref, k_ref, v_ref, kk_ref, b_ref, lw_ref, s0_ref, o_ref, s_out, state):
    ci = pl.program_id(1)
    bt, c, width = r_ref.shape
    n_pairs = width // LANES
    c2 = HEADS_PER_TILE * c
    n_double = int(math.log2(c))
    assert 1 << n_double == c and HEADS_PER_TILE == 2

    @pl.when(ci == 0)
    def _():
        state[...] = s0_ref[...]

    r2 = lax.broadcasted_iota(jnp.int32, (c2, c2), 0)
    q2 = lax.broadcasted_iota(jnp.int32, (c2, c2), 1)
    same_blk = (r2 >= c) == (q2 >= c)
    strict2 = (same_blk & (q2 < r2)).astype(f32)
    incl2 = (same_blk & (q2 <= r2)).astype(f32)
    eye2 = (q2 == r2).astype(f32)
    lane = lax.broadcasted_iota(jnp.int32, (1, LANES), 1)
    hm = [(lane // HEAD_DIM == j).astype(f32) for j in range(HEADS_PER_TILE)]
    bi_ = lax.broadcasted_iota(jnp.int32, (LANES, LANES), 0) // HEAD_DIM
    bj_ = lax.broadcasted_iota(jnp.int32, (LANES, LANES), 1) // HEAD_DIM
    block_diag = (bi_ == bj_).astype(f32)

    def stack(x):
        return jnp.concatenate([x * hm[0], x * hm[1]], axis=0)

    def twice(x):
        return jnp.concatenate([x, x], axis=0)

    def unstack(x2):
        return x2[:c] * hm[0] + x2[c:] * hm[1]

    probs = [(bi, pl.ds(pair * LANES, LANES), pair) for bi in range(bt) for pair in range(n_pairs)]
    ld = lambda ref: [ref[bi, :, sl] for bi, sl, _ in probs]
    r, k, v, kk, b, lw = ld(r_ref), ld(k_ref), ld(v_ref), ld(kk_ref), ld(b_ref), ld(lw_ref)
    nprob = len(probs)
    each = range(nprob)

    t_row = lax.broadcasted_iota(jnp.int32, (c, LANES), 0)
    cum = lw
    for step in range(n_double):
        cum = [cum[i] + jnp.where(t_row >= (1 << step), pltpu.roll(cum[i], 1 << step, axis=0), 0.0)
               for i in each]
    cum_end = [cum[i][c - 1:c, :] for i in each]
    e_neg = [jnp.exp(-cum[i]) for i in each]
    e_end = [jnp.exp(cum_end[i] - cum[i]) for i in each]
    a_mat = [kk[i] * jnp.exp(cum[i] - lw[i]) for i in each]
    p_mat = [r[i] * jnp.exp(cum[i]) for i in each]
    xa = [stack(a_mat[i]) for i in each]
    xap = [jnp.concatenate([xa[i], stack(p_mat[i])], axis=0) for i in each]
    yb = [stack(b[i] * e_neg[i]) for i in each]
    yk = [stack(k[i] * e_neg[i]) for i in each]
    v2 = [twice(v[i]) for i in each]

    gb = [_dot_lo(xap[i], yb[i], _NT) for i in each]
    gk = [_dot_lo(xap[i], yk[i], _NT) for i in each]
    l_mat = [gb[i][:c2] * strict2 for i in each]
    m_mat = [gk[i][:c2] * strict2 for i in each]
    lr_mat = [gb[i][c2:] * incl2 for i in each]
    mr_mat = [gk[i][c2:] * incl2 for i in each]

    npow = [-l_mat[i] for i in each]
    t_inv = [eye2 + npow[i] for i in each]
    for _ in range(n_double - 1):
        npow = [_dot_lo(npow[i], npow[i]) for i in each]
        t_inv = [t_inv[i] + _dot_lo(npow[i], t_inv[i]) for i in each]
    mv = [_dot_lo(m_mat[i], v2[i]) for i in each]
    tz = [_dot_lo(t_inv[i], jnp.concatenate([xa[i], mv[i]], axis=1)) for i in each]
    ta = [tz[i][:c, :LANES] + tz[i][c:, :LANES] for i in each]
    tmv = [unstack(tz[i][:, LANES:]) for i in each]

    s_prev = [state[bi, pair] for bi, _, pair in probs]
    xs = [_dot_lo(jnp.concatenate([ta[i], p_mat[i]], axis=0), s_prev[i], _NT) for i in each]
    u = [-(xs[i][:c] + tmv[i]) for i in each]
    corr = [_dot_lo(lr_mat[i], twice(u[i])) + _dot_lo(mr_mat[i], v2[i]) for i in each]
    upd = [_dot_lo(jnp.concatenate([u[i], v[i]], axis=0),
                   jnp.concatenate([b[i] * e_end[i], k[i] * e_end[i]], axis=0), _TN) for i in each]
    for i, (bi, sl, pair) in enumerate(probs):
        state[bi, pair] = s_prev[i] * jnp.exp(cum_end[i]) + upd[i] * block_diag
        o_ref[bi, :, sl] = xs[i][c:] + unstack(corr[i])

    @pl.when(ci == pl.num_programs(1) - 1)
    def _():
        s_out[...] = state[...]


def _rwkv_scan(r, k, v, kk, b, lw, s0, bt, chunk):
    batch, t_pad, width = r.shape
    n_pairs = width // LANES
    seq_spec = pl.BlockSpec((bt, chunk, width), lambda bb, c: (bb, c, 0))
    st_spec = pl.BlockSpec((bt, n_pairs, LANES, LANES), lambda bb, c: (bb, 0, 0, 0))
    return pl.pallas_call(
        _rwkv_scan_kernel,
        grid=(batch // bt, t_pad // chunk),
        in_specs=[seq_spec] * 6 + [st_spec],
        out_specs=[seq_spec, st_spec],
        out_shape=[jax.ShapeDtypeStruct((batch, t_pad, width), f32),
                   jax.ShapeDtypeStruct((batch, n_pairs, LANES, LANES), f32)],
        scratch_shapes=[pltpu.VMEM((bt, n_pairs, LANES, LANES), f32)],
        compiler_params=_params("parallel", "arbitrary"),
        name="rwkv_scan",
    )(r, k, v, kk, b, lw, s0)


def _pair_state(s):
    bsz, h, n, _ = s.shape
    s = s.reshape(bsz, h // HEADS_PER_TILE, HEADS_PER_TILE, n, n)
    eye = jnp.eye(HEADS_PER_TILE, dtype=s.dtype)
    out = jnp.einsum("bpjvk,ji->bpjvik", s, eye)
    return out.reshape(bsz, h // HEADS_PER_TILE, LANES, LANES)


def _unpair_state(s, heads):
    bsz = s.shape[0]
    s = s.reshape(bsz, heads // HEADS_PER_TILE, HEADS_PER_TILE, HEAD_DIM, HEADS_PER_TILE, HEAD_DIM)
    diag = jnp.stack([s[:, :, j, :, j, :] for j in range(HEADS_PER_TILE)], axis=2)
    return diag.reshape(bsz, heads, HEAD_DIM, HEAD_DIM)


STATE_ROWS = 8


def _rwkv_step_kernel(r_ref, k_ref, v_ref, kk_ref, b_ref, lw_ref, s_ref, o_ref, s_out):
    n_tok = r_ref.shape[0]
    n_val = s_ref.shape[1]
    decay = [jnp.exp(lw_ref[t, 0]) for t in range(n_tok)]

    def rows(c, _):
        base = pl.multiple_of(c * STATE_ROWS, STATE_ROWS)
        state = [s_ref[0, base + i] for i in range(STATE_ROWS)]
        for t in range(n_tok):
            kk, b, k, r = kk_ref[t, 0], b_ref[t, 0], k_ref[t, 0], r_ref[t, 0]
            v_rows = v_ref[t, 0, pl.ds(base, STATE_ROWS), :]
            out = []
            for i in range(STATE_ROWS):
                s_kk = jnp.sum(state[i] * kk, axis=0, keepdims=True)
                state[i] = state[i] * decay[t] - s_kk * b + v_rows[i:i + 1] * k
                out.append(jnp.sum(state[i] * r, axis=0, keepdims=True))
            o_ref[t, 0, pl.ds(base, STATE_ROWS), :] = jnp.concatenate(out, axis=0)
        for i in range(STATE_ROWS):
            s_out[0, base + i] = state[i]
        return 0

    lax.fori_loop(0, n_val // STATE_ROWS, rows, 0)


def _rwkv_step(r, k, v, kk, b, lw, s0):
    n_tok, n_heads, dim, batch = r.shape
    assert s0.shape == (n_heads, dim, dim, batch) and dim % STATE_ROWS == 0
    seq_spec = pl.BlockSpec((n_tok, 1, dim, batch), lambda h: (0, h, 0, 0))
    st_spec = pl.BlockSpec((1, dim, dim, batch), lambda h: (h, 0, 0, 0))
    return pl.pallas_call(
        _rwkv_step_kernel,
        grid=(n_heads,),
        in_specs=[seq_spec] * 6 + [st_spec],
        out_specs=[seq_spec, st_spec],
        out_shape=[jax.ShapeDtypeStruct(r.shape, f32), jax.ShapeDtypeStruct(s0.shape, f32)],
        compiler_params=_params("parallel"),
        name="rwkv_step",
    )(r, k, v, kk, b, lw, s0)


def _merge_kernel(h_ref, oa_ref, o_ref, r_ref, k_ref, v_ref, g_ref, ga_ref, gb_ref,
                  rk_ref, lnw_ref, lnb_ref, ones_ref, wba_ref, wbb_ref, wout_ref, gpost_ref,
                  out_ref):
    ones = ones_ref[...]
    inv = 1.0 / HEAD_DIM
    branch_a = ga_ref[...] * jnp.dot(oa_ref[...].astype(bf16), wba_ref[...], preferred_element_type=f32)
    o = o_ref[...]
    rkr = r_ref[...] * k_ref[...] * rk_ref[...]
    mean = _head_sums(o, ones) * inv
    bonus = _head_sums(rkr, ones) * v_ref[...]
    d = o - mean
    var = _head_sums(d * d, ones) * inv
    on = d * lax.rsqrt(var + GN_EPS) * lnw_ref[...] + lnb_ref[...]
    ob = (on + bonus) * g_ref[...]
    merged = branch_a + gb_ref[...] * jnp.dot(ob.astype(bf16), wbb_ref[...], preferred_element_type=f32)
    y = jnp.dot(merged.astype(bf16), wout_ref[...], preferred_element_type=f32)
    out_ref[...] = h_ref[...] + _rms(y, gpost_ref[...])


def _merge(h, oa, o, r, k, v, g, ga, gb, consts, tm):
    n, d = h.shape
    w = oa.shape[1]
    row = lambda width: pl.BlockSpec((tm, width), lambda i: (i, 0))
    return pl.pallas_call(
        _merge_kernel,
        grid=(n // tm,),
        in_specs=[row(d)] + [row(w)] * 6 + [row(d)] * 2 + [_const_spec(c.shape) for c in consts],
        out_specs=row(d),
        out_shape=jax.ShapeDtypeStruct((n, d), f32),
        compiler_params=_params("parallel"),
        name="merge",
    )(h, oa, o, r, k, v, g, ga, gb, *consts)


def _row_tile(n, target):
    t = min(n, target)
    assert n % t == 0
    return t


def _layer(x, prev_shift, wkv0, past, w, n_heads):
    (g1a, g1b, f1g, f1u, f1d, gma, gmb, w_in, w_ba, w_bb, w_out,
     mu, w0, w2, a0, a2, g2, k_k, k_a, r_k, ln_w, ln_b,
     g2a, g2b, f2g, f2u, f2d) = w
    batch, t, d = x.shape
    n = batch * t
    mw = w_ba.shape[0]
    rw = w_bb.shape[0]
    rp = mu.shape[-1]
    widths = (mw, mw, mw, rp, d, d)
    row = lambda a: a.reshape(1, -1)

    x2 = x.reshape(n, d)
    h = _ffn(x2, row(g1a), row(g1b), f1g, f1u, f1d, _row_tile(n, 512))
    kv_shape = (batch, t, n_heads, HEAD_DIM)
    pre_w = (row(mu), row(w0), w2, row(a0), a2, g2, row(k_k), row(k_a))
    prev = prev_shift.reshape(batch, 1, rp)
    if past is None:
        assert t % MOBA_BLOCK == 0 and t % RWKV_CHUNK == 0
        q, p_rw, ga, gb, k_t, v_t, k_bf, vt_bf, kmean, *pre = _in_proj_seq(
            h, row(gma), w_in, widths, t, prev, pre_w)
        o_a = _moba_prompt(q, k_bf, vt_bf, kmean.reshape(n // MOBA_BLOCK, mw), batch, t, n_heads)
        k, v = (jnp.transpose(a.reshape(batch, n_heads, HEAD_DIM, t), (0, 3, 1, 2)) for a in (k_t, v_t))
        r_, kh, v_, g_ = (pre[i] for i in (0, 1, 2, 6))
        o_scan, s_fin = _rwkv_scan(*(a.reshape(batch, t, rw) for a in pre[:6]),
                                   _pair_state(wkv0.astype(f32)), min(batch, 2), RWKV_CHUNK)
        o_scan = o_scan.reshape(n, rw)
        wkv = _unpair_state(s_fin, rw // HEAD_DIM).astype(wkv0.dtype)
    else:
        cache_k, cache_v, page_table = past
        q, p_rw, ga, gb, k, v = _in_proj(h, row(gma), w_in, widths, _row_tile(n, MOBA_BLOCK))
        rows3 = lambda a: a.reshape(batch, t, mw)
        o_a = _moba_sample(rows3(q), rows3(k), rows3(v), cache_k, cache_v, page_table).reshape(n, mw)
        k, v = k.reshape(kv_shape), v.reshape(kv_shape)
        t_pad = -(-t // 8) * 8
        p3p = jnp.pad(p_rw.reshape(batch, t, rp), ((0, 0), (0, t_pad - t), (0, 0)))
        pre = _rwkv_pre(p3p, prev, t, pre_w, min(batch, 16), t_pad)
        r_, kh, v_, g_ = (pre[i][:, :t].reshape(n, rw) for i in (0, 1, 2, 6))
        heads = rw // HEAD_DIM
        lanes_last = lambda a: jnp.transpose(a[:, :t], (1, 2, 0)).reshape(t, heads, HEAD_DIM, batch)
        o_scan, s_fin = _rwkv_step(*(lanes_last(a) for a in pre[:6]),
                                   jnp.transpose(wkv0.astype(f32), (1, 2, 3, 0)))
        o_scan = jnp.transpose(o_scan.reshape(t, rw, batch), (2, 0, 1)).reshape(n, rw)
        wkv = jnp.transpose(s_fin, (3, 0, 1, 2)).astype(wkv0.dtype)
    p3 = p_rw.reshape(batch, t, rp)
    consts = (row(r_k), row(ln_w), row(ln_b), _head_ones(), w_ba, w_bb, w_out, row(gmb))
    h2 = _merge(h, o_a, o_scan, r_, kh, v_, g_, ga, gb, consts, _row_tile(n, 512))
    y = _ffn(h2, row(g2a), row(g2b), f2g, f2u, f2d, _row_tile(n, 512))
    return y.reshape(batch, t, d), k, v, wkv, p3[:, -1]


def kernel(x_prompt, x_sample, cache_k, cache_v, state_wkv, state_shift, page_table,
           g_ffn1_pre, g_ffn1_post, w_ffn1_gate, w_ffn1_up, w_ffn1_down,
           g_mix_pre, g_mix_post, w_in, w_branch_a, w_branch_b, w_out,
           rwkv_mu, rwkv_w0, rwkv_w2, rwkv_a0, rwkv_a2, rwkv_g2, rwkv_k_k, rwkv_k_a, rwkv_r_k,
           rwkv_ln_w, rwkv_ln_b,
           g_ffn2_pre, g_ffn2_post, w_ffn2_gate, w_ffn2_up, w_ffn2_down):
    depth = w_in.shape[0]
    bp = x_prompt.shape[0]
    n_heads = cache_k.shape[3]
    rp = rwkv_mu.shape[-1]
    rw = rwkv_w0.shape[-1]
    yp, ys = x_prompt, x_sample
    outs = [[] for _ in range(8)]
    for i in range(depth):
        cast = lambda a: a[i].astype(bf16)
        w = (g_ffn1_pre[i], g_ffn1_post[i], cast(w_ffn1_gate), cast(w_ffn1_up), cast(w_ffn1_down),
             g_mix_pre[i], g_mix_post[i], cast(w_in), cast(w_branch_a), cast(w_branch_b), cast(w_out),
             rwkv_mu[i], rwkv_w0[i], rwkv_w2[i], rwkv_a0[i], rwkv_a2[i], rwkv_g2[i], rwkv_k_k[i],
             rwkv_k_a[i], rwkv_r_k[i], rwkv_ln_w[i], rwkv_ln_b[i],
             g_ffn2_pre[i], g_ffn2_post[i], cast(w_ffn2_gate), cast(w_ffn2_up), cast(w_ffn2_down))
        yp, kp, vp, wkvp, shp = _layer(
            yp, jnp.zeros((bp, rp), x_prompt.dtype),
            jnp.zeros((bp, rw // HEAD_DIM, HEAD_DIM, HEAD_DIM), state_wkv.dtype), None, w, n_heads)
        ys, ksn, vsn, wkvs, shs = _layer(
            ys, state_shift[i], state_wkv[i], (cache_k[i], cache_v[i], page_table), w, n_heads)
        for lst, val in zip(outs, (kp, vp, ksn, vsn, wkvp, shp, wkvs, shs)):
            lst.append(val)
    return (yp, ys) + tuple(jnp.stack(lst) for lst in outs)
```
